```python
import math
import jax, jax.numpy as jnp
from jax import lax
import numpy as np

D_MODEL = 2048
BATCH = 1
SEQ = 16384
DEPTH = 1

DN_HEADS = 8
DN_DK = 128
DN_DV = 128
DN_CONV = 4
DN_CHUNK = 64
ATT_HEADS = 8
ATT_DIM = 128
IDX_HEADS = 8
IDX_DIM = 64
TOPK_MAX = 256
Q_BLOCK = 128
ROPE_THETA = 500000.0
ROPE_FRACTION = 4
N_GROUPS = 8
EXPERTS_PER_GROUP = 8
N_EXPERTS = N_GROUPS * EXPERTS_PER_GROUP
TOPK_IN_GROUP = 2
EXPERT_HIDDEN = 512
MOE_BLOCK = 128
EPS = 1e-6

DN_QKV_COLS = 2 * DN_HEADS * DN_DK + DN_HEADS * DN_DV
SPLIT_SIZES = (
    DN_QKV_COLS,
    DN_HEADS * DN_DV,
    DN_HEADS,
    DN_HEADS,
    ATT_HEADS * ATT_DIM,
    ATT_HEADS * ATT_DIM,
    ATT_HEADS * ATT_DIM,
    IDX_HEADS * IDX_DIM,
    IDX_DIM,
    IDX_HEADS,
    2 * D_MODEL,
)
IN_COLS = sum(SPLIT_SIZES)

kernel_name = "hybrid_gdn_dsa_hmoe_block"


def rmsnorm(x, gain):
    xf = x.astype(jnp.float32)
    y = xf * lax.rsqrt(jnp.mean(xf * xf, axis=-1, keepdims=True) + EPS)
    return (y * gain.astype(jnp.float32)).astype(x.dtype)


def l2norm(x):
    xf = x.astype(jnp.float32)
    return (xf * lax.rsqrt(jnp.sum(xf * xf, axis=-1, keepdims=True) + EPS)).astype(x.dtype)


def partial_rope(x, positions):
    d = x.shape[-1]
    rot = d // ROPE_FRACTION
    half = rot // 2
    inv_freq = jnp.float32(ROPE_THETA) ** (-jnp.arange(half, dtype=jnp.float32) * 2.0 / rot)
    ang = positions.astype(jnp.float32)[..., None] * inv_freq
    cos = jnp.cos(ang)[:, :, None, :]
    sin = jnp.sin(ang)[:, :, None, :]
    xf = x.astype(jnp.float32)
    x1, x2 = xf[..., :half], xf[..., half:rot]
    out = jnp.concatenate([x1 * cos - x2 * sin, x2 * cos + x1 * sin, xf[..., rot:]], axis=-1)
    return out.astype(x.dtype)


def causal_depthwise_conv(x, w):
    c = x.shape[-1]
    y = lax.conv_general_dilated(
        x, w[:, None, :].astype(x.dtype), window_strides=(1,), padding=[(DN_CONV - 1, 0)],
        dimension_numbers=("NWC", "WIO", "NWC"), feature_group_count=c)
    return jax.nn.silu(y)


def chunked_gated_delta_rule(q, k, v, g, beta):
    B_, L, H, Dk = q.shape
    Dv = v.shape[-1]
    C = DN_CHUNK
    N = L // C
    f32 = jnp.float32

    def to_chunks(t):
        return t.astype(f32).reshape(B_, N, C, H, -1).transpose(0, 3, 1, 2, 4)

    q, k, v = to_chunks(q), to_chunks(k), to_chunks(v)
    g = g.astype(f32).reshape(B_, N, C, H).transpose(0, 3, 1, 2)
    beta = beta.astype(f32).reshape(B_, N, C, H).transpose(0, 3, 1, 2)
    G = jnp.cumsum(g, axis=-1)

    tri_incl = jnp.tril(jnp.ones((C, C), dtype=bool))
    tri_strict = jnp.tril(jnp.ones((C, C), dtype=bool), -1)
    decay = jnp.exp(jnp.where(tri_incl, G[..., :, None] - G[..., None, :], -jnp.inf))

    kk = jnp.einsum("bhnid,bhnjd->bhnij", k, k)
    A = jnp.where(tri_strict, beta[..., :, None] * kk * decay, 0.0)
    unit_lower = A + jnp.eye(C, dtype=f32)
    rhs = jnp.concatenate([k * (beta * jnp.exp(G))[..., None], v * beta[..., None]], axis=-1)
    sol = lax.linalg.triangular_solve(unit_lower, rhs, left_side=True, lower=True, unit_diagonal=True)
    w, u = sol[..., :Dk], sol[..., Dk:]

    attn = jnp.einsum("bhnid,bhnjd->bhnij", q, k) * decay
    q_dec = q * jnp.exp(G)[..., None]
    k_dec = k * jnp.exp(G[..., -1:] - G)[..., None]
    g_last = jnp.exp(G[..., -1])

    def step(S, inp):
        q_c, w_c, u_c, a_c, k_c, gl = inp
        v_new = u_c - jnp.einsum("bhcd,bhde->bhce", w_c, S)
        o = jnp.einsum("bhcd,bhde->bhce", q_c, S) + jnp.einsum("bhij,bhje->bhie", a_c, v_new)
        S = gl[..., None, None] * S + jnp.einsum("bhcd,bhce->bhde", k_c, v_new)
        return S, o

    xs = tuple(jnp.moveaxis(t, 2, 0) for t in (q_dec, w, u, attn, k_dec, g_last))
    S0 = jnp.zeros((B_, H, Dk, Dv), f32)
    _, o = lax.scan(step, S0, xs)
    return o.transpose(1, 0, 3, 2, 4).reshape(B_, L, H, Dv)


def gated_deltanet_branch(qkv, z, b, a, conv_w, a_log, dt_bias, norm_w):
    B_, L, _ = qkv.shape
    dtype = qkv.dtype
    qkv = causal_depthwise_conv(qkv, conv_w)
    q, k, v = jnp.split(qkv, [DN_HEADS * DN_DK, 2 * DN_HEADS * DN_DK], axis=-1)
    q = l2norm(q.reshape(B_, L, DN_HEADS, DN_DK)) * (DN_DK ** -0.5)
    k = l2norm(k.reshape(B_, L, DN_HEADS, DN_DK))
    v = v.reshape(B_, L, DN_HEADS, DN_DV)
    beta = jax.nn.sigmoid(b.astype(jnp.float32))
    g = -jnp.exp(a_log.astype(jnp.float32)) * jax.nn.softplus(a.astype(jnp.float32) + dt_bias.astype(jnp.float32))
    o = chunked_gated_delta_rule(q, k, v, g, beta)
    o = rmsnorm(o, norm_w) * jax.nn.silu(z.astype(jnp.float32).reshape(B_, L, DN_HEADS, DN_DV))
    return o.reshape(B_, L, DN_HEADS * DN_DV).astype(dtype)


def dsa_branch(q, k, v, q_idx, k_idx, w_idx, positions, idx_k_norm):
    B_, L, _ = q.shape
    dtype = q.dtype
    q = partial_rope(q.reshape(B_, L, ATT_HEADS, ATT_DIM), positions)
    k = partial_rope(k.reshape(B_, L, ATT_HEADS, ATT_DIM), positions)
    v = v.reshape(B_, L, ATT_HEADS, ATT_DIM)
    q_idx = partial_rope(q_idx.reshape(B_, L, IDX_HEADS, IDX_DIM), positions)
    k_idx = partial_rope(rmsnorm(k_idx, idx_k_norm)[:, :, None, :], positions)[:, :, 0, :]
    w_idx = w_idx * (IDX_HEADS ** -0.5 * IDX_DIM ** -0.5)

    topk = min(TOPK_MAX, L // 4)
    nb = L // Q_BLOCK
    key_pos = jnp.arange(L)
    scale = ATT_DIM ** -0.5

    def blocks(t):
        return jnp.moveaxis(t.reshape((B_, nb, Q_BLOCK) + t.shape[2:]), 1, 0)

    def one_block(args):
        i, qb, qib, wb = args
        qpos = i * Q_BLOCK + jnp.arange(Q_BLOCK)
        causal = key_pos[None, :] <= qpos[:, None]
        s_idx = jax.nn.relu(jnp.einsum("bqhd,bsd->bqhs", qib, k_idx))
        score = jnp.einsum("bqh,bqhs->bqs", wb, s_idx).astype(jnp.float32)
        score = jnp.where(causal[None], score, -jnp.inf)
        _, sel = lax.top_k(score, topk)
        valid = sel <= qpos[None, :, None]
        k_sel = jax.vmap(lambda kk, ii: kk[ii])(k, sel)
        v_sel = jax.vmap(lambda vv, ii: vv[ii])(v, sel)
        logits = jnp.einsum("bqhd,bqkhd->bhqk", qb, k_sel).astype(jnp.float32) * scale
        logits = jnp.where(valid[:, None], logits, -jnp.inf)
        p = jax.nn.softmax(logits, axis=-1)
        return jnp.einsum("bhqk,bqkhd->bqhd", p.astype(dtype), v_sel)

    out = lax.map(one_block, (jnp.arange(nb), blocks(q), blocks(q_idx), blocks(w_idx)))
    return jnp.moveaxis(out, 0, 1).reshape(B_, L, ATT_HEADS * ATT_DIM)


def hierarchical_moe(h, w_group, b_group, w_router, b_router, w_gate, w_up, w_down):
    B_, L, D = h.shape
    T = B_ * L
    hf = h.reshape(T, D)
    p_grp = jax.nn.softmax((hf @ w_group).astype(jnp.float32) + b_group.astype(jnp.float32), axis=-1)
    p_g_sel, g_sel = lax.top_k(p_grp, 1)
    e_logits = ((hf @ w_router).astype(jnp.float32) + b_router.astype(jnp.float32)).reshape(T, N_GROUPS, EXPERTS_PER_GROUP)
    e_logits = jnp.take_along_axis(e_logits, g_sel[:, :, None], axis=1)[:, 0]
    top_p, top_i = lax.top_k(jax.nn.softmax(e_logits, axis=-1), TOPK_IN_GROUP)
    weights = p_g_sel * top_p / jnp.sum(top_p, axis=-1, keepdims=True)
    expert = g_sel * EXPERTS_PER_GROUP + top_i

    A = T * TOPK_IN_GROUP
    flat_e = expert.reshape(A)
    flat_w = weights.reshape(A)
    flat_tok = jnp.repeat(jnp.arange(T, dtype=jnp.int32), TOPK_IN_GROUP)
    order = jnp.argsort(flat_e)
    se, stok, sw = flat_e[order], flat_tok[order], flat_w[order]
    counts = jnp.zeros((N_EXPERTS,), jnp.int32).at[flat_e].add(1)
    start = jnp.cumsum(counts) - counts
    padded = (counts + MOE_BLOCK - 1) // MOE_BLOCK * MOE_BLOCK
    pad_end = jnp.cumsum(padded)
    pad_start = pad_end - padded
    dest = pad_start[se] + (jnp.arange(A, dtype=jnp.int32) - start[se])
    P = (A + N_EXPERTS * (MOE_BLOCK - 1) + MOE_BLOCK - 1) // MOE_BLOCK * MOE_BLOCK
    n_blk = P // MOE_BLOCK
    tok_buf = jnp.full((P,), T, jnp.int32).at[dest].set(stok)
    w_buf = jnp.zeros((P,), jnp.float32).at[dest].set(sw)
    blk_e = jnp.minimum(jnp.searchsorted(pad_end, jnp.arange(n_blk, dtype=jnp.int32) * MOE_BLOCK, side="right"),
                        N_EXPERTS - 1)
    h_pad = jnp.concatenate([hf, jnp.zeros((1, D), hf.dtype)], axis=0)
    x_blocks = h_pad[tok_buf].reshape(n_blk, MOE_BLOCK, D)

    def expert_block(args):
        xb, e = args
        return (jax.nn.silu(xb @ w_gate[e]) * (xb @ w_up[e])) @ w_down[e]

    y = lax.map(expert_block, (x_blocks, blk_e)).reshape(P, D)
    out = jnp.zeros((T + 1, D), jnp.float32).at[tok_buf].add(y.astype(jnp.float32) * w_buf[:, None])[:T]
    return out.reshape(B_, L, D).astype(h.dtype)


def setup_inputs(seed: int = 0) -> dict:
    key = jax.random.key(seed)
    ks = jax.random.split(key, 24)
    f32 = jnp.float32
    nrm = lambda k, shape, s: jax.random.normal(k, shape, f32) * s
    dt = jnp.exp(jax.random.uniform(ks[6], (DEPTH, DN_HEADS), f32, math.log(1e-3), math.log(1e-1)))
    return {
        "x": jax.random.normal(ks[0], (BATCH, SEQ, D_MODEL), f32),
        "positions": jnp.broadcast_to(jnp.arange(SEQ, dtype=jnp.int32), (BATCH, SEQ)),
        "norm_mix": 1.0 + nrm(ks[1], (DEPTH, D_MODEL), 0.1),
        "w_in": nrm(ks[2], (DEPTH, D_MODEL, IN_COLS), D_MODEL ** -0.5),
        "b_gates": nrm(ks[3], (DEPTH, 2 * D_MODEL), 0.1),
        "dn_conv_w": nrm(ks[4], (DEPTH, DN_CONV, DN_QKV_COLS), DN_CONV ** -0.5),
        "dn_a_log": jnp.log(jax.random.uniform(ks[5], (DEPTH, DN_HEADS), f32, 1.0, 16.0)),
        "dn_dt_bias": dt + jnp.log(-jnp.expm1(-dt)),
        "dn_norm_w": 1.0 + nrm(ks[7], (DEPTH, DN_DV), 0.1),
        "idx_k_norm": 1.0 + nrm(ks[8], (DEPTH, IDX_DIM), 0.1),
        "w_proj_dn": nrm(ks[9], (DEPTH, DN_HEADS * DN_DV, D_MODEL), (DN_HEADS * DN_DV) ** -0.5),
        "w_proj_att": nrm(ks[10], (DEPTH, ATT_HEADS * ATT_DIM, D_MODEL), (ATT_HEADS * ATT_DIM) ** -0.5),
        "w_out": nrm(ks[11], (DEPTH, D_MODEL, D_MODEL), D_MODEL ** -0.5),
        "norm_ffn": 1.0 + nrm(ks[12], (DEPTH, D_MODEL), 0.1),
        "w_group": nrm(ks[13], (DEPTH, D_MODEL, N_GROUPS), D_MODEL ** -0.5),
        "b_group": nrm(ks[14], (DEPTH, N_GROUPS), 0.01),
        "w_router": nrm(ks[15], (DEPTH, D_MODEL, N_EXPERTS), D_MODEL ** -0.5),
        "b_router": nrm(ks[16], (DEPTH, N_EXPERTS), 0.01),
        "w_exp_gate": nrm(ks[17], (DEPTH, N_EXPERTS, D_MODEL, EXPERT_HIDDEN), D_MODEL ** -0.5),
        "w_exp_up": nrm(ks[18], (DEPTH, N_EXPERTS, D_MODEL, EXPERT_HIDDEN), D_MODEL ** -0.5),
        "w_exp_down": nrm(ks[19], (DEPTH, N_EXPERTS, EXPERT_HIDDEN, D_MODEL), EXPERT_HIDDEN ** -0.5),
        "norm_final": 1.0 + nrm(ks[20], (D_MODEL,), 0.1),
    }


def reference(x, positions, norm_mix, w_in, b_gates, dn_conv_w, dn_a_log, dn_dt_bias, dn_norm_w,
              idx_k_norm, w_proj_dn, w_proj_att, w_out, norm_ffn, w_group, b_group, w_router, b_router,
              w_exp_gate, w_exp_up, w_exp_down, norm_final):
    B_, L, _ = x.shape
    split_at = [int(s) for s in np.cumsum(SPLIT_SIZES)[:-1]]
    for l in range(DEPTH):
        h = rmsnorm(x, norm_mix[l])
        proj = h @ w_in[l]
        (dn_qkv, dn_z, dn_b, dn_a, at_q, at_k, at_v, ix_q, ix_k, ix_w, gate_cols) = jnp.split(proj, split_at, axis=-1)
        y_dn = gated_deltanet_branch(dn_qkv, dn_z, dn_b, dn_a, dn_conv_w[l], dn_a_log[l], dn_dt_bias[l], dn_norm_w[l])
        y_at = dsa_branch(at_q, at_k, at_v, ix_q, ix_k, ix_w, positions, idx_k_norm[l])
        gates = jax.nn.sigmoid((gate_cols + b_gates[l]).astype(jnp.float32)).reshape(B_, L, 2, D_MODEL)
        merged = (gates[:, :, 0] * (y_dn @ w_proj_dn[l]).astype(jnp.float32)
                  + gates[:, :, 1] * (y_at @ w_proj_att[l]).astype(jnp.float32)).astype(x.dtype)
        x = x + merged @ w_out[l]
        h2 = rmsnorm(x, norm_ffn[l])
        x = x + hierarchical_moe(h2, w_group[l], b_group[l], w_router[l], b_router[l],
                                 w_exp_gate[l], w_exp_up[l], w_exp_down[l])
    return rmsnorm(x, norm_final)
```

```python
import functools

import jax
import jax.numpy as jnp
import numpy as np
from jax import lax
from jax.experimental import pallas as pl
from jax.experimental.pallas import tpu as pltpu

D_MODEL = 2048
DN_HEADS = 8
DN_DK = 128
DN_DV = 128
DN_CONV = 4
DN_CHUNK = 64
ATT_HEADS = 8
ATT_DIM = 128
IDX_HEADS = 8
IDX_DIM = 64
TOPK_MAX = 256
ROPE_THETA = 500000.0
ROPE_FRACTION = 4
N_GROUPS = 8
EXPERTS_PER_GROUP = 8
N_EXPERTS = N_GROUPS * EXPERTS_PER_GROUP
EXPERT_HIDDEN = 512
EPS = 1e-6

LANES = 128
MOE_ROWS = 256
NEG_BIG = -1e30

F32 = jnp.float32
BF16 = jnp.bfloat16
I32 = jnp.int32

QKV_COLS = 2 * DN_HEADS * DN_DK + DN_HEADS * DN_DV
COL_QKV = 0
COL_Z = COL_QKV + QKV_COLS
COL_ATQ = COL_Z + DN_HEADS * DN_DV
COL_ATK = COL_ATQ + ATT_HEADS * ATT_DIM
COL_ATV = COL_ATK + ATT_HEADS * ATT_DIM
COL_IXQ = COL_ATV + ATT_HEADS * ATT_DIM
COL_GATE = COL_IXQ + IDX_HEADS * IDX_DIM
MAIN_COLS = COL_GATE + 2 * D_MODEL
SM_B = 0
SM_A = 8
SM_IXW = 16
SM_IXK = 64


def _params(sem, vmem_mb=48):
    return pltpu.CompilerParams(dimension_semantics=sem, vmem_limit_bytes=vmem_mb * 1024 * 1024)


def _sigmoid(x):
    return 1.0 / (1.0 + jnp.exp(-x))


def _dot(a, b):
    return jnp.dot(a, b, preferred_element_type=F32)


def _dot_nt(a, b):
    return lax.dot_general(a, b, (((1,), (1,)), ((), ())), preferred_element_type=F32)


def _proj_kernel(x_ref, g_ref, w_ref, ws_ref, o_ref, os_ref, h_ref):
    @pl.when(pl.program_id(1) == 0)
    def _():
        x = x_ref[...]
        h = x * lax.rsqrt(jnp.mean(x * x, axis=-1, keepdims=True) + EPS) * g_ref[...]
        h_ref[...] = h.astype(BF16)
        os_ref[...] = _dot(h_ref[...], ws_ref[...])

    o_ref[...] = _dot(h_ref[...], w_ref[...]).astype(o_ref.dtype)


def _proj_call(x2, gain, w_main, w_small):
    L, D = x2.shape
    N = w_main.shape[1]
    tm = min(1024, L)
    tn = 512
    return pl.pallas_call(
        _proj_kernel,
        grid=(L // tm, N // tn),
        in_specs=[
            pl.BlockSpec((tm, D), lambda i, j: (i, 0)),
            pl.BlockSpec((1, D), lambda i, j: (0, 0)),
            pl.BlockSpec((D, tn), lambda i, j: (0, j)),
            pl.BlockSpec((D, LANES), lambda i, j: (0, 0)),
        ],
        out_specs=[
            pl.BlockSpec((tm, tn), lambda i, j: (i, j)),
            pl.BlockSpec((tm, LANES), lambda i, j: (i, 0)),
        ],
        out_shape=[jax.ShapeDtypeStruct((L, N), BF16), jax.ShapeDtypeStruct((L, LANES), F32)],
        scratch_shapes=[pltpu.VMEM((tm, D), BF16)],
        compiler_params=_params(("parallel", "arbitrary")),
        name="proj",
    )(x2, gain, w_main, w_small)


def _gdn_prep_kernel(qkv_ref, halo_ref, sm_ref, cw_ref, alog_ref, dtb_ref,
                     q_ref, k_ref, v_ref, g_ref, b_ref, xs_ref):
    tT = qkv_ref.shape[0]
    first = pl.program_id(0) == 0
    xs_ref[8:8 + tT, :] = qkv_ref[...].astype(F32)
    halo = halo_ref[8:16, :].astype(F32)
    xs_ref[0:8, :] = jnp.where(first, 0.0, halo)
    nh = DN_HEADS * DN_DK
    for c in range(QKV_COLS // LANES):
        sl = slice(c * LANES, (c + 1) * LANES)
        y = xs_ref[8:8 + tT, sl] * cw_ref[3:4, sl]
        for j in range(DN_CONV - 1):
            y = y + xs_ref[5 + j:5 + j + tT, sl] * cw_ref[j:j + 1, sl]
        y = y * _sigmoid(y)
        if c < 2 * DN_HEADS:
            y = y * lax.rsqrt(jnp.sum(y * y, axis=-1, keepdims=True) + EPS)
            if c < DN_HEADS:
                q_ref[:, sl] = (y * (DN_DK ** -0.5)).astype(BF16)
            else:
                k_ref[:, c * LANES - nh:(c + 1) * LANES - nh] = y.astype(BF16)
        else:
            v_ref[:, c * LANES - 2 * nh:(c + 1) * LANES - 2 * nh] = y.astype(BF16)
    sm = sm_ref[...]
    b_ref[...] = _sigmoid(sm[:, SM_B:SM_B + DN_HEADS])
    a = sm[:, SM_A:SM_A + DN_HEADS] + dtb_ref[...]
    softplus = jnp.maximum(a, 0.0) + jnp.log1p(jnp.exp(-jnp.abs(a)))
    g_ref[...] = -jnp.exp(alog_ref[...]) * softplus


def _gdn_prep_call(proj, small, conv_w, a_log, dt_bias):
    L = proj.shape[0]
    tT = min(256, L)
    hb = tT // 16
    nh = DN_HEADS * DN_DK
    return pl.pallas_call(
        _gdn_prep_kernel,
        grid=(L // tT,),
        in_specs=[
            pl.BlockSpec((tT, QKV_COLS), lambda i: (i, 0)),
            pl.BlockSpec((16, QKV_COLS), lambda i: (jnp.maximum(i * hb - 1, 0), 0)),
            pl.BlockSpec((tT, LANES), lambda i: (i, 0)),
            pl.BlockSpec((DN_CONV, QKV_COLS), lambda i: (0, 0)),
            pl.BlockSpec((1, DN_HEADS), lambda i: (0, 0)),
            pl.BlockSpec((1, DN_HEADS), lambda i: (0, 0)),
        ],
        out_specs=[
            pl.BlockSpec((tT, nh), lambda i: (i, 0)),
            pl.BlockSpec((tT, nh), lambda i: (i, 0)),
            pl.BlockSpec((tT, nh), lambda i: (i, 0)),
            pl.BlockSpec((tT, DN_HEADS), lambda i: (i, 0)),
            pl.BlockSpec((tT, DN_HEADS), lambda i: (i, 0)),
        ],
        out_shape=[jax.ShapeDtypeStruct((L, nh), BF16)] * 3 + [jax.ShapeDtypeStruct((L, DN_HEADS), F32)] * 2,
        scratch_shapes=[pltpu.VMEM((tT + 8, QKV_COLS), F32)],
        compiler_params=_params(("parallel",)),
        name="gdn_prep",
    )(proj, proj, small, conv_w, a_log, dt_bias)


GDN_STEP = 2 * DN_CHUNK


def _gdn_chunk_kernel(q_ref, k_ref, v_ref, z_ref, gc_ref, gr_ref, bc_ref, nw_ref, y_ref, s_ref):
    C = DN_CHUNK

    @pl.when(pl.program_id(0) == 0)
    def _():
        s_ref[...] = jnp.zeros_like(s_ref)

    row = lax.broadcasted_iota(I32, (C, C), 0)
    col = lax.broadcasted_iota(I32, (C, C), 1)
    incl = row >= col
    strict = row > col
    tri = incl.astype(F32)
    tri_t = (row <= col).astype(F32)
    eye = (row == col).astype(F32)
    nw = nw_ref[...]
    for c in range(GDN_STEP // C):
        rs = slice(c * C, (c + 1) * C)
        G_col = jnp.dot(tri, gc_ref[rs, :], preferred_element_type=F32, precision=lax.Precision.HIGHEST)
        G_row = jnp.dot(gr_ref[:, rs], tri_t, preferred_element_type=F32, precision=lax.Precision.HIGHEST)
        beta = bc_ref[rs, :]
        for h in range(DN_HEADS):
            hs = slice(h * DN_DK, (h + 1) * DN_DK)
            Gc = G_col[:, h:h + 1]
            Gr = G_row[h:h + 1, :]
            Gl = G_col[C - 1:C, h:h + 1]
            bcol = beta[:, h:h + 1]
            qh = q_ref[rs, hs]
            kh = k_ref[rs, hs]
            vh = v_ref[rs, hs]
            decay = jnp.exp(jnp.where(incl, Gc - Gr, -jnp.inf))
            kk = _dot_nt(kh, kh)
            A = jnp.where(strict, bcol * kk * decay, 0.0)
            M = -A
            T = eye + M
            for _ in range(5):
                Mb = M.astype(BF16)
                M = _dot(Mb, Mb)
                T = T + _dot(T.astype(BF16), M.astype(BF16))
            Tb = T.astype(BF16)
            eg = jnp.exp(Gc)
            kf = kh.astype(F32)
            w = _dot(Tb, (kf * (bcol * eg)).astype(BF16))
            u = _dot(Tb, (vh.astype(F32) * bcol).astype(BF16))
            attn = _dot_nt(qh, kh) * decay
            q_dec = (qh.astype(F32) * eg).astype(BF16)
            k_dec = (kf * jnp.exp(Gl - Gc)).astype(BF16)
            S = s_ref[h]
            Sb = S.astype(BF16)
            v_new = u - _dot(w.astype(BF16), Sb)
            vb = v_new.astype(BF16)
            o = _dot(q_dec, Sb) + _dot(attn.astype(BF16), vb)
            s_ref[h] = jnp.exp(Gl) * S + lax.dot_general(
                k_dec, vb, (((0,), (0,)), ((), ())), preferred_element_type=F32)
            on = o * lax.rsqrt(jnp.mean(o * o, axis=-1, keepdims=True) + EPS) * nw
            zz = z_ref[rs, hs].astype(F32)
            y_ref[rs, hs] = (on * (zz * _sigmoid(zz))).astype(BF16)


def _gdn_chunk_call(q, k, v, proj, g, g_t, beta, norm_w):
    L, nh = q.shape
    R = GDN_STEP
    zc = COL_Z // nh
    return pl.pallas_call(
        _gdn_chunk_kernel,
        grid=(L // R,),
        in_specs=[
            pl.BlockSpec((R, nh), lambda i: (i, 0)),
            pl.BlockSpec((R, nh), lambda i: (i, 0)),
            pl.BlockSpec((R, nh), lambda i: (i, 0)),
            pl.BlockSpec((R, nh), lambda i: (i, zc)),
            pl.BlockSpec((R, DN_HEADS), lambda i: (i, 0)),
            pl.BlockSpec((DN_HEADS, R), lambda i: (0, i)),
            pl.BlockSpec((R, DN_HEADS), lambda i: (i, 0)),
            pl.BlockSpec((1, DN_DV), lambda i: (0, 0)),
        ],
        out_specs=pl.BlockSpec((R, nh), lambda i: (i, 0)),
        out_shape=jax.ShapeDtypeStruct((L, nh), BF16),
        scratch_shapes=[pltpu.VMEM((DN_HEADS, DN_DK, DN_DV), F32)],
        compiler_params=_params(("arbitrary",)),
        name="gdn_chunk",
    )(q, k, v, proj, g, g_t, beta, norm_w)


def _rope_tables(pos, period, lane):
    rot = period // ROPE_FRACTION
    half = rot // 2
    lp = lane % period
    expo = -((lp % half).astype(F32) * 2.0 / rot)
    inv_freq = jnp.power(jnp.float32(ROPE_THETA), expo)
    ang = pos * inv_freq
    cos, sin = jnp.cos(ang), jnp.sin(ang)
    c = jnp.where(lp < rot, cos, 1.0)
    s_lo = jnp.where(lp < half, -sin, 0.0)
    s_hi = jnp.where((lp >= half) & (lp < rot), sin, 0.0)
    return c, s_lo, s_hi, half


def _rope(x, tab):
    c, s_lo, s_hi, half = tab
    return x * c + pltpu.roll(x, LANES - half, 1) * s_lo + pltpu.roll(x, half, 1) * s_hi


def _dsa_prep_kernel(q_ref, k_ref, iq_ref, sm_ref, pos_ref, kn_ref, qo_ref, ko_ref, iqo_ref, iko_ref, wo_ref):
    lane = lax.broadcasted_iota(I32, (1, LANES), 1)
    pos = pos_ref[...]
    tab_att = _rope_tables(pos, ATT_DIM, lane)
    tab_idx = _rope_tables(pos, IDX_DIM, lane)
    scale = ATT_DIM ** -0.5
    for h in range(ATT_HEADS):
        hs = slice(h * ATT_DIM, (h + 1) * ATT_DIM)
        qo_ref[:, hs] = (_rope(q_ref[:, hs].astype(F32), tab_att) * scale).astype(BF16)
        ko_ref[:, hs] = _rope(k_ref[:, hs].astype(F32), tab_att).astype(BF16)
    low = lane < IDX_DIM
    for p in range(IDX_HEADS // 2):
        x = _rope(iq_ref[:, p * LANES:(p + 1) * LANES].astype(F32), tab_idx)
        iqo_ref[:, (2 * p) * LANES:(2 * p + 1) * LANES] = jnp.where(low, x, 0.0).astype(BF16)
        iqo_ref[:, (2 * p + 1) * LANES:(2 * p + 2) * LANES] = jnp.where(low, pltpu.roll(x, IDX_DIM, 1), 0.0).astype(BF16)
    sm = sm_ref[...]
    kx = jnp.where(low, 0.0, sm)
    kx = kx * lax.rsqrt(jnp.sum(kx * kx, axis=-1, keepdims=True) * (1.0 / IDX_DIM) + EPS) * kn_ref[...]
    kx = _rope(kx, tab_idx)
    iko_ref[...] = jnp.where(low, pltpu.roll(kx, IDX_DIM, 1), 0.0).astype(BF16)
    wo_ref[...] = sm * (IDX_HEADS ** -0.5 * IDX_DIM ** -0.5)


def _dsa_prep_call(proj, small, pos_col, kn_lanes):
    L = proj.shape[0]
    tT = min(256, L)
    na = ATT_HEADS * ATT_DIM
    ni = IDX_HEADS * IDX_DIM
    return pl.pallas_call(
        _dsa_prep_kernel,
        grid=(L // tT,),
        in_specs=[
            pl.BlockSpec((tT, na), lambda i: (i, COL_ATQ // na)),
            pl.BlockSpec((tT, na), lambda i: (i, COL_ATK // na)),
            pl.BlockSpec((tT, ni), lambda i: (i, COL_IXQ // ni)),
            pl.BlockSpec((tT, LANES), lambda i: (i, 0)),
            pl.BlockSpec((tT, 1), lambda i: (i, 0)),
            pl.BlockSpec((1, LANES), lambda i: (0, 0)),
        ],
        out_specs=[
            pl.BlockSpec((tT, na), lambda i: (i, 0)),
            pl.BlockSpec((tT, na), lambda i: (i, 0)),
            pl.BlockSpec((tT, IDX_HEADS * LANES), lambda i: (i, 0)),
            pl.BlockSpec((tT, LANES), lambda i: (i, 0)),
            pl.BlockSpec((tT, LANES), lambda i: (i, 0)),
        ],
        out_shape=[jax.ShapeDtypeStruct((L, na), BF16), jax.ShapeDtypeStruct((L, na), BF16),
                   jax.ShapeDtypeStruct((L, IDX_HEADS * LANES), BF16), jax.ShapeDtypeStruct((L, LANES), BF16),
                   jax.ShapeDtypeStruct((L, LANES), F32)],
        compiler_params=_params(("parallel",)),
        name="dsa_prep",
    )(proj, proj, proj, small, pos_col, kn_lanes)


def _index_scores(iq_heads, w_cols, ik_blk):
    acc = None
    for qh, wh in zip(iq_heads, w_cols):
        term = wh * jnp.maximum(_dot_nt(qh, ik_blk), 0.0)
        acc = term if acc is None else acc + term
    return acc


def _sortable(bits):
    return jnp.where(bits < 0, bits ^ jnp.int32(0x7FFFFFFF), bits)


def _score_keys(s):
    key = _sortable(pltpu.bitcast(s, I32))
    return jnp.where(key == -1, 0, key)


THR_TQ = 128
THR_TK = 512
KEY_NEG_INF = int(np.int32(np.array(-np.inf, np.float32).view(np.int32)) ^ np.int32(0x7FFFFFFF))
INT_MIN = -(2 ** 31)
INT_MAX = 2 ** 31 - 1


def _dsa_thresh_kernel(iq_ref, w_ref, ik_ref, thr_ref, jt_ref, sc_ref, *, topk, tk):
    TQ = iq_ref.shape[0]
    L = ik_ref.shape[0]
    t0 = pl.program_id(0) * TQ
    nkt = (t0 + TQ + tk - 1) // tk
    iq_heads = [iq_ref[:, h * LANES:(h + 1) * LANES] for h in range(IDX_HEADS)]
    w_cols = [w_ref[:, SM_IXW + h:SM_IXW + h + 1] for h in range(IDX_HEADS)]
    qpos = t0 + lax.broadcasted_iota(I32, (TQ, tk), 0)
    kcol = lax.broadcasted_iota(I32, (TQ, tk), 1)

    def fill(kt, carry):
        off = pl.multiple_of(kt * tk, tk)
        s = _index_scores(iq_heads, w_cols, ik_ref[pl.ds(off, tk), :])
        s = jnp.where(kcol + off <= qpos, s, -jnp.inf)
        sc_ref[:, pl.ds(off, tk)] = _score_keys(s)
        return carry

    lax.fori_loop(0, nkt, fill, 0)

    def count(pred):
        def body(kt, acc):
            off = pl.multiple_of(kt * tk, tk)
            m = pred(sc_ref[:, pl.ds(off, tk)], kcol + off).astype(I32)
            for c in range(tk // LANES):
                acc = acc + m[:, c * LANES:(c + 1) * LANES]
            return acc
        acc = lax.fori_loop(0, nkt, body, jnp.zeros((TQ, LANES), I32))
        return jnp.sum(acc, axis=1, keepdims=True)

    def bit_cond(st):
        b, _, cnt = st
        return (b >= 0) & (jnp.max(jnp.abs(cnt - topk)) > 0)

    def bit_body(st):
        b, theta_u, cnt = st
        cand_u = theta_u | (jnp.int32(1) << b)
        cand = cand_u ^ jnp.int32(INT_MIN)
        c = count(lambda keys, _: keys >= cand)
        ok = c >= topk
        return b - 1, jnp.where(ok, cand_u, theta_u), jnp.where(ok, c, cnt)

    cnt0 = jnp.zeros((TQ, 1), I32) + nkt * tk
    _, theta_u, cnt = lax.while_loop(bit_cond, bit_body, (jnp.int32(31), jnp.zeros((TQ, 1), I32), cnt0))
    theta = theta_u ^ jnp.int32(INT_MIN)
    thr_ref[...] = pltpu.bitcast(_sortable(theta), F32)
    jt_ref[...] = jnp.full((TQ, 1), INT_MAX, I32)

    tied = (cnt > topk) & (theta > KEY_NEG_INF)

    @pl.when(jnp.max(tied.astype(I32)) > 0)
    def _():
        above = count(lambda keys, _: keys > theta)
        need = topk - above

        def idx_body(i, m):
            cand = m | (jnp.int32(1) << (nbits - 1 - i))
            c = count(lambda keys, idx: (keys == theta) & (idx < cand))
            return jnp.where(c <= need - 1, cand, m)

        nbits = max(int(L - 1).bit_length(), 1)
        m = lax.fori_loop(0, nbits, idx_body, jnp.zeros((TQ, 1), I32))
        jt_ref[...] = jnp.where(tied, m, INT_MAX)


def _dsa_thresh_call(iq, ik, wi, topk):
    L = iq.shape[0]
    TQ = min(THR_TQ, L)
    tk = min(THR_TK, L)
    return pl.pallas_call(
        functools.partial(_dsa_thresh_kernel, topk=topk, tk=tk),
        grid=(L // TQ,),
        in_specs=[
            pl.BlockSpec((TQ, IDX_HEADS * LANES), lambda i: (i, 0)),
            pl.BlockSpec((TQ, LANES), lambda i: (i, 0)),
            pl.BlockSpec((L, LANES), lambda i: (0, 0)),
        ],
        out_specs=[pl.BlockSpec((TQ, 1), lambda i: (i, 0)), pl.BlockSpec((TQ, 1), lambda i: (i, 0))],
        out_shape=[jax.ShapeDtypeStruct((L, 1), F32), jax.ShapeDtypeStruct((L, 1), I32)],
        scratch_shapes=[pltpu.VMEM((TQ, L), I32)],
        compiler_params=_params(("parallel",)),
        name="dsa_thresh",
    )(iq, wi, ik)


ATT_TQ = 256
ATT_TK = 512


def _dsa_attn_kernel(q_ref, k_ref, v_ref, iq_ref, w_ref, ik_ref, thr_ref, jt_ref, o_ref, m_ref, l_ref, acc_ref):
    TQ, TK = q_ref.shape[0], k_ref.shape[0]
    i, kt = pl.program_id(0), pl.program_id(1)
    t0 = i * TQ
    k0 = kt * TK

    @pl.when(kt == 0)
    def _():
        m_ref[...] = jnp.full_like(m_ref, NEG_BIG)
        l_ref[...] = jnp.zeros_like(l_ref)
        acc_ref[...] = jnp.zeros_like(acc_ref)

    @pl.when(k0 <= t0 + TQ - 1)
    def _():
        iq_heads = [iq_ref[:, h * LANES:(h + 1) * LANES] for h in range(IDX_HEADS)]
        w_cols = [w_ref[:, SM_IXW + h:SM_IXW + h + 1] for h in range(IDX_HEADS)]
        score = _index_scores(iq_heads, w_cols, ik_ref[...])
        qpos = t0 + lax.broadcasted_iota(I32, (TQ, TK), 0)
        kpos = k0 + lax.broadcasted_iota(I32, (TQ, TK), 1)
        thr = thr_ref[...]
        keep = (kpos <= qpos) & ((score > thr) | ((score == thr) & (kpos <= jt_ref[...])))
        for h in range(ATT_HEADS):
            hs = slice(h * ATT_DIM, (h + 1) * ATT_DIM)
            s = jnp.where(keep, _dot_nt(q_ref[:, hs], k_ref[:, hs]), NEG_BIG)
            m_old = m_ref[:, h:h + 1]
            m_new = jnp.maximum(m_old, jnp.max(s, axis=-1, keepdims=True))
            alpha = jnp.exp(m_old - m_new)
            p = jnp.exp(s - m_new)
            l_ref[:, h:h + 1] = alpha * l_ref[:, h:h + 1] + jnp.sum(p, axis=-1, keepdims=True)
            acc_ref[:, hs] = alpha * acc_ref[:, hs] + _dot(p.astype(BF16), v_ref[:, hs])
            m_ref[:, h:h + 1] = m_new

    @pl.when(kt == pl.num_programs(1) - 1)
    def _():
        for h in range(ATT_HEADS):
            hs = slice(h * ATT_DIM, (h + 1) * ATT_DIM)
            o_ref[:, hs] = (acc_ref[:, hs] / l_ref[:, h:h + 1]).astype(BF16)


def _dsa_attn_call(q, k, proj, iq, ik, wi, thr, jt):
    L, na = q.shape
    TQ = min(ATT_TQ, L)
    TK = min(ATT_TK, L)
    last = lambda i: (i * TQ + TQ - 1) // TK
    kv_map = lambda i, j: (jnp.minimum(j, last(i)), 0)
    return pl.pallas_call(
        _dsa_attn_kernel,
        grid=(L // TQ, L // TK),
        in_specs=[
            pl.BlockSpec((TQ, na), lambda i, j: (i, 0)),
            pl.BlockSpec((TK, na), kv_map),
            pl.BlockSpec((TK, na), lambda i, j: (jnp.minimum(j, last(i)), COL_ATV // na)),
            pl.BlockSpec((TQ, IDX_HEADS * LANES), lambda i, j: (i, 0)),
            pl.BlockSpec((TQ, LANES), lambda i, j: (i, 0)),
            pl.BlockSpec((TK, LANES), kv_map),
            pl.BlockSpec((TQ, 1), lambda i, j: (i, 0)),
            pl.BlockSpec((TQ, 1), lambda i, j: (i, 0)),
        ],
        out_specs=pl.BlockSpec((TQ, na), lambda i, j: (i, 0)),
        out_shape=jax.ShapeDtypeStruct((L, na), BF16),
        scratch_shapes=[pltpu.VMEM((TQ, LANES), F32), pltpu.VMEM((TQ, LANES), F32), pltpu.VMEM((TQ, na), F32)],
        compiler_params=_params(("parallel", "arbitrary")),
        name="dsa_attn",
    )(q, k, proj, iq, wi, ik, thr, jt)


def _merge_kernel(g0_ref, g1_ref, b0_ref, b1_ref, ydn_ref, yat_ref, wdn_ref, wat_ref, o_ref):
    gate0 = _sigmoid(g0_ref[...].astype(F32) + b0_ref[...])
    gate1 = _sigmoid(g1_ref[...].astype(F32) + b1_ref[...])
    merged = gate0 * _dot(ydn_ref[...], wdn_ref[...]) + gate1 * _dot(yat_ref[...], wat_ref[...])
    o_ref[...] = merged.astype(o_ref.dtype)


def _merge_call(proj, b_gates, y_dn, y_at, w_dn, w_at):
    L = proj.shape[0]
    tm = min(1024, L)
    tn = 512
    nj = D_MODEL // tn
    c0 = COL_GATE // tn
    kd = y_dn.shape[1]
    return pl.pallas_call(
        _merge_kernel,
        grid=(L // tm, nj),
        in_specs=[
            pl.BlockSpec((tm, tn), lambda i, j: (i, c0 + j)),
            pl.BlockSpec((tm, tn), lambda i, j: (i, c0 + nj + j)),
            pl.BlockSpec((1, tn), lambda i, j: (0, j)),
            pl.BlockSpec((1, tn), lambda i, j: (0, nj + j)),
            pl.BlockSpec((tm, kd), lambda i, j: (i, 0)),
            pl.BlockSpec((tm, kd), lambda i, j: (i, 0)),
            pl.BlockSpec((kd, tn), lambda i, j: (0, j)),
            pl.BlockSpec((kd, tn), lambda i, j: (0, j)),
        ],
        out_specs=pl.BlockSpec((tm, tn), lambda i, j: (i, j)),
        out_shape=jax.ShapeDtypeStruct((L, D_MODEL), BF16),
        compiler_params=_params(("parallel", "arbitrary")),
        name="merge",
    )(proj, proj, b_gates, b_gates, y_dn, y_at, w_dn, w_at)


RT_E0, RT_E1, RT_W0, RT_W1 = 0, 1, 2, 3


def _first_lane_of_max(v, lane):
    m = jnp.max(v, axis=-1, keepdims=True)
    return m, jnp.min(jnp.where(v == m, lane, LANES), axis=-1, keepdims=True)


def _outproj_kernel(x_ref, mg_ref, wo_ref, nf_ref, wr_ref, br_ref, x1_ref, h2_ref, rt_ref):
    x1 = x_ref[...] + _dot(mg_ref[...], wo_ref[...])
    x1_ref[...] = x1
    h2 = x1 * lax.rsqrt(jnp.mean(x1 * x1, axis=-1, keepdims=True) + EPS) * nf_ref[...]
    h2_ref[...] = h2
    lg = _dot(h2.astype(BF16), wr_ref[...]) + br_ref[...]
    lane = lax.broadcasted_iota(I32, (1, LANES), 1)
    ninf = -jnp.inf
    gl = jnp.where(lane < N_GROUPS, lg, ninf)
    gmax, g_sel = _first_lane_of_max(gl, lane)
    p_group = 1.0 / jnp.sum(jnp.exp(gl - gmax), axis=-1, keepdims=True)
    ex = lane - N_GROUPS
    in_group = (ex >= 0) & (ex < N_EXPERTS) & ((ex // EXPERTS_PER_GROUP) == g_sel)
    el = jnp.where(in_group, lg, ninf)
    m1, i1 = _first_lane_of_max(el, lane)
    m2, i2 = _first_lane_of_max(jnp.where(lane == i1, ninf, el), lane)
    e2 = jnp.exp(m2 - m1)
    w0 = p_group / (1.0 + e2)
    w1 = p_group * e2 / (1.0 + e2)
    rec = jnp.where(lane == RT_E0, (i1 - N_GROUPS).astype(F32), 0.0)
    rec = jnp.where(lane == RT_E1, (i2 - N_GROUPS).astype(F32), rec)
    rec = jnp.where(lane == RT_W0, w0, rec)
    rt_ref[...] = jnp.where(lane == RT_W1, w1, rec)


def _outproj_call(x2, merged, w_out, norm_ffn, w_route, b_route):
    L = x2.shape[0]
    tm = min(256, L)
    row = lambda i: (i, 0)
    fixed = lambda i: (0, 0)
    return pl.pallas_call(
        _outproj_kernel,
        grid=(L // tm,),
        in_specs=[
            pl.BlockSpec((tm, D_MODEL), row),
            pl.BlockSpec((tm, D_MODEL), row),
            pl.BlockSpec((D_MODEL, D_MODEL), fixed),
            pl.BlockSpec((1, D_MODEL), fixed),
            pl.BlockSpec((D_MODEL, LANES), fixed),
            pl.BlockSpec((1, LANES), fixed),
        ],
        out_specs=[pl.BlockSpec((tm, D_MODEL), row), pl.BlockSpec((tm, D_MODEL), row), pl.BlockSpec((tm, LANES), row)],
        out_shape=[jax.ShapeDtypeStruct((L, D_MODEL), F32), jax.ShapeDtypeStruct((L, D_MODEL), F32),
                   jax.ShapeDtypeStruct((L, LANES), F32)],
        compiler_params=_params(("parallel",)),
        name="outproj_route",
    )(x2, merged, w_out, norm_ffn, w_route, b_route)


def _moe_slots(L):
    return -(-(2 * L + N_EXPERTS * (MOE_ROWS - 1)) // MOE_ROWS) * MOE_ROWS


def _moe_plan_kernel(rt_ref, dest_ref, blk_ref, cnt_ref, carry_ref, start_ref):
    phase, i = pl.program_id(0), pl.program_id(1)
    tR = rt_ref.shape[0]
    lane = lax.broadcasted_iota(I32, (1, LANES), 1)
    lane_f = lane.astype(F32)
    rt = rt_ref[...]
    e0, e1 = rt[:, RT_E0:RT_E0 + 1], rt[:, RT_E1:RT_E1 + 1]
    hit0, hit1 = lane_f == e0, lane_f == e1
    onehot = (hit0 | hit1).astype(F32)
    colsum = jnp.sum(onehot, axis=0, keepdims=True)

    @pl.when((phase == 0) & (i == 0))
    def _():
        cnt_ref[...] = jnp.zeros_like(cnt_ref)

    @pl.when(phase == 0)
    def _():
        cnt_ref[...] += colsum

    @pl.when((phase == 1) & (i == 0))
    def _():
        carry_ref[...] = jnp.zeros_like(carry_ref)
        padded = jnp.floor((cnt_ref[...] + (MOE_ROWS - 1)) * (1.0 / MOE_ROWS)) * MOE_ROWS
        r = lax.broadcasted_iota(I32, (LANES, LANES), 0)
        c = lax.broadcasted_iota(I32, (LANES, LANES), 1)
        upper = (r <= c).astype(F32)
        end = jnp.dot(jnp.broadcast_to(padded, (8, LANES)), upper, preferred_element_type=F32,
                      precision=lax.Precision.HIGHEST)[0:1, :]
        start_ref[...] = end - padded
        n_used = end[:, N_EXPERTS - 1:N_EXPERTS] * (1.0 / MOE_ROWS)
        nb = blk_ref.shape[0]
        b = lax.broadcasted_iota(I32, (nb, 1), 0).astype(F32)
        b_eff = jnp.minimum(b, n_used - 1.0)
        done = ((end <= b_eff * MOE_ROWS) & (lane < N_EXPERTS)).astype(F32)
        blk_e = jnp.minimum(jnp.sum(done, axis=-1, keepdims=True), N_EXPERTS - 1.0)
        blk_ref[...] = jnp.where(lane == 0, blk_e, jnp.where(lane == 1, n_used, 0.0)).astype(I32)

    @pl.when(phase == 1)
    def _():
        r = lax.broadcasted_iota(I32, (tR, tR), 0)
        c = lax.broadcasted_iota(I32, (tR, tR), 1)
        before = _dot((r > c).astype(BF16), onehot.astype(BF16)) + carry_ref[...]
        slot = before + start_ref[...]
        d0 = jnp.sum(jnp.where(hit0, slot, 0.0), axis=-1, keepdims=True)
        d1 = jnp.sum(jnp.where(hit1, slot, 0.0), axis=-1, keepdims=True)
        dest_ref[...] = jnp.where(lane == 0, d0, jnp.where(lane == 1, d1, 0.0)).astype(I32)
        carry_ref[...] += colsum


def _moe_plan_call(route):
    L = route.shape[0]
    tR = min(256, L)
    nb = _moe_slots(L) // MOE_ROWS
    nb_pad = -(-nb // 8) * 8
    return pl.pallas_call(
        _moe_plan_kernel,
        grid=(2, L // tR),
        in_specs=[pl.BlockSpec((tR, LANES), lambda p, i: (i, 0))],
        out_specs=[pl.BlockSpec((tR, LANES), lambda p, i: (p * i, 0)), pl.BlockSpec((nb_pad, LANES), lambda p, i: (0, 0))],
        out_shape=[jax.ShapeDtypeStruct((L, LANES), I32), jax.ShapeDtypeStruct((nb_pad, LANES), I32)],
        scratch_shapes=[pltpu.VMEM((1, LANES), F32)] * 3,
        compiler_params=_params(("arbitrary", "arbitrary")),
        name="moe_plan",
    )(route)


def _moe_scatter_kernel(dest_ref, h_hbm, xs_in, xs_out, sem, *, rows):
    del xs_in
    base = pl.program_id(0) * rows

    def row_copy(t, j):
        return pltpu.make_async_copy(h_hbm.at[pl.ds(t, 1)], xs_out.at[pl.ds(dest_ref[2 * t + j], 1)], sem)

    def start(r, c):
        row_copy(base + r, 0).start()
        row_copy(base + r, 1).start()
        return c

    def wait(r, c):
        row_copy(base + r, 0).wait()
        row_copy(base + r, 1).wait()
        return c

    lax.fori_loop(0, rows, start, 0)
    lax.fori_loop(0, rows, wait, 0)


def _moe_scatter_call(dest_flat, h2, xs_zero):
    L = h2.shape[0]
    rows = min(256, L)
    return pl.pallas_call(
        functools.partial(_moe_scatter_kernel, rows=rows),
        grid_spec=pltpu.PrefetchScalarGridSpec(
            num_scalar_prefetch=1,
            grid=(L // rows,),
            in_specs=[pl.BlockSpec(memory_space=pl.ANY), pl.BlockSpec(memory_space=pl.ANY)],
            out_specs=pl.BlockSpec(memory_space=pl.ANY),
            scratch_shapes=[pltpu.SemaphoreType.DMA(())],
        ),
        out_shape=jax.ShapeDtypeStruct(xs_zero.shape, xs_zero.dtype),
        input_output_aliases={2: 0},
        compiler_params=_params(("arbitrary",)),
        name="moe_scatter",
    )(dest_flat, h2, xs_zero)


def _moe_ffn_kernel(be_ref, x_ref, wg_ref, wu_ref, wd_ref, y_ref, *, nb):
    b = pl.program_id(0)

    @pl.when(b < be_ref[nb])
    def _():
        xb = x_ref[...].astype(BF16)
        gate = _dot(xb, wg_ref[...].astype(BF16))
        up = _dot(xb, wu_ref[...].astype(BF16))
        hidden = (gate * _sigmoid(gate) * up).astype(BF16)
        y_ref[...] = _dot(hidden, wd_ref[...].astype(BF16))

    @pl.when(b >= be_ref[nb])
    def _():
        y_ref[...] = jnp.zeros_like(y_ref)


def _moe_ffn_call(blk_e, xs, w_gate, w_up, w_down):
    P = xs.shape[0]
    nb = P // MOE_ROWS
    return pl.pallas_call(
        functools.partial(_moe_ffn_kernel, nb=nb),
        grid_spec=pltpu.PrefetchScalarGridSpec(
            num_scalar_prefetch=1,
            grid=(nb,),
            in_specs=[
                pl.BlockSpec((MOE_ROWS, D_MODEL), lambda b, be: (b, 0)),
                pl.BlockSpec((None, D_MODEL, EXPERT_HIDDEN), lambda b, be: (be[b], 0, 0)),
                pl.BlockSpec((None, D_MODEL, EXPERT_HIDDEN), lambda b, be: (be[b], 0, 0)),
                pl.BlockSpec((None, EXPERT_HIDDEN, D_MODEL), lambda b, be: (be[b], 0, 0)),
            ],
            out_specs=pl.BlockSpec((MOE_ROWS, D_MODEL), lambda b, be: (b, 0)),
        ),
        out_shape=jax.ShapeDtypeStruct((P, D_MODEL), F32),
        compiler_params=_params(("arbitrary",), vmem_mb=56),
        name="moe_ffn",
    )(blk_e, xs, w_gate, w_up, w_down)


def _moe_combine_kernel(dest_ref, x1_ref, rt_ref, nf_ref, ys_hbm, o_ref, ya_ref, yb_ref, sem, *, rows):
    base = pl.program_id(0) * rows

    def row_copy(r, j):
        dst = (ya_ref, yb_ref)[j]
        return pltpu.make_async_copy(ys_hbm.at[pl.ds(dest_ref[2 * (base + r) + j], 1)], dst.at[pl.ds(r, 1)], sem)

    def start(r, c):
        row_copy(r, 0).start()
        row_copy(r, 1).start()
        return c

    def wait(r, c):
        row_copy(r, 0).wait()
        row_copy(r, 1).wait()
        return c

    lax.fori_loop(0, rows, start, 0)
    lax.fori_loop(0, rows, wait, 0)
    rt = rt_ref[...]
    x = x1_ref[...] + rt[:, RT_W0:RT_W0 + 1] * ya_ref[...] + rt[:, RT_W1:RT_W1 + 1] * yb_ref[...]
    o_ref[...] = x * lax.rsqrt(jnp.mean(x * x, axis=-1, keepdims=True) + EPS) * nf_ref[...]


def _moe_combine_call(dest_flat, x1, route, norm_final, ys):
    L = x1.shape[0]
    rows = min(256, L)
    return pl.pallas_call(
        functools.partial(_moe_combine_kernel, rows=rows),
        grid_spec=pltpu.PrefetchScalarGridSpec(
            num_scalar_prefetch=1,
            grid=(L // rows,),
            in_specs=[
                pl.BlockSpec((rows, D_MODEL), lambda i, d: (i, 0)),
                pl.BlockSpec((rows, LANES), lambda i, d: (i, 0)),
                pl.BlockSpec((1, D_MODEL), lambda i, d: (0, 0)),
                pl.BlockSpec(memory_space=pl.ANY),
            ],
            out_specs=pl.BlockSpec((rows, D_MODEL), lambda i, d: (i, 0)),
            scratch_shapes=[pltpu.VMEM((rows, D_MODEL), F32), pltpu.VMEM((rows, D_MODEL), F32),
                            pltpu.SemaphoreType.DMA(())],
        ),
        out_shape=jax.ShapeDtypeStruct((L, D_MODEL), F32),
        compiler_params=_params(("arbitrary",)),
        name="moe_combine",
    )(dest_flat, x1, route, norm_final, ys)


def _pack_w_in(w):
    s = np.cumsum((0, QKV_COLS, DN_HEADS * DN_DV, DN_HEADS, DN_HEADS, ATT_HEADS * ATT_DIM, ATT_HEADS * ATT_DIM,
                   ATT_HEADS * ATT_DIM, IDX_HEADS * IDX_DIM, IDX_DIM, IDX_HEADS, 2 * D_MODEL))
    seg = lambda n: w[:, int(s[n]):int(s[n + 1])]
    w_main = jnp.concatenate([seg(0), seg(1), seg(4), seg(5), seg(6), seg(7), seg(10)], axis=1).astype(BF16)
    pad = jnp.zeros((w.shape[0], SM_IXK - SM_IXW - IDX_HEADS), w.dtype)
    w_small = jnp.concatenate([seg(2), seg(3), seg(9), pad, seg(8)], axis=1).astype(BF16)
    return w_main, w_small


def _forward(x, positions, norm_mix, w_in, b_gates, dn_conv_w, dn_a_log, dn_dt_bias, dn_norm_w, idx_k_norm,
             w_proj_dn, w_proj_att, w_out, norm_ffn, w_group, b_group, w_router, b_router, w_exp_gate, w_exp_up,
             w_exp_down, norm_final):
    st = {}
    L = x.shape[1]
    x2 = x.reshape(L, D_MODEL)
    w_main, w_small = _pack_w_in(w_in[0])
    proj, small = _proj_call(x2, norm_mix[0].reshape(1, D_MODEL), w_main, w_small)
    st["proj"], st["small"] = proj, small
    q, k, v, g, beta = _gdn_prep_call(proj, small, dn_conv_w[0], dn_a_log[0].reshape(1, DN_HEADS),
                                      dn_dt_bias[0].reshape(1, DN_HEADS))
    st["y_dn"] = _gdn_chunk_call(q, k, v, proj, g, g.T, beta, dn_norm_w[0].reshape(1, DN_DV))

    pos_col = positions.reshape(L, 1).astype(F32)
    kn_lanes = jnp.concatenate([jnp.zeros((SM_IXK,), F32), idx_k_norm[0].astype(F32)]).reshape(1, LANES)
    aq, ak, iq, ik, wi = _dsa_prep_call(proj, small, pos_col, kn_lanes)
    thr, jt = _dsa_thresh_call(iq, ik, wi, min(TOPK_MAX, L // 4))
    st["thr"], st["jt"] = thr, jt
    st["y_at"] = _dsa_attn_call(aq, ak, proj, iq, ik, wi, thr, jt)

    merged = _merge_call(proj, b_gates[0].reshape(1, 2 * D_MODEL), st["y_dn"], st["y_at"],
                         w_proj_dn[0].astype(BF16), w_proj_att[0].astype(BF16))
    st["merged"] = merged
    n_route = N_GROUPS + N_EXPERTS
    w_route = jnp.concatenate([w_group[0], w_router[0], jnp.zeros((D_MODEL, LANES - n_route), F32)], axis=1).astype(BF16)
    b_route = jnp.concatenate([b_group[0], b_router[0], jnp.zeros((LANES - n_route,), F32)]).reshape(1, LANES)
    x1, h2, route = _outproj_call(x2, merged, w_out[0].astype(BF16), norm_ffn[0].reshape(1, D_MODEL), w_route, b_route)
    st["x1"], st["h2"], st["route"] = x1, h2, route

    dest, blk = _moe_plan_call(route)
    dest_flat = dest[:, :2].reshape(2 * L)
    slots = _moe_slots(L)
    nb = slots // MOE_ROWS
    blk_e = jnp.concatenate([blk[:nb, 0], blk[0:1, 1]])
    xs = _moe_scatter_call(dest_flat, h2, jnp.zeros((slots, D_MODEL), F32))
    ys = _moe_ffn_call(blk_e, xs, w_exp_gate[0], w_exp_up[0], w_exp_down[0])
    out = _moe_combine_call(dest_flat, x1, route, norm_final.reshape(1, D_MODEL), ys)
    st["out"] = out.reshape(1, L, D_MODEL)
    return st


def kernel(x, positions, norm_mix, w_in, b_gates, dn_conv_w, dn_a_log, dn_dt_bias, dn_norm_w, idx_k_norm, w_proj_dn,
           w_proj_att, w_out, norm_ffn, w_group, b_group, w_router, b_router, w_exp_gate, w_exp_up, w_exp_down,
           norm_final):
    return _forward(x, positions, norm_mix, w_in, b_gates, dn_conv_w, dn_a_log, dn_dt_bias, dn_norm_w, idx_k_norm,
                    w_proj_dn, w_proj_att, w_out, norm_ffn, w_group, b_group, w_router, b_router, w_exp_gate,
                    w_exp_up, w_exp_down, norm_final)["out"]


def _stages(d, upto=None):
    return _forward(*[d[n] for n in ("x", "positions", "norm_mix", "w_in", "b_gates", "dn_conv_w", "dn_a_log",
                                     "dn_dt_bias", "dn_norm_w", "idx_k_norm", "w_proj_dn", "w_proj_att", "w_out",
                                     "norm_ffn", "w_group", "b_group", "w_router", "b_router", "w_exp_gate",
                                     "w_exp_up", "w_exp_down", "norm_final")])
```

```python
import functools

import jax
import jax.numpy as jnp
import numpy as np
from jax import lax
from jax.experimental import pallas as pl
from jax.experimental.pallas import tpu as pltpu

D_MODEL = 2048
DN_HEADS = 8
DN_DK = 128
DN_DV = 128
DN_CONV = 4
DN_CHUNK = 64
ATT_HEADS = 8
ATT_DIM = 128
IDX_HEADS = 8
IDX_DIM = 64
TOPK_MAX = 256
ROPE_THETA = 500000.0
ROPE_FRACTION = 4
N_GROUPS = 8
EXPERTS_PER_GROUP = 8
N_EXPERTS = N_GROUPS * EXPERTS_PER_GROUP
EXPERT_HIDDEN = 512
EPS = 1e-6

LANES = 128
MOE_ROWS = 256
NEG_BIG = -1e30

F32 = jnp.float32
BF16 = jnp.bfloat16
I32 = jnp.int32

QKV_COLS = 2 * DN_HEADS * DN_DK + DN_HEADS * DN_DV
COL_QKV = 0
COL_Z = COL_QKV + QKV_COLS
COL_ATQ = COL_Z + DN_HEADS * DN_DV
COL_ATK = COL_ATQ + ATT_HEADS * ATT_DIM
COL_ATV = COL_ATK + ATT_HEADS * ATT_DIM
COL_IXQ = COL_ATV + ATT_HEADS * ATT_DIM
COL_GATE = COL_IXQ + IDX_HEADS * IDX_DIM
MAIN_COLS = COL_GATE + 2 * D_MODEL
SM_B = 0
SM_A = 8
SM_IXW = 16
SM_IXK = 64


def _params(sem, vmem_mb=48):
    return pltpu.CompilerParams(dimension_semantics=sem, vmem_limit_bytes=vmem_mb * 1024 * 1024)


def _sigmoid(x):
    return 1.0 / (1.0 + jnp.exp(-x))


def _dot(a, b):
    return jnp.dot(a, b, preferred_element_type=F32)


def _dot_nt(a, b):
    return lax.dot_general(a, b, (((1,), (1,)), ((), ())), preferred_element_type=F32)


def _proj_kernel(x_ref, g_ref, w_ref, ws_ref, o_ref, os_ref, h_ref):
    @pl.when(pl.program_id(1) == 0)
    def _():
        x = x_ref[...]
        h = x * lax.rsqrt(jnp.mean(x * x, axis=-1, keepdims=True) + EPS) * g_ref[...]
        h_ref[...] = h.astype(BF16)
        os_ref[...] = _dot(h_ref[...], ws_ref[...])

    o_ref[...] = _dot(h_ref[...], w_ref[...]).astype(o_ref.dtype)


def _proj_call(x2, gain, w_main, w_small):
    L, D = x2.shape
    N = w_main.shape[1]
    tm = min(1024, L)
    tn = 512
    return pl.pallas_call(
        _proj_kernel,
        grid=(L // tm, N // tn),
        in_specs=[
            pl.BlockSpec((tm, D), lambda i, j: (i, 0)),
            pl.BlockSpec((1, D), lambda i, j: (0, 0)),
            pl.BlockSpec((D, tn), lambda i, j: (0, j)),
            pl.BlockSpec((D, LANES), lambda i, j: (0, 0)),
        ],
        out_specs=[
            pl.BlockSpec((tm, tn), lambda i, j: (i, j)),
            pl.BlockSpec((tm, LANES), lambda i, j: (i, 0)),
        ],
        out_shape=[jax.ShapeDtypeStruct((L, N), BF16), jax.ShapeDtypeStruct((L, LANES), F32)],
        scratch_shapes=[pltpu.VMEM((tm, D), BF16)],
        compiler_params=_params(("parallel", "arbitrary")),
        name="proj",
    )(x2, gain, w_main, w_small)


def _gdn_prep_kernel(qkv_ref, halo_ref, sm_ref, cw_ref, alog_ref, dtb_ref,
                     q_ref, k_ref, v_ref, g_ref, b_ref, xs_ref):
    tT = qkv_ref.shape[0]
    first = pl.program_id(0) == 0
    xs_ref[8:8 + tT, :] = qkv_ref[...].astype(F32)
    halo = halo_ref[8:16, :].astype(F32)
    xs_ref[0:8, :] = jnp.where(first, 0.0, halo)
    nh = DN_HEADS * DN_DK
    for c in range(QKV_COLS // LANES):
        sl = slice(c * LANES, (c + 1) * LANES)
        y = xs_ref[8:8 + tT, sl] * cw_ref[3:4, sl]
        for j in range(DN_CONV - 1):
            y = y + xs_ref[5 + j:5 + j + tT, sl] * cw_ref[j:j + 1, sl]
        y = y * _sigmoid(y)
        if c < 2 * DN_HEADS:
            y = y * lax.rsqrt(jnp.sum(y * y, axis=-1, keepdims=True) + EPS)
            if c < DN_HEADS:
                q_ref[:, sl] = (y * (DN_DK ** -0.5)).astype(BF16)
            else:
                k_ref[:, c * LANES - nh:(c + 1) * LANES - nh] = y.astype(BF16)
        else:
            v_ref[:, c * LANES - 2 * nh:(c + 1) * LANES - 2 * nh] = y.astype(BF16)
    sm = sm_ref[...]
    b_ref[...] = _sigmoid(sm[:, SM_B:SM_B + DN_HEADS])
    a = sm[:, SM_A:SM_A + DN_HEADS] + dtb_ref[...]
    softplus = jnp.maximum(a, 0.0) + jnp.log1p(jnp.exp(-jnp.abs(a)))
    g_ref[...] = -jnp.exp(alog_ref[...]) * softplus


def _gdn_prep_call(proj, small, conv_w, a_log, dt_bias):
    L = proj.shape[0]
    tT = min(256, L)
    hb = tT // 16
    nh = DN_HEADS * DN_DK
    return pl.pallas_call(
        _gdn_prep_kernel,
        grid=(L // tT,),
        in_specs=[
            pl.BlockSpec((tT, QKV_COLS), lambda i: (i, 0)),
            pl.BlockSpec((16, QKV_COLS), lambda i: (jnp.maximum(i * hb - 1, 0), 0)),
            pl.BlockSpec((tT, LANES), lambda i: (i, 0)),
            pl.BlockSpec((DN_CONV, QKV_COLS), lambda i: (0, 0)),
            pl.BlockSpec((1, DN_HEADS), lambda i: (0, 0)),
            pl.BlockSpec((1, DN_HEADS), lambda i: (0, 0)),
        ],
        out_specs=[
            pl.BlockSpec((tT, nh), lambda i: (i, 0)),
            pl.BlockSpec((tT, nh), lambda i: (i, 0)),
            pl.BlockSpec((tT, nh), lambda i: (i, 0)),
            pl.BlockSpec((tT, DN_HEADS), lambda i: (i, 0)),
            pl.BlockSpec((tT, DN_HEADS), lambda i: (i, 0)),
        ],
        out_shape=[jax.ShapeDtypeStruct((L, nh), BF16)] * 3 + [jax.ShapeDtypeStruct((L, DN_HEADS), F32)] * 2,
        scratch_shapes=[pltpu.VMEM((tT + 8, QKV_COLS), F32)],
        compiler_params=_params(("parallel",)),
        name="gdn_prep",
    )(proj, proj, small, conv_w, a_log, dt_bias)


GDN_STEP = 4 * DN_CHUNK


def _bdot(a, b):
    return lax.dot_general(a, b, (((2,), (1,)), ((0,), (0,))), preferred_element_type=F32)


def _bdot_nt(a, b):
    return lax.dot_general(a, b, (((2,), (2,)), ((0,), (0,))), preferred_element_type=F32)


def _gdn_chunk_kernel(q_ref, k_ref, v_ref, z_ref, gc_ref, gr_ref, bc_ref, nw_ref, y_ref, s_ref):
    C, H = DN_CHUNK, DN_HEADS
    n_chunks = q_ref.shape[0] // C

    @pl.when(pl.program_id(0) == 0)
    def _():
        s_ref[...] = jnp.zeros_like(s_ref)

    row = lax.broadcasted_iota(I32, (C, C), 0)
    col = lax.broadcasted_iota(I32, (C, C), 1)
    incl = row >= col
    strict = row > col
    tri = incl.astype(F32)
    tri_t = (row <= col).astype(F32)
    eye = (row == col).astype(F32)

    def per_head(fn):
        return jnp.stack([fn(slice(c * C, (c + 1) * C), h) for c in range(n_chunks) for h in range(H)])

    head = lambda ref: per_head(lambda rs, h: ref[rs, h * DN_DK:(h + 1) * DN_DK])
    qb, kb, vb = head(q_ref), head(k_ref), head(v_ref)
    G_col = [jnp.dot(tri, gc_ref[c * C:(c + 1) * C, :], preferred_element_type=F32, precision=lax.Precision.HIGHEST)
             for c in range(n_chunks)]
    G_row = [jnp.dot(gr_ref[:, c * C:(c + 1) * C], tri_t, preferred_element_type=F32, precision=lax.Precision.HIGHEST)
             for c in range(n_chunks)]
    Gc = jnp.stack([G_col[c][:, h:h + 1] for c in range(n_chunks) for h in range(H)])
    Gr = jnp.stack([G_row[c][h:h + 1, :] for c in range(n_chunks) for h in range(H)])
    Gl = jnp.stack([G_col[c][C - 1:C, h:h + 1] for c in range(n_chunks) for h in range(H)])
    bcol = per_head(lambda rs, h: bc_ref[rs, h:h + 1])

    decay = jnp.exp(jnp.where(incl[None], Gc - Gr, -jnp.inf))
    A = jnp.where(strict[None], bcol * _bdot_nt(kb, kb) * decay, 0.0)
    M = -A
    T = eye[None] + M
    for _ in range(5):
        Mb = M.astype(BF16)
        M = _bdot(Mb, Mb)
        T = T + _bdot(T.astype(BF16), M.astype(BF16))
    Tb = T.astype(BF16)
    eg = jnp.exp(Gc)
    kf = kb.astype(F32)
    w = _bdot(Tb, (kf * (bcol * eg)).astype(BF16)).astype(BF16)
    u = _bdot(Tb, (vb.astype(F32) * bcol).astype(BF16))
    attn = (_bdot_nt(qb, kb) * decay).astype(BF16)
    q_dec = (qb.astype(F32) * eg).astype(BF16)
    k_dec = (kf * jnp.exp(Gl - Gc)).astype(BF16)
    g_last = jnp.exp(Gl)

    nw = nw_ref[...]
    S = s_ref[...]
    for c in range(n_chunks):
        rs = slice(c * C, (c + 1) * C)
        bs = slice(c * H, (c + 1) * H)
        Sb = S.astype(BF16)
        v_new = (u[bs] - _bdot(w[bs], Sb)).astype(BF16)
        o = _bdot(q_dec[bs], Sb) + _bdot(attn[bs], v_new)
        S = g_last[bs] * S + jnp.stack([
            lax.dot_general(k_dec[c * H + h], v_new[h], (((0,), (0,)), ((), ())), preferred_element_type=F32)
            for h in range(H)])
        on = o * lax.rsqrt(jnp.mean(o * o, axis=-1, keepdims=True) + EPS) * nw
        for h in range(H):
            hs = slice(h * DN_DV, (h + 1) * DN_DV)
            zz = z_ref[rs, hs].astype(F32)
            y_ref[rs, hs] = (on[h] * (zz * _sigmoid(zz))).astype(BF16)
    s_ref[...] = S


def _gdn_chunk_call(q, k, v, proj, g, g_t, beta, norm_w):
    L, nh = q.shape
    R = GDN_STEP
    zc = COL_Z // nh
    return pl.pallas_call(
        _gdn_chunk_kernel,
        grid=(L // R,),
        in_specs=[
            pl.BlockSpec((R, nh), lambda i: (i, 0)),
            pl.BlockSpec((R, nh), lambda i: (i, 0)),
            pl.BlockSpec((R, nh), lambda i: (i, 0)),
            pl.BlockSpec((R, nh), lambda i: (i, zc)),
            pl.BlockSpec((R, DN_HEADS), lambda i: (i, 0)),
            pl.BlockSpec((DN_HEADS, R), lambda i: (0, i)),
            pl.BlockSpec((R, DN_HEADS), lambda i: (i, 0)),
            pl.BlockSpec((1, DN_DV), lambda i: (0, 0)),
        ],
        out_specs=pl.BlockSpec((R, nh), lambda i: (i, 0)),
        out_shape=jax.ShapeDtypeStruct((L, nh), BF16),
        scratch_shapes=[pltpu.VMEM((DN_HEADS, DN_DK, DN_DV), F32)],
        compiler_params=_params(("arbitrary",)),
        name="gdn_chunk",
    )(q, k, v, proj, g, g_t, beta, norm_w)


def _rope_tables(pos, period, lane):
    rot = period // ROPE_FRACTION
    half = rot // 2
    lp = lane % period
    expo = -((lp % half).astype(F32) * 2.0 / rot)
    inv_freq = jnp.power(jnp.float32(ROPE_THETA), expo)
    ang = pos * inv_freq
    cos, sin = jnp.cos(ang), jnp.sin(ang)
    c = jnp.where(lp < rot, cos, 1.0)
    s_lo = jnp.where(lp < half, -sin, 0.0)
    s_hi = jnp.where((lp >= half) & (lp < rot), sin, 0.0)
    return c, s_lo, s_hi, half


def _rope(x, tab):
    c, s_lo, s_hi, half = tab
    return x * c + pltpu.roll(x, LANES - half, 1) * s_lo + pltpu.roll(x, half, 1) * s_hi


def _dsa_prep_kernel(q_ref, k_ref, v_ref, iq_ref, sm_ref, pos_ref, kn_ref,
                     qo_ref, ko_ref, vto_ref, iqo_ref, iko_ref, wto_ref):
    lane = lax.broadcasted_iota(I32, (1, LANES), 1)
    pos = pos_ref[...]
    tab_att = _rope_tables(pos, ATT_DIM, lane)
    tab_idx = _rope_tables(pos, IDX_DIM, lane)
    scale = ATT_DIM ** -0.5
    for h in range(ATT_HEADS):
        hs = slice(h * ATT_DIM, (h + 1) * ATT_DIM)
        qo_ref[:, hs] = (_rope(q_ref[:, hs].astype(F32), tab_att) * scale).astype(BF16)
        ko_ref[:, hs] = _rope(k_ref[:, hs].astype(F32), tab_att).astype(BF16)
        vto_ref[hs, :] = v_ref[:, hs].astype(F32).T.astype(BF16)
    low = lane < IDX_DIM
    for p in range(IDX_HEADS // 2):
        x = _rope(iq_ref[:, p * LANES:(p + 1) * LANES].astype(F32), tab_idx)
        iqo_ref[:, (2 * p) * LANES:(2 * p + 1) * LANES] = jnp.where(low, x, 0.0).astype(BF16)
        iqo_ref[:, (2 * p + 1) * LANES:(2 * p + 2) * LANES] = jnp.where(low, pltpu.roll(x, IDX_DIM, 1), 0.0).astype(BF16)
    sm = sm_ref[...]
    kx = jnp.where(low, 0.0, sm)
    kx = kx * lax.rsqrt(jnp.sum(kx * kx, axis=-1, keepdims=True) * (1.0 / IDX_DIM) + EPS) * kn_ref[...]
    kx = _rope(kx, tab_idx)
    iko_ref[...] = jnp.where(low, pltpu.roll(kx, IDX_DIM, 1), 0.0).astype(BF16)
    wto_ref[...] = (sm * (IDX_HEADS ** -0.5 * IDX_DIM ** -0.5)).T[SM_IXW:SM_IXW + IDX_HEADS, :]


def _dsa_prep_call(proj, small, pos_col, kn_lanes):
    L = proj.shape[0]
    tT = min(256, L)
    na = ATT_HEADS * ATT_DIM
    ni = IDX_HEADS * IDX_DIM
    return pl.pallas_call(
        _dsa_prep_kernel,
        grid=(L // tT,),
        in_specs=[
            pl.BlockSpec((tT, na), lambda i: (i, COL_ATQ // na)),
            pl.BlockSpec((tT, na), lambda i: (i, COL_ATK // na)),
            pl.BlockSpec((tT, na), lambda i: (i, COL_ATV // na)),
            pl.BlockSpec((tT, ni), lambda i: (i, COL_IXQ // ni)),
            pl.BlockSpec((tT, LANES), lambda i: (i, 0)),
            pl.BlockSpec((tT, 1), lambda i: (i, 0)),
            pl.BlockSpec((1, LANES), lambda i: (0, 0)),
        ],
        out_specs=[
            pl.BlockSpec((tT, na), lambda i: (i, 0)),
            pl.BlockSpec((tT, na), lambda i: (i, 0)),
            pl.BlockSpec((na, tT), lambda i: (0, i)),
            pl.BlockSpec((tT, IDX_HEADS * LANES), lambda i: (i, 0)),
            pl.BlockSpec((tT, LANES), lambda i: (i, 0)),
            pl.BlockSpec((IDX_HEADS, tT), lambda i: (0, i)),
        ],
        out_shape=[jax.ShapeDtypeStruct((L, na), BF16), jax.ShapeDtypeStruct((L, na), BF16),
                   jax.ShapeDtypeStruct((na, L), BF16),
                   jax.ShapeDtypeStruct((L, IDX_HEADS * LANES), BF16), jax.ShapeDtypeStruct((L, LANES), BF16),
                   jax.ShapeDtypeStruct((IDX_HEADS, L), F32)],
        compiler_params=_params(("parallel",)),
        name="dsa_prep",
    )(proj, proj, proj, proj, small, pos_col, kn_lanes)


def _index_scores_t(ik_blk, iq_heads, w_rows):
    acc = None
    for qh, wh in zip(iq_heads, w_rows):
        term = wh * jnp.maximum(_dot_nt(ik_blk, qh), 0.0)
        acc = term if acc is None else acc + term
    return acc


def _sortable(bits):
    return jnp.where(bits < 0, bits ^ jnp.int32(0x7FFFFFFF), bits)


def _score_keys(s):
    key = _sortable(pltpu.bitcast(s, I32))
    return jnp.where(key == -1, 0, key)


DSA_TQ = 256
DSA_TK = 512
KEY_NEG_INF = int(np.int32(np.array(-np.inf, np.float32).view(np.int32)) ^ np.int32(0x7FFFFFFF))
INT_MIN = -(2 ** 31)
INT_MAX = 2 ** 31 - 1


def _dsa_kernel(q_ref, iq_ref, wt_ref, ik_ref, k_hbm, vt_hbm, o_ref,
                sc_ref, kbuf, vbuf, sem, jt_ref, *acc_refs, topk, tk):
    TQ = q_ref.shape[0]
    L = ik_ref.shape[0]
    t0 = pl.program_id(0) * TQ
    nkt = (t0 + TQ + tk - 1) // tk

    def kv_copies(kt, slot):
        off = pl.multiple_of(kt * tk, tk)
        return (pltpu.make_async_copy(k_hbm.at[pl.ds(off, tk)], kbuf.at[slot], sem.at[0, slot]),
                pltpu.make_async_copy(vt_hbm.at[:, pl.ds(off, tk)], vbuf.at[slot], sem.at[1, slot]))

    for cp in kv_copies(0, 0):
        cp.start()

    iq_heads = [iq_ref[:, h * LANES:(h + 1) * LANES] for h in range(IDX_HEADS)]
    w_rows = [wt_ref[h:h + 1, :] for h in range(IDX_HEADS)]
    qpos = t0 + lax.broadcasted_iota(I32, (tk, TQ), 1)
    krow = lax.broadcasted_iota(I32, (tk, TQ), 0)

    def fill(kt, carry):
        m1, m2 = carry
        off = pl.multiple_of(kt * tk, tk)
        s = _index_scores_t(ik_ref[pl.ds(off, tk), :], iq_heads, w_rows)
        keys = _score_keys(jnp.where(krow + off <= qpos, s, -jnp.inf))
        sc_ref[pl.ds(off, tk), :] = keys
        for g in range(tk // LANES):
            x = keys[g * LANES:(g + 1) * LANES]
            m2 = jnp.maximum(m2, jnp.minimum(m1, x))
            m1 = jnp.maximum(m1, x)
        return m1, m2

    floor = jnp.full((LANES, TQ), INT_MIN, I32)
    m1, m2 = lax.fori_loop(0, nkt, fill, (floor, floor))
    lo = jnp.min(m2, axis=0, keepdims=True)
    hi = jnp.max(m1, axis=0, keepdims=True)

    def count(pred):
        def body(kt, acc):
            off = pl.multiple_of(kt * tk, tk)
            m = pred(sc_ref[pl.ds(off, tk), :], krow + off).astype(I32)
            return acc + jnp.sum(m.reshape(tk // 32, 32, TQ), axis=0)
        acc = lax.fori_loop(0, nkt, body, jnp.zeros((32, TQ), I32))
        return jnp.sum(acc, axis=0, keepdims=True)

    d = hi - lo
    wide = jnp.max(((d < 0) | (d >= 2 ** 30)).astype(I32)) > 0
    base = jnp.where(wide, INT_MIN, lo)
    span = jnp.where(wide, INT_MAX, d)
    nbits = jnp.where(wide, 32, 32 - jnp.min(lax.clz(d)))

    def bit_cond(st):
        b, _, cnt = st
        return (b >= 0) & (jnp.max(jnp.abs(cnt - topk)) > 0)

    def bit_body(st):
        b, delta, cnt = st
        dc = delta | (jnp.int32(1) << b)
        cand = base + dc
        c = count(lambda keys, _: keys >= cand)
        ok = (c >= topk) & (dc <= span)
        return b - 1, jnp.where(ok, dc, delta), jnp.where(ok, c, cnt)

    cnt0 = count(lambda keys, _: keys >= base)
    _, delta, cnt = lax.while_loop(bit_cond, bit_body, (nbits - 1, jnp.zeros((1, TQ), I32), cnt0))
    theta = base + delta

    none_valid = theta <= KEY_NEG_INF
    tied = (cnt > topk) & jnp.logical_not(none_valid)
    jt_ref[0:1, :] = jnp.where(none_valid, -1, INT_MAX)

    @pl.when(jnp.max(tied.astype(I32)) > 0)
    def _():
        above = count(lambda keys, _: keys > theta)
        need = topk - above
        idx_bits = max(int(L - 1).bit_length(), 1)

        def idx_body(i, m):
            cand = m | (jnp.int32(1) << (idx_bits - 1 - i))
            c = count(lambda keys, idx: (keys == theta) & (idx < cand))
            return jnp.where(c <= need - 1, cand, m)

        m = lax.fori_loop(0, idx_bits, idx_body, jnp.zeros((1, TQ), I32))
        jt_ref[0:1, :] = jnp.where(tied, m, jt_ref[0:1, :])

    jt = jt_ref[0:1, :]
    for acc_ref in acc_refs:
        acc_ref[...] = jnp.zeros_like(acc_ref)

    def attend(kt, carry):
        m_run, l_run = carry
        slot = kt % 2
        off = pl.multiple_of(kt * tk, tk)
        for cp in kv_copies(kt, slot):
            cp.wait()

        @pl.when(kt + 1 < nkt)
        def _():
            for cp in kv_copies(kt + 1, 1 - slot):
                cp.start()

        keys = sc_ref[pl.ds(off, tk), :]
        keep = (keys > theta) | ((keys == theta) & (krow + off <= jt))
        m_rows, l_rows = [], []
        heads = [slice(h * ATT_DIM, (h + 1) * ATT_DIM) for h in range(ATT_HEADS)]
        logits = [jnp.where(keep, _dot_nt(kbuf[slot, :, hs], q_ref[:, hs]), NEG_BIG) for hs in heads]
        for h, acc_ref in enumerate(acc_refs):
            hs = heads[h]
            s = logits[h]
            m_old = m_run[h:h + 1, :]
            m_new = jnp.maximum(m_old, jnp.max(s, axis=0, keepdims=True))
            alpha = jnp.exp(m_old - m_new)
            p = jnp.exp(s - m_new)
            l_rows.append(alpha * l_run[h:h + 1, :] + jnp.sum(p, axis=0, keepdims=True))
            acc_ref[...] = alpha * acc_ref[...] + _dot(vbuf[slot, hs, :], p.astype(BF16))
            m_rows.append(m_new)
        return jnp.concatenate(m_rows, axis=0), jnp.concatenate(l_rows, axis=0)

    _, l_fin = lax.fori_loop(0, nkt, attend, (jnp.full((ATT_HEADS, TQ), NEG_BIG, F32), jnp.zeros((ATT_HEADS, TQ), F32)))
    for h, acc_ref in enumerate(acc_refs):
        o_ref[:, h * ATT_DIM:(h + 1) * ATT_DIM] = (acc_ref[...] / l_fin[h:h + 1, :]).T.astype(BF16)


def _dsa_call(q, k, vt, iq, ik, wt, topk):
    L, na = q.shape
    TQ = min(DSA_TQ, L)
    tk = min(DSA_TK, L)
    assert topk <= 2 * LANES and topk <= tk
    return pl.pallas_call(
        functools.partial(_dsa_kernel, topk=topk, tk=tk),
        grid=(L // TQ,),
        in_specs=[
            pl.BlockSpec((TQ, na), lambda i: (i, 0)),
            pl.BlockSpec((TQ, IDX_HEADS * LANES), lambda i: (i, 0)),
            pl.BlockSpec((IDX_HEADS, TQ), lambda i: (0, i)),
            pl.BlockSpec((L, LANES), lambda i: (0, 0)),
            pl.BlockSpec(memory_space=pl.ANY),
            pl.BlockSpec(memory_space=pl.ANY),
        ],
        out_specs=pl.BlockSpec((TQ, na), lambda i: (i, 0)),
        out_shape=jax.ShapeDtypeStruct((L, na), BF16),
        scratch_shapes=[
            pltpu.VMEM((L, TQ), I32),
            pltpu.VMEM((2, tk, na), BF16),
            pltpu.VMEM((2, na, tk), BF16),
            pltpu.SemaphoreType.DMA((2, 2)),
            pltpu.VMEM((8, TQ), I32),
        ] + [pltpu.VMEM((ATT_DIM, TQ), F32)] * ATT_HEADS,
        compiler_params=_params(("parallel",), vmem_mb=56),
        name="dsa",
    )(q, iq, wt, ik, k, vt)


def _merge_kernel(g0_ref, g1_ref, b0_ref, b1_ref, ydn_ref, yat_ref, wdn_ref, wat_ref, o_ref):
    gate0 = _sigmoid(g0_ref[...].astype(F32) + b0_ref[...])
    gate1 = _sigmoid(g1_ref[...].astype(F32) + b1_ref[...])
    merged = gate0 * _dot(ydn_ref[...], wdn_ref[...]) + gate1 * _dot(yat_ref[...], wat_ref[...])
    o_ref[...] = merged.astype(o_ref.dtype)


def _merge_call(proj, b_gates, y_dn, y_at, w_dn, w_at):
    L = proj.shape[0]
    tm = min(1024, L)
    tn = 512
    nj = D_MODEL // tn
    c0 = COL_GATE // tn
    kd = y_dn.shape[1]
    return pl.pallas_call(
        _merge_kernel,
        grid=(L // tm, nj),
        in_specs=[
            pl.BlockSpec((tm, tn), lambda i, j: (i, c0 + j)),
            pl.BlockSpec((tm, tn), lambda i, j: (i, c0 + nj + j)),
            pl.BlockSpec((1, tn), lambda i, j: (0, j)),
            pl.BlockSpec((1, tn), lambda i, j: (0, nj + j)),
            pl.BlockSpec((tm, kd), lambda i, j: (i, 0)),
            pl.BlockSpec((tm, kd), lambda i, j: (i, 0)),
            pl.BlockSpec((kd, tn), lambda i, j: (0, j)),
            pl.BlockSpec((kd, tn), lambda i, j: (0, j)),
        ],
        out_specs=pl.BlockSpec((tm, tn), lambda i, j: (i, j)),
        out_shape=jax.ShapeDtypeStruct((L, D_MODEL), BF16),
        compiler_params=_params(("parallel", "arbitrary")),
        name="merge",
    )(proj, proj, b_gates, b_gates, y_dn, y_at, w_dn, w_at)


RT_E0, RT_E1, RT_W0, RT_W1 = 0, 1, 2, 3


def _first_lane_of_max(v, lane):
    m = jnp.max(v, axis=-1, keepdims=True)
    return m, jnp.min(jnp.where(v == m, lane, LANES), axis=-1, keepdims=True)


def _outproj_kernel(x_ref, mg_ref, wo_ref, nf_ref, wr_ref, br_ref, x1_ref, h2_ref, rt_ref):
    x1 = x_ref[...] + _dot(mg_ref[...], wo_ref[...])
    x1_ref[...] = x1
    h2 = x1 * lax.rsqrt(jnp.mean(x1 * x1, axis=-1, keepdims=True) + EPS) * nf_ref[...]
    h2_ref[...] = h2
    lg = _dot(h2.astype(BF16), wr_ref[...]) + br_ref[...]
    lane = lax.broadcasted_iota(I32, (1, LANES), 1)
    ninf = -jnp.inf
    gl = jnp.where(lane < N_GROUPS, lg, ninf)
    gmax, g_sel = _first_lane_of_max(gl, lane)
    p_group = 1.0 / jnp.sum(jnp.exp(gl - gmax), axis=-1, keepdims=True)
    ex = lane - N_GROUPS
    in_group = (ex >= 0) & (ex < N_EXPERTS) & ((ex // EXPERTS_PER_GROUP) == g_sel)
    el = jnp.where(in_group, lg, ninf)
    m1, i1 = _first_lane_of_max(el, lane)
    m2, i2 = _first_lane_of_max(jnp.where(lane == i1, ninf, el), lane)
    e2 = jnp.exp(m2 - m1)
    w0 = p_group / (1.0 + e2)
    w1 = p_group * e2 / (1.0 + e2)
    rec = jnp.where(lane == RT_E0, (i1 - N_GROUPS).astype(F32), 0.0)
    rec = jnp.where(lane == RT_E1, (i2 - N_GROUPS).astype(F32), rec)
    rec = jnp.where(lane == RT_W0, w0, rec)
    rt_ref[...] = jnp.where(lane == RT_W1, w1, rec)


def _outproj_call(x2, merged, w_out, norm_ffn, w_route, b_route):
    L = x2.shape[0]
    tm = min(256, L)
    row = lambda i: (i, 0)
    fixed = lambda i: (0, 0)
    return pl.pallas_call(
        _outproj_kernel,
        grid=(L // tm,),
        in_specs=[
            pl.BlockSpec((tm, D_MODEL), row),
            pl.BlockSpec((tm, D_MODEL), row),
            pl.BlockSpec((D_MODEL, D_MODEL), fixed),
            pl.BlockSpec((1, D_MODEL), fixed),
            pl.BlockSpec((D_MODEL, LANES), fixed),
            pl.BlockSpec((1, LANES), fixed),
        ],
        out_specs=[pl.BlockSpec((tm, D_MODEL), row), pl.BlockSpec((tm, D_MODEL), row), pl.BlockSpec((tm, LANES), row)],
        out_shape=[jax.ShapeDtypeStruct((L, D_MODEL), F32), jax.ShapeDtypeStruct((L, D_MODEL), F32),
                   jax.ShapeDtypeStruct((L, LANES), F32)],
        compiler_params=_params(("parallel",)),
        name="outproj_route",
    )(x2, merged, w_out, norm_ffn, w_route, b_route)


def _moe_slots(L):
    return -(-(2 * L + N_EXPERTS * (MOE_ROWS - 1)) // MOE_ROWS) * MOE_ROWS


def _moe_plan_kernel(rt_ref, dest_ref, blk_ref, cnt_ref, carry_ref, start_ref):
    phase, i = pl.program_id(0), pl.program_id(1)
    tR = rt_ref.shape[0]
    lane = lax.broadcasted_iota(I32, (1, LANES), 1)
    lane_f = lane.astype(F32)
    rt = rt_ref[...]
    e0, e1 = rt[:, RT_E0:RT_E0 + 1], rt[:, RT_E1:RT_E1 + 1]
    hit0, hit1 = lane_f == e0, lane_f == e1
    onehot = (hit0 | hit1).astype(F32)
    colsum = jnp.sum(onehot, axis=0, keepdims=True)

    @pl.when((phase == 0) & (i == 0))
    def _():
        cnt_ref[...] = jnp.zeros_like(cnt_ref)

    @pl.when(phase == 0)
    def _():
        cnt_ref[...] += colsum

    @pl.when((phase == 1) & (i == 0))
    def _():
        carry_ref[...] = jnp.zeros_like(carry_ref)
        padded = jnp.floor((cnt_ref[...] + (MOE_ROWS - 1)) * (1.0 / MOE_ROWS)) * MOE_ROWS
        r = lax.broadcasted_iota(I32, (LANES, LANES), 0)
        c = lax.broadcasted_iota(I32, (LANES, LANES), 1)
        upper = (r <= c).astype(F32)
        end = jnp.dot(jnp.broadcast_to(padded, (8, LANES)), upper, preferred_element_type=F32,
                      precision=lax.Precision.HIGHEST)[0:1, :]
        start_ref[...] = end - padded
        n_used = end[:, N_EXPERTS - 1:N_EXPERTS] * (1.0 / MOE_ROWS)
        nb = blk_ref.shape[0]
        b = lax.broadcasted_iota(I32, (nb, 1), 0).astype(F32)
        b_eff = jnp.minimum(b, n_used - 1.0)
        done = ((end <= b_eff * MOE_ROWS) & (lane < N_EXPERTS)).astype(F32)
        blk_e = jnp.minimum(jnp.sum(done, axis=-1, keepdims=True), N_EXPERTS - 1.0)
        blk_ref[...] = jnp.where(lane == 0, blk_e, jnp.where(lane == 1, n_used, 0.0)).astype(I32)

    @pl.when(phase == 1)
    def _():
        r = lax.broadcasted_iota(I32, (tR, tR), 0)
        c = lax.broadcasted_iota(I32, (tR, tR), 1)
        before = _dot((r > c).astype(BF16), onehot.astype(BF16)) + carry_ref[...]
        slot = before + start_ref[...]
        d0 = jnp.sum(jnp.where(hit0, slot, 0.0), axis=-1, keepdims=True)
        d1 = jnp.sum(jnp.where(hit1, slot, 0.0), axis=-1, keepdims=True)
        dest_ref[...] = jnp.where(lane == 0, d0, jnp.where(lane == 1, d1, 0.0)).astype(I32)
        carry_ref[...] += colsum


def _moe_plan_call(route):
    L = route.shape[0]
    tR = min(256, L)
    nb = _moe_slots(L) // MOE_ROWS
    nb_pad = -(-nb // 8) * 8
    return pl.pallas_call(
        _moe_plan_kernel,
        grid=(2, L // tR),
        in_specs=[pl.BlockSpec((tR, LANES), lambda p, i: (i, 0))],
        out_specs=[pl.BlockSpec((tR, LANES), lambda p, i: (p * i, 0)), pl.BlockSpec((nb_pad, LANES), lambda p, i: (0, 0))],
        out_shape=[jax.ShapeDtypeStruct((L, LANES), I32), jax.ShapeDtypeStruct((nb_pad, LANES), I32)],
        scratch_shapes=[pltpu.VMEM((1, LANES), F32)] * 3,
        compiler_params=_params(("arbitrary", "arbitrary")),
        name="moe_plan",
    )(route)


def _moe_scatter_kernel(dest_ref, h_ref, xs_in, xs_out, sem, *, rows):
    del xs_in
    base = pl.program_id(0) * rows

    def row_copy(r, j):
        return pltpu.make_async_copy(h_ref.at[pl.ds(r, 1)], xs_out.at[pl.ds(dest_ref[2 * (base + r) + j], 1)], sem)

    def start(r, c):
        row_copy(r, 0).start()
        row_copy(r, 1).start()
        return c

    def wait(r, c):
        row_copy(r, 0).wait()
        row_copy(r, 1).wait()
        return c

    lax.fori_loop(0, rows, start, 0)
    lax.fori_loop(0, rows, wait, 0)


def _moe_scatter_call(dest_flat, h2, xs_zero):
    L = h2.shape[0]
    rows = min(256, L)
    return pl.pallas_call(
        functools.partial(_moe_scatter_kernel, rows=rows),
        grid_spec=pltpu.PrefetchScalarGridSpec(
            num_scalar_prefetch=1,
            grid=(L // rows,),
            in_specs=[pl.BlockSpec((rows, D_MODEL), lambda i, d: (i, 0)), pl.BlockSpec(memory_space=pl.ANY)],
            out_specs=pl.BlockSpec(memory_space=pl.ANY),
            scratch_shapes=[pltpu.SemaphoreType.DMA(())],
        ),
        out_shape=jax.ShapeDtypeStruct(xs_zero.shape, xs_zero.dtype),
        input_output_aliases={2: 0},
        compiler_params=_params(("arbitrary",)),
        name="moe_scatter",
    )(dest_flat, h2, xs_zero)


def _moe_ffn_kernel(be_ref, x_ref, wg_ref, wu_ref, wd_ref, y_ref, *, nb):
    b = pl.program_id(0)

    @pl.when(b < be_ref[nb])
    def _():
        xb = x_ref[...].astype(BF16)
        gate = _dot(xb, wg_ref[...].astype(BF16))
        up = _dot(xb, wu_ref[...].astype(BF16))
        hidden = (gate * _sigmoid(gate) * up).astype(BF16)
        y_ref[...] = _dot(hidden, wd_ref[...].astype(BF16))

    @pl.when(b >= be_ref[nb])
    def _():
        y_ref[...] = jnp.zeros_like(y_ref)


def _moe_ffn_call(blk_e, xs, w_gate, w_up, w_down):
    P = xs.shape[0]
    nb = P // MOE_ROWS
    return pl.pallas_call(
        functools.partial(_moe_ffn_kernel, nb=nb),
        grid_spec=pltpu.PrefetchScalarGridSpec(
            num_scalar_prefetch=1,
            grid=(nb,),
            in_specs=[
                pl.BlockSpec((MOE_ROWS, D_MODEL), lambda b, be: (b, 0)),
                pl.BlockSpec((None, D_MODEL, EXPERT_HIDDEN), lambda b, be: (be[b], 0, 0)),
                pl.BlockSpec((None, D_MODEL, EXPERT_HIDDEN), lambda b, be: (be[b], 0, 0)),
                pl.BlockSpec((None, EXPERT_HIDDEN, D_MODEL), lambda b, be: (be[b], 0, 0)),
            ],
            out_specs=pl.BlockSpec((MOE_ROWS, D_MODEL), lambda b, be: (b, 0)),
        ),
        out_shape=jax.ShapeDtypeStruct((P, D_MODEL), F32),
        compiler_params=_params(("arbitrary",), vmem_mb=56),
        name="moe_ffn",
    )(blk_e, xs, w_gate, w_up, w_down)


def _moe_combine_kernel(dest_ref, x1_ref, rt_ref, nf_ref, ys_hbm, o_ref, ya_ref, yb_ref, sem, *, rows):
    base = pl.program_id(0) * rows

    def row_copy(r, j):
        dst = (ya_ref, yb_ref)[j]
        return pltpu.make_async_copy(ys_hbm.at[pl.ds(dest_ref[2 * (base + r) + j], 1)], dst.at[pl.ds(r, 1)], sem)

    def start(r, c):
        row_copy(r, 0).start()
        row_copy(r, 1).start()
        return c

    def wait(r, c):
        row_copy(r, 0).wait()
        row_copy(r, 1).wait()
        return c

    lax.fori_loop(0, rows, start, 0)
    lax.fori_loop(0, rows, wait, 0)
    rt = rt_ref[...]
    x = x1_ref[...] + rt[:, RT_W0:RT_W0 + 1] * ya_ref[...] + rt[:, RT_W1:RT_W1 + 1] * yb_ref[...]
    o_ref[...] = x * lax.rsqrt(jnp.mean(x * x, axis=-1, keepdims=True) + EPS) * nf_ref[...]


def _moe_combine_call(dest_flat, x1, route, norm_final, ys):
    L = x1.shape[0]
    rows = min(256, L)
    return pl.pallas_call(
        functools.partial(_moe_combine_kernel, rows=rows),
        grid_spec=pltpu.PrefetchScalarGridSpec(
            num_scalar_prefetch=1,
            grid=(L // rows,),
            in_specs=[
                pl.BlockSpec((rows, D_MODEL), lambda i, d: (i, 0)),
                pl.BlockSpec((rows, LANES), lambda i, d: (i, 0)),
                pl.BlockSpec((1, D_MODEL), lambda i, d: (0, 0)),
                pl.BlockSpec(memory_space=pl.ANY),
            ],
            out_specs=pl.BlockSpec((rows, D_MODEL), lambda i, d: (i, 0)),
            scratch_shapes=[pltpu.VMEM((rows, D_MODEL), F32), pltpu.VMEM((rows, D_MODEL), F32),
                            pltpu.SemaphoreType.DMA(())],
        ),
        out_shape=jax.ShapeDtypeStruct((L, D_MODEL), F32),
        compiler_params=_params(("arbitrary",)),
        name="moe_combine",
    )(dest_flat, x1, route, norm_final, ys)


def _pack_w_in(w):
    s = np.cumsum((0, QKV_COLS, DN_HEADS * DN_DV, DN_HEADS, DN_HEADS, ATT_HEADS * ATT_DIM, ATT_HEADS * ATT_DIM,
                   ATT_HEADS * ATT_DIM, IDX_HEADS * IDX_DIM, IDX_DIM, IDX_HEADS, 2 * D_MODEL))
    seg = lambda n: w[:, int(s[n]):int(s[n + 1])]
    w_main = jnp.concatenate([seg(0), seg(1), seg(4), seg(5), seg(6), seg(7), seg(10)], axis=1).astype(BF16)
    pad = jnp.zeros((w.shape[0], SM_IXK - SM_IXW - IDX_HEADS), w.dtype)
    w_small = jnp.concatenate([seg(2), seg(3), seg(9), pad, seg(8)], axis=1).astype(BF16)
    return w_main, w_small


def _forward(x, positions, norm_mix, w_in, b_gates, dn_conv_w, dn_a_log, dn_dt_bias, dn_norm_w, idx_k_norm,
             w_proj_dn, w_proj_att, w_out, norm_ffn, w_group, b_group, w_router, b_router, w_exp_gate, w_exp_up,
             w_exp_down, norm_final):
    st = {}
    L = x.shape[1]
    x2 = x.reshape(L, D_MODEL)
    w_main, w_small = _pack_w_in(w_in[0])
    proj, small = _proj_call(x2, norm_mix[0].reshape(1, D_MODEL), w_main, w_small)
    st["proj"], st["small"] = proj, small
    q, k, v, g, beta = _gdn_prep_call(proj, small, dn_conv_w[0], dn_a_log[0].reshape(1, DN_HEADS),
                                      dn_dt_bias[0].reshape(1, DN_HEADS))
    st["y_dn"] = _gdn_chunk_call(q, k, v, proj, g, g.T, beta, dn_norm_w[0].reshape(1, DN_DV))

    pos_col = positions.reshape(L, 1).astype(F32)
    kn_lanes = jnp.concatenate([jnp.zeros((SM_IXK,), F32), idx_k_norm[0].astype(F32)]).reshape(1, LANES)
    aq, ak, avt, iq, ik, wt = _dsa_prep_call(proj, small, pos_col, kn_lanes)
    st["y_at"] = _dsa_call(aq, ak, avt, iq, ik, wt, min(TOPK_MAX, L // 4))

    merged = _merge_call(proj, b_gates[0].reshape(1, 2 * D_MODEL), st["y_dn"], st["y_at"],
                         w_proj_dn[0].astype(BF16), w_proj_att[0].astype(BF16))
    st["merged"] = merged
    n_route = N_GROUPS + N_EXPERTS
    w_route = jnp.concatenate([w_group[0], w_router[0], jnp.zeros((D_MODEL, LANES - n_route), F32)], axis=1).astype(BF16)
    b_route = jnp.concatenate([b_group[0], b_router[0], jnp.zeros((LANES - n_route,), F32)]).reshape(1, LANES)
    x1, h2, route = _outproj_call(x2, merged, w_out[0].astype(BF16), norm_ffn[0].reshape(1, D_MODEL), w_route, b_route)
    st["x1"], st["h2"], st["route"] = x1, h2, route

    dest, blk = _moe_plan_call(route)
    dest_flat = dest[:, :2].reshape(2 * L)
    slots = _moe_slots(L)
    nb = slots // MOE_ROWS
    blk_e = jnp.concatenate([blk[:nb, 0], blk[0:1, 1]])
    xs = _moe_scatter_call(dest_flat, h2, jnp.zeros((slots, D_MODEL), F32))
    ys = _moe_ffn_call(blk_e, xs, w_exp_gate[0], w_exp_up[0], w_exp_down[0])
    out = _moe_combine_call(dest_flat, x1, route, norm_final.reshape(1, D_MODEL), ys)
    st["out"] = out.reshape(1, L, D_MODEL)
    return st


def kernel(x, positions, norm_mix, w_in, b_gates, dn_conv_w, dn_a_log, dn_dt_bias, dn_norm_w, idx_k_norm, w_proj_dn,
           w_proj_att, w_out, norm_ffn, w_group, b_group, w_router, b_router, w_exp_gate, w_exp_up, w_exp_down,
           norm_final):
    return _forward(x, positions, norm_mix, w_in, b_gates, dn_conv_w, dn_a_log, dn_dt_bias, dn_norm_w, idx_k_norm,
                    w_proj_dn, w_proj_att, w_out, norm_ffn, w_group, b_group, w_router, b_router, w_exp_gate,
                    w_exp_up, w_exp_down, norm_final)["out"]


def _stages(d, upto=None):
    return _forward(*[d[n] for n in ("x", "positions", "norm_mix", "w_in", "b_gates", "dn_conv_w", "dn_a_log",
                                     "dn_dt_bias", "dn_norm_w", "idx_k_norm", "w_proj_dn", "w_proj_att", "w_out",
                                     "norm_ffn", "w_group", "b_group", "w_router", "b_router", "w_exp_gate",
                                     "w_exp_up", "w_exp_down", "norm_final")])
```

```python
import functools

import jax
import jax.numpy as jnp
import numpy as np
from jax import lax
from jax.experimental import pallas as pl
from jax.experimental.pallas import tpu as pltpu

D_MODEL = 2048
DN_HEADS = 8
DN_DK = 128
DN_DV = 128
DN_CONV = 4
DN_CHUNK = 64
ATT_HEADS = 8
ATT_DIM = 128
IDX_HEADS = 8
IDX_DIM = 64
TOPK_MAX = 256
ROPE_THETA = 500000.0
ROPE_FRACTION = 4
N_GROUPS = 8
EXPERTS_PER_GROUP = 8
N_EXPERTS = N_GROUPS * EXPERTS_PER_GROUP
EXPERT_HIDDEN = 512
EPS = 1e-6

LANES = 128
MOE_ROWS = 256
NEG_BIG = -1e30
LOG2_E = 1.4426950408889634

F32 = jnp.float32
BF16 = jnp.bfloat16
I32 = jnp.int32
I16 = jnp.int16

QKV_COLS = 2 * DN_HEADS * DN_DK + DN_HEADS * DN_DV
COL_QKV = 0
COL_Z = COL_QKV + QKV_COLS
COL_ATQ = COL_Z + DN_HEADS * DN_DV
COL_ATK = COL_ATQ + ATT_HEADS * ATT_DIM
COL_ATV = COL_ATK + ATT_HEADS * ATT_DIM
COL_IXQ = COL_ATV + ATT_HEADS * ATT_DIM
COL_GATE = COL_IXQ + IDX_HEADS * IDX_DIM
MAIN_COLS = COL_GATE + 2 * D_MODEL
SM_B = 0
SM_A = 8
SM_IXW = 16
SM_IXK = 64


def _params(sem, vmem_mb=48):
    return pltpu.CompilerParams(dimension_semantics=sem, vmem_limit_bytes=vmem_mb * 1024 * 1024)


def _sigmoid(x):
    return 1.0 / (1.0 + jnp.exp(-x))


def _dot(a, b):
    return jnp.dot(a, b, preferred_element_type=F32)


def _dot_nt(a, b):
    return lax.dot_general(a, b, (((1,), (1,)), ((), ())), preferred_element_type=F32)


def _proj_kernel(x_ref, g_ref, w_ref, ws_ref, o_ref, os_ref, h_ref):
    @pl.when(pl.program_id(1) == 0)
    def _():
        x = x_ref[...]
        h = x * lax.rsqrt(jnp.mean(x * x, axis=-1, keepdims=True) + EPS) * g_ref[...]
        h_ref[...] = h.astype(BF16)
        os_ref[...] = _dot(h_ref[...], ws_ref[...])

    o_ref[...] = _dot(h_ref[...], w_ref[...]).astype(o_ref.dtype)


def _proj_call(x2, gain, w_main, w_small):
    L, D = x2.shape
    N = w_main.shape[1]
    tm = min(1024, L)
    tn = 512
    return pl.pallas_call(
        _proj_kernel,
        grid=(L // tm, N // tn),
        in_specs=[
            pl.BlockSpec((tm, D), lambda i, j: (i, 0)),
            pl.BlockSpec((1, D), lambda i, j: (0, 0)),
            pl.BlockSpec((D, tn), lambda i, j: (0, j)),
            pl.BlockSpec((D, LANES), lambda i, j: (0, 0)),
        ],
        out_specs=[
            pl.BlockSpec((tm, tn), lambda i, j: (i, j)),
            pl.BlockSpec((tm, LANES), lambda i, j: (i, 0)),
        ],
        out_shape=[jax.ShapeDtypeStruct((L, N), BF16), jax.ShapeDtypeStruct((L, LANES), F32)],
        scratch_shapes=[pltpu.VMEM((tm, D), BF16)],
        compiler_params=_params(("parallel", "arbitrary")),
        name="proj",
    )(x2, gain, w_main, w_small)


def _gdn_prep_kernel(qkv_ref, halo_ref, sm_ref, cw_ref, alog_ref, dtb_ref,
                     q_ref, k_ref, v_ref, g_ref, b_ref, xs_ref):
    tT = qkv_ref.shape[0]
    first = pl.program_id(0) == 0
    xs_ref[8:8 + tT, :] = qkv_ref[...].astype(F32)
    halo = halo_ref[8:16, :].astype(F32)
    xs_ref[0:8, :] = jnp.where(first, 0.0, halo)
    nh = DN_HEADS * DN_DK
    for c in range(QKV_COLS // LANES):
        sl = slice(c * LANES, (c + 1) * LANES)
        y = xs_ref[8:8 + tT, sl] * cw_ref[3:4, sl]
        for j in range(DN_CONV - 1):
            y = y + xs_ref[5 + j:5 + j + tT, sl] * cw_ref[j:j + 1, sl]
        y = y * _sigmoid(y)
        if c < 2 * DN_HEADS:
            y = y * lax.rsqrt(jnp.sum(y * y, axis=-1, keepdims=True) + EPS)
            if c < DN_HEADS:
                q_ref[:, sl] = (y * (DN_DK ** -0.5)).astype(BF16)
            else:
                k_ref[:, c * LANES - nh:(c + 1) * LANES - nh] = y.astype(BF16)
        else:
            v_ref[:, c * LANES - 2 * nh:(c + 1) * LANES - 2 * nh] = y.astype(BF16)
    sm = sm_ref[...]
    b_ref[...] = _sigmoid(sm[:, SM_B:SM_B + DN_HEADS])
    a = sm[:, SM_A:SM_A + DN_HEADS] + dtb_ref[...]
    softplus = jnp.maximum(a, 0.0) + jnp.log1p(jnp.exp(-jnp.abs(a)))
    g_ref[...] = -jnp.exp(alog_ref[...]) * softplus


def _gdn_prep_call(proj, small, conv_w, a_log, dt_bias):
    L = proj.shape[0]
    tT = min(256, L)
    hb = tT // 16
    nh = DN_HEADS * DN_DK
    return pl.pallas_call(
        _gdn_prep_kernel,
        grid=(L // tT,),
        in_specs=[
            pl.BlockSpec((tT, QKV_COLS), lambda i: (i, 0)),
            pl.BlockSpec((16, QKV_COLS), lambda i: (jnp.maximum(i * hb - 1, 0), 0)),
            pl.BlockSpec((tT, LANES), lambda i: (i, 0)),
            pl.BlockSpec((DN_CONV, QKV_COLS), lambda i: (0, 0)),
            pl.BlockSpec((1, DN_HEADS), lambda i: (0, 0)),
            pl.BlockSpec((1, DN_HEADS), lambda i: (0, 0)),
        ],
        out_specs=[
            pl.BlockSpec((tT, nh), lambda i: (i, 0)),
            pl.BlockSpec((tT, nh), lambda i: (i, 0)),
            pl.BlockSpec((tT, nh), lambda i: (i, 0)),
            pl.BlockSpec((tT, DN_HEADS), lambda i: (i, 0)),
            pl.BlockSpec((tT, DN_HEADS), lambda i: (i, 0)),
        ],
        out_shape=[jax.ShapeDtypeStruct((L, nh), BF16)] * 3 + [jax.ShapeDtypeStruct((L, DN_HEADS), F32)] * 2,
        scratch_shapes=[pltpu.VMEM((tT + 8, QKV_COLS), F32)],
        compiler_params=_params(("parallel",)),
        name="gdn_prep",
    )(proj, proj, small, conv_w, a_log, dt_bias)


GDN_STEP = 4 * DN_CHUNK


def _bdot(a, b):
    return lax.dot_general(a, b, (((2,), (1,)), ((0,), (0,))), preferred_element_type=F32)


def _bdot_nt(a, b):
    return lax.dot_general(a, b, (((2,), (2,)), ((0,), (0,))), preferred_element_type=F32)


def _gdn_chunk_kernel(q_ref, k_ref, v_ref, z_ref, gc_ref, gr_ref, bc_ref, nw_ref, y_ref, s_ref):
    C, H = DN_CHUNK, DN_HEADS
    n_chunks = q_ref.shape[0] // C

    @pl.when(pl.program_id(0) == 0)
    def _():
        s_ref[...] = jnp.zeros_like(s_ref)

    row = lax.broadcasted_iota(I32, (C, C), 0)
    col = lax.broadcasted_iota(I32, (C, C), 1)
    incl = row >= col
    strict = row > col
    tri = incl.astype(F32)
    tri_t = (row <= col).astype(F32)
    eye = (row == col).astype(F32)

    def per_head(fn):
        return jnp.stack([fn(slice(c * C, (c + 1) * C), h) for c in range(n_chunks) for h in range(H)])

    head = lambda ref: per_head(lambda rs, h: ref[rs, h * DN_DK:(h + 1) * DN_DK])
    qb, kb, vb = head(q_ref), head(k_ref), head(v_ref)
    G_col = [jnp.dot(tri, gc_ref[c * C:(c + 1) * C, :], preferred_element_type=F32, precision=lax.Precision.HIGHEST)
             for c in range(n_chunks)]
    G_row = [jnp.dot(gr_ref[:, c * C:(c + 1) * C], tri_t, preferred_element_type=F32, precision=lax.Precision.HIGHEST)
             for c in range(n_chunks)]
    Gc = jnp.stack([G_col[c][:, h:h + 1] for c in range(n_chunks) for h in range(H)])
    Gr = jnp.stack([G_row[c][h:h + 1, :] for c in range(n_chunks) for h in range(H)])
    Gl = jnp.stack([G_col[c][C - 1:C, h:h + 1] for c in range(n_chunks) for h in range(H)])
    bcol = per_head(lambda rs, h: bc_ref[rs, h:h + 1])

    decay = jnp.exp(jnp.where(incl[None], Gc - Gr, -jnp.inf))
    A = jnp.where(strict[None], bcol * _bdot_nt(kb, kb) * decay, 0.0)
    M = -A
    T = eye[None] + M
    for _ in range(5):
        Mb = M.astype(BF16)
        M = _bdot(Mb, Mb)
        T = T + _bdot(T.astype(BF16), M.astype(BF16))
    Tb = T.astype(BF16)
    eg = jnp.exp(Gc)
    kf = kb.astype(F32)
    w = _bdot(Tb, (kf * (bcol * eg)).astype(BF16)).astype(BF16)
    u = _bdot(Tb, (vb.astype(F32) * bcol).astype(BF16))
    attn = (_bdot_nt(qb, kb) * decay).astype(BF16)
    q_dec = (qb.astype(F32) * eg).astype(BF16)
    k_dec = (kf * jnp.exp(Gl - Gc)).astype(BF16)
    g_last = jnp.exp(Gl)

    nw = nw_ref[...]
    S = s_ref[...]
    for c in range(n_chunks):
        rs = slice(c * C, (c + 1) * C)
        bs = slice(c * H, (c + 1) * H)
        Sb = S.astype(BF16)
        v_new = (u[bs] - _bdot(w[bs], Sb)).astype(BF16)
        o = _bdot(q_dec[bs], Sb) + _bdot(attn[bs], v_new)
        S = g_last[bs] * S + jnp.stack([
            lax.dot_general(k_dec[c * H + h], v_new[h], (((0,), (0,)), ((), ())), preferred_element_type=F32)
            for h in range(H)])
        on = o * lax.rsqrt(jnp.mean(o * o, axis=-1, keepdims=True) + EPS) * nw
        for h in range(H):
            hs = slice(h * DN_DV, (h + 1) * DN_DV)
            zz = z_ref[rs, hs].astype(F32)
            y_ref[rs, hs] = (on[h] * (zz * _sigmoid(zz))).astype(BF16)
    s_ref[...] = S


def _gdn_chunk_call(q, k, v, proj, g, g_t, beta, norm_w):
    L, nh = q.shape
    R = GDN_STEP
    zc = COL_Z // nh
    return pl.pallas_call(
        _gdn_chunk_kernel,
        grid=(L // R,),
        in_specs=[
            pl.BlockSpec((R, nh), lambda i: (i, 0)),
            pl.BlockSpec((R, nh), lambda i: (i, 0)),
            pl.BlockSpec((R, nh), lambda i: (i, 0)),
            pl.BlockSpec((R, nh), lambda i: (i, zc)),
            pl.BlockSpec((R, DN_HEADS), lambda i: (i, 0)),
            pl.BlockSpec((DN_HEADS, R), lambda i: (0, i)),
            pl.BlockSpec((R, DN_HEADS), lambda i: (i, 0)),
            pl.BlockSpec((1, DN_DV), lambda i: (0, 0)),
        ],
        out_specs=pl.BlockSpec((R, nh), lambda i: (i, 0)),
        out_shape=jax.ShapeDtypeStruct((L, nh), BF16),
        scratch_shapes=[pltpu.VMEM((DN_HEADS, DN_DK, DN_DV), F32)],
        compiler_params=_params(("arbitrary",)),
        name="gdn_chunk",
    )(q, k, v, proj, g, g_t, beta, norm_w)


def _rope_tables(pos, period, lane):
    rot = period // ROPE_FRACTION
    half = rot // 2
    lp = lane % period
    expo = -((lp % half).astype(F32) * 2.0 / rot)
    inv_freq = jnp.power(jnp.float32(ROPE_THETA), expo)
    ang = pos * inv_freq
    cos, sin = jnp.cos(ang), jnp.sin(ang)
    c = jnp.where(lp < rot, cos, 1.0)
    s_lo = jnp.where(lp < half, -sin, 0.0)
    s_hi = jnp.where((lp >= half) & (lp < rot), sin, 0.0)
    return c, s_lo, s_hi, half


def _rope(x, tab):
    c, s_lo, s_hi, half = tab
    return x * c + pltpu.roll(x, LANES - half, 1) * s_lo + pltpu.roll(x, half, 1) * s_hi


def _dsa_prep_kernel(q_ref, k_ref, v_ref, iq_ref, sm_ref, pos_ref, kn_ref,
                     qo_ref, ko_ref, vto_ref, iqo_ref, iko_ref, wto_ref):
    lane = lax.broadcasted_iota(I32, (1, LANES), 1)
    pos = pos_ref[...]
    tab_att = _rope_tables(pos, ATT_DIM, lane)
    tab_idx = _rope_tables(pos, IDX_DIM, lane)
    scale = ATT_DIM ** -0.5 * LOG2_E
    for h in range(ATT_HEADS):
        hs = slice(h * ATT_DIM, (h + 1) * ATT_DIM)
        qo_ref[:, hs] = (_rope(q_ref[:, hs].astype(F32), tab_att) * scale).astype(BF16)
        ko_ref[:, hs] = _rope(k_ref[:, hs].astype(F32), tab_att).astype(BF16)
        vto_ref[hs, :] = v_ref[:, hs].astype(F32).T.astype(BF16)
    low = lane < IDX_DIM
    for p in range(IDX_HEADS // 2):
        x = _rope(iq_ref[:, p * LANES:(p + 1) * LANES].astype(F32), tab_idx)
        iqo_ref[:, (2 * p) * LANES:(2 * p + 1) * LANES] = jnp.where(low, x, 0.0).astype(BF16)
        iqo_ref[:, (2 * p + 1) * LANES:(2 * p + 2) * LANES] = jnp.where(low, pltpu.roll(x, IDX_DIM, 1), 0.0).astype(BF16)
    sm = sm_ref[...]
    kx = jnp.where(low, 0.0, sm)
    kx = kx * lax.rsqrt(jnp.sum(kx * kx, axis=-1, keepdims=True) * (1.0 / IDX_DIM) + EPS) * kn_ref[...]
    kx = _rope(kx, tab_idx)
    iko_ref[...] = jnp.where(low, pltpu.roll(kx, IDX_DIM, 1), 0.0).astype(BF16)
    wto_ref[...] = (sm * (IDX_HEADS ** -0.5 * IDX_DIM ** -0.5)).T[SM_IXW:SM_IXW + IDX_HEADS, :]


def _dsa_prep_call(proj, small, pos_col, kn_lanes):
    L = proj.shape[0]
    tT = min(256, L)
    na = ATT_HEADS * ATT_DIM
    ni = IDX_HEADS * IDX_DIM
    return pl.pallas_call(
        _dsa_prep_kernel,
        grid=(L // tT,),
        in_specs=[
            pl.BlockSpec((tT, na), lambda i: (i, COL_ATQ // na)),
            pl.BlockSpec((tT, na), lambda i: (i, COL_ATK // na)),
            pl.BlockSpec((tT, na), lambda i: (i, COL_ATV // na)),
            pl.BlockSpec((tT, ni), lambda i: (i, COL_IXQ // ni)),
            pl.BlockSpec((tT, LANES), lambda i: (i, 0)),
            pl.BlockSpec((tT, 1), lambda i: (i, 0)),
            pl.BlockSpec((1, LANES), lambda i: (0, 0)),
        ],
        out_specs=[
            pl.BlockSpec((tT, na), lambda i: (i, 0)),
            pl.BlockSpec((tT, na), lambda i: (i, 0)),
            pl.BlockSpec((na, tT), lambda i: (0, i)),
            pl.BlockSpec((tT, IDX_HEADS * LANES), lambda i: (i, 0)),
            pl.BlockSpec((tT, LANES), lambda i: (i, 0)),
            pl.BlockSpec((IDX_HEADS, tT), lambda i: (0, i)),
        ],
        out_shape=[jax.ShapeDtypeStruct((L, na), BF16), jax.ShapeDtypeStruct((L, na), BF16),
                   jax.ShapeDtypeStruct((na, L), BF16),
                   jax.ShapeDtypeStruct((L, IDX_HEADS * LANES), BF16), jax.ShapeDtypeStruct((L, LANES), BF16),
                   jax.ShapeDtypeStruct((IDX_HEADS, L), F32)],
        compiler_params=_params(("parallel",)),
        name="dsa_prep",
    )(proj, proj, proj, proj, small, pos_col, kn_lanes)


def _index_scores_t(ik_blk, iq_heads, w_rows):
    acc = None
    for qh, wh in zip(iq_heads, w_rows):
        term = wh * jnp.maximum(_dot_nt(ik_blk, qh), 0.0)
        acc = term if acc is None else acc + term
    return acc


def _sortable(bits):
    return jnp.where(bits < 0, bits ^ jnp.int32(0x7FFFFFFF), bits)


def _score_keys(s):
    key = _sortable(pltpu.bitcast(s, I32))
    return jnp.where(key == -1, 0, key)


DSA_TQ = 256
DSA_TK = 512
SUM_ROWS = 16
KEY_NEG_INF = int(np.int32(np.array(-np.inf, np.float32).view(np.int32)) ^ np.int32(0x7FFFFFFF))
INT_MIN = -(2 ** 31)
INT_MAX = 2 ** 31 - 1


def _dsa_kernel(q_ref, iq_ref, wt_ref, ik_ref, k_hbm, vt_hbm, o_ref,
                sc_ref, hi_ref, kbuf, vbuf, sem, tie_ref, bias_ref, *acc_refs, topk, tk):
    TQ = q_ref.shape[0]
    L = ik_ref.shape[0]
    t0 = pl.program_id(0) * TQ
    nkt = (t0 + TQ + tk - 1) // tk

    def kv_copies(kt, slot):
        off = pl.multiple_of(kt * tk, tk)
        return (pltpu.make_async_copy(k_hbm.at[pl.ds(off, tk)], kbuf.at[slot], sem.at[0, slot]),
                pltpu.make_async_copy(vt_hbm.at[:, pl.ds(off, tk)], vbuf.at[slot], sem.at[1, slot]))

    for cp in kv_copies(0, 0):
        cp.start()

    iq_heads = [iq_ref[:, h * LANES:(h + 1) * LANES] for h in range(IDX_HEADS)]
    w_rows = [wt_ref[h:h + 1, :] for h in range(IDX_HEADS)]
    qpos = t0 + lax.broadcasted_iota(I32, (tk, TQ), 1)
    krow = lax.broadcasted_iota(I32, (tk, TQ), 0)

    def fill(kt, carry, on_diagonal):
        off = pl.multiple_of(kt * tk, tk)
        s = _index_scores_t(ik_ref[pl.ds(off, tk), :], iq_heads, w_rows)
        if on_diagonal:
            s = jnp.where(krow + off <= qpos, s, -jnp.inf)
        keys = _score_keys(s)
        sc_ref[pl.ds(off, tk), :] = keys
        hi_ref[pl.ds(off, tk), :] = (keys >> 16).astype(I16)
        return carry

    n_below = (t0 + 1) // tk
    lax.fori_loop(0, n_below, functools.partial(fill, on_diagonal=False), 0)
    lax.fori_loop(n_below, nkt, functools.partial(fill, on_diagonal=True), 0)

    def count(pred):
        def body(kt, acc):
            off = pl.multiple_of(kt * tk, tk)
            m = pred(sc_ref[pl.ds(off, tk), :], krow + off).astype(I32)
            return acc + jnp.sum(m.reshape(tk // 32, 32, TQ), axis=0)
        acc = lax.fori_loop(0, nkt, body, jnp.zeros((32, TQ), I32))
        return jnp.sum(acc, axis=0, keepdims=True)

    def count_hi(cand):
        def body(kt, acc):
            off = pl.multiple_of(kt * tk, tk)
            m = (hi_ref[pl.ds(off, tk), :] >= cand).astype(I16)
            for g in range(tk // 32):
                acc = acc + m[g * 32:(g + 1) * 32]
            return acc
        acc = lax.fori_loop(0, nkt, body, jnp.zeros((32, TQ), I16))
        return jnp.sum(acc.astype(I32), axis=0, keepdims=True)

    def bit_cond(st):
        b, _, cnt = st
        return (b >= 0) & (jnp.max(jnp.abs(cnt - topk)) > 0)

    def hi_body(i, st):
        u, cnt = st
        uc = u | (jnp.int32(2 ** 15) >> i)
        c = count_hi((uc - 2 ** 15).astype(I16))
        ok = c >= topk
        return jnp.where(ok, uc, u), jnp.where(ok, c, cnt)

    cnt_all = jnp.zeros((1, TQ), I32) + nkt * tk
    u, cnt = lax.fori_loop(0, 16, hi_body, (jnp.zeros((1, TQ), I32), cnt_all))
    base = (u - 2 ** 15) << 16

    def lo_body(st):
        b, delta, cnt = st
        dc = delta | (jnp.int32(1) << b)
        cand = base + dc
        c = count(lambda keys, _: keys >= cand)
        ok = c >= topk
        return b - 1, jnp.where(ok, dc, delta), jnp.where(ok, c, cnt)

    _, delta, cnt = lax.while_loop(bit_cond, lo_body, (jnp.int32(15), jnp.zeros((1, TQ), I32), cnt))
    theta = base + delta

    none_valid = theta <= KEY_NEG_INF
    tied = (cnt > topk) & jnp.logical_not(none_valid)
    any_tied = jnp.max(tied.astype(I32)) > 0
    tie_ref[0:1, :] = jnp.where(none_valid, 0, INT_MAX)
    tie_ref[1:2, :] = jnp.zeros((1, TQ), I32)

    @pl.when(any_tied)
    def _():
        above = count(lambda keys, _: keys > theta)
        tie_ref[0:1, :] = jnp.where(tied, topk - above, tie_ref[0:1, :])

    need = tie_ref[0:1, :]
    theta_keep = jnp.where(need > 0, theta, theta + 1)

    for acc_ref in acc_refs:
        acc_ref[...] = jnp.zeros_like(acc_ref)

    ones_rows = jnp.ones((SUM_ROWS, tk), BF16)

    def attend(kt, m_run):
        slot = kt % 2
        off = pl.multiple_of(kt * tk, tk)
        for cp in kv_copies(kt, slot):
            cp.wait()

        @pl.when(kt + 1 < nkt)
        def _():
            for cp in kv_copies(kt + 1, 1 - slot):
                cp.start()

        keys = sc_ref[pl.ds(off, tk), :]

        @pl.when(jnp.logical_not(any_tied))
        def _():
            bias_ref[...] = jnp.where(keys >= theta_keep, 0.0, NEG_BIG)

        @pl.when(any_tied)
        def _():
            tie = keys == theta
            r = lax.broadcasted_iota(I32, (tk, tk), 0)
            c = lax.broadcasted_iota(I32, (tk, tk), 1)
            earlier = _dot((r > c).astype(F32).astype(BF16), jnp.where(tie, 1.0, 0.0).astype(BF16))
            rank = earlier + tie_ref[1:2, :].astype(F32)
            keep = (keys > theta) | (tie & (rank < need.astype(F32)))
            bias_ref[...] = jnp.where(keep, 0.0, NEG_BIG)
            tie_ref[1:2, :] += jnp.sum(tie.astype(I32), axis=0, keepdims=True)

        m_rows = []
        heads = [slice(h * ATT_DIM, (h + 1) * ATT_DIM) for h in range(ATT_HEADS)]
        logits = [_dot_nt(kbuf[slot, :, hs], q_ref[:, hs]) + bias_ref[...] for hs in heads]
        for h, acc_ref in enumerate(acc_refs):
            s = logits[h]
            m_old = m_run[h:h + 1, :]
            m_new = jnp.maximum(m_old, jnp.max(s, axis=0, keepdims=True))
            p = jnp.exp2(s - m_new).astype(BF16)
            v_ext = jnp.concatenate([vbuf[slot, heads[h], :], ones_rows], axis=0)
            acc_ref[...] = jnp.exp2(m_old - m_new) * acc_ref[...] + _dot(v_ext, p)
            m_rows.append(m_new)
        return jnp.concatenate(m_rows, axis=0)

    lax.fori_loop(0, nkt, attend, jnp.full((ATT_HEADS, TQ), NEG_BIG, F32))
    for h, acc_ref in enumerate(acc_refs):
        out_t = acc_ref[0:ATT_DIM, :] / acc_ref[ATT_DIM:ATT_DIM + 1, :]
        o_ref[:, h * ATT_DIM:(h + 1) * ATT_DIM] = out_t.T.astype(BF16)


def _dsa_call(q, k, vt, iq, ik, wt, topk):
    L, na = q.shape
    TQ = min(DSA_TQ, L)
    tk = min(DSA_TK, L)
    assert topk <= tk
    return pl.pallas_call(
        functools.partial(_dsa_kernel, topk=topk, tk=tk),
        grid=(L // TQ,),
        in_specs=[
            pl.BlockSpec((TQ, na), lambda i: (i, 0)),
            pl.BlockSpec((TQ, IDX_HEADS * LANES), lambda i: (i, 0)),
            pl.BlockSpec((IDX_HEADS, TQ), lambda i: (0, i)),
            pl.BlockSpec((L, LANES), lambda i: (0, 0)),
            pl.BlockSpec(memory_space=pl.ANY),
            pl.BlockSpec(memory_space=pl.ANY),
        ],
        out_specs=pl.BlockSpec((TQ, na), lambda i: (i, 0)),
        out_shape=jax.ShapeDtypeStruct((L, na), BF16),
        scratch_shapes=[
            pltpu.VMEM((L, TQ), I32),
            pltpu.VMEM((L, TQ), I16),
            pltpu.VMEM((2, tk, na), BF16),
            pltpu.VMEM((2, na, tk), BF16),
            pltpu.SemaphoreType.DMA((2, 2)),
            pltpu.VMEM((8, TQ), I32),
            pltpu.VMEM((tk, TQ), F32),
        ] + [pltpu.VMEM((ATT_DIM + SUM_ROWS, TQ), F32)] * ATT_HEADS,
        compiler_params=_params(("parallel",), vmem_mb=56),
        name="dsa",
    )(q, iq, wt, ik, k, vt)


def _merge_kernel(g0_ref, g1_ref, b0_ref, b1_ref, ydn_ref, yat_ref, wdn_ref, wat_ref, o_ref):
    gate0 = _sigmoid(g0_ref[...].astype(F32) + b0_ref[...])
    gate1 = _sigmoid(g1_ref[...].astype(F32) + b1_ref[...])
    merged = gate0 * _dot(ydn_ref[...], wdn_ref[...]) + gate1 * _dot(yat_ref[...], wat_ref[...])
    o_ref[...] = merged.astype(o_ref.dtype)


def _merge_call(proj, b_gates, y_dn, y_at, w_dn, w_at):
    L = proj.shape[0]
    tm = min(1024, L)
    tn = 512
    nj = D_MODEL // tn
    c0 = COL_GATE // tn
    kd = y_dn.shape[1]
    return pl.pallas_call(
        _merge_kernel,
        grid=(L // tm, nj),
        in_specs=[
            pl.BlockSpec((tm, tn), lambda i, j: (i, c0 + j)),
            pl.BlockSpec((tm, tn), lambda i, j: (i, c0 + nj + j)),
            pl.BlockSpec((1, tn), lambda i, j: (0, j)),
            pl.BlockSpec((1, tn), lambda i, j: (0, nj + j)),
            pl.BlockSpec((tm, kd), lambda i, j: (i, 0)),
            pl.BlockSpec((tm, kd), lambda i, j: (i, 0)),
            pl.BlockSpec((kd, tn), lambda i, j: (0, j)),
            pl.BlockSpec((kd, tn), lambda i, j: (0, j)),
        ],
        out_specs=pl.BlockSpec((tm, tn), lambda i, j: (i, j)),
        out_shape=jax.ShapeDtypeStruct((L, D_MODEL), BF16),
        compiler_params=_params(("parallel", "arbitrary")),
        name="merge",
    )(proj, proj, b_gates, b_gates, y_dn, y_at, w_dn, w_at)


RT_E0, RT_E1, RT_W0, RT_W1 = 0, 1, 2, 3


def _first_lane_of_max(v, lane):
    m = jnp.max(v, axis=-1, keepdims=True)
    return m, jnp.min(jnp.where(v == m, lane, LANES), axis=-1, keepdims=True)


def _outproj_kernel(x_ref, mg_ref, wo_ref, nf_ref, wr_ref, br_ref, x1_ref, h2_ref, rt_ref):
    x1 = x_ref[...] + _dot(mg_ref[...], wo_ref[...])
    x1_ref[...] = x1
    h2 = x1 * lax.rsqrt(jnp.mean(x1 * x1, axis=-1, keepdims=True) + EPS) * nf_ref[...]
    h2_ref[...] = h2
    lg = _dot(h2.astype(BF16), wr_ref[...]) + br_ref[...]
    lane = lax.broadcasted_iota(I32, (1, LANES), 1)
    ninf = -jnp.inf
    gl = jnp.where(lane < N_GROUPS, lg, ninf)
    gmax, g_sel = _first_lane_of_max(gl, lane)
    p_group = 1.0 / jnp.sum(jnp.exp(gl - gmax), axis=-1, keepdims=True)
    ex = lane - N_GROUPS
    in_group = (ex >= 0) & (ex < N_EXPERTS) & ((ex // EXPERTS_PER_GROUP) == g_sel)
    el = jnp.where(in_group, lg, ninf)
    m1, i1 = _first_lane_of_max(el, lane)
    m2, i2 = _first_lane_of_max(jnp.where(lane == i1, ninf, el), lane)
    e2 = jnp.exp(m2 - m1)
    w0 = p_group / (1.0 + e2)
    w1 = p_group * e2 / (1.0 + e2)
    rec = jnp.where(lane == RT_E0, (i1 - N_GROUPS).astype(F32), 0.0)
    rec = jnp.where(lane == RT_E1, (i2 - N_GROUPS).astype(F32), rec)
    rec = jnp.where(lane == RT_W0, w0, rec)
    rt_ref[...] = jnp.where(lane == RT_W1, w1, rec)


def _outproj_call(x2, merged, w_out, norm_ffn, w_route, b_route):
    L = x2.shape[0]
    tm = min(256, L)
    row = lambda i: (i, 0)
    fixed = lambda i: (0, 0)
    return pl.pallas_call(
        _outproj_kernel,
        grid=(L // tm,),
        in_specs=[
            pl.BlockSpec((tm, D_MODEL), row),
            pl.BlockSpec((tm, D_MODEL), row),
            pl.BlockSpec((D_MODEL, D_MODEL), fixed),
            pl.BlockSpec((1, D_MODEL), fixed),
            pl.BlockSpec((D_MODEL, LANES), fixed),
            pl.BlockSpec((1, LANES), fixed),
        ],
        out_specs=[pl.BlockSpec((tm, D_MODEL), row), pl.BlockSpec((tm, D_MODEL), row), pl.BlockSpec((tm, LANES), row)],
        out_shape=[jax.ShapeDtypeStruct((L, D_MODEL), F32), jax.ShapeDtypeStruct((L, D_MODEL), F32),
                   jax.ShapeDtypeStruct((L, LANES), F32)],
        compiler_params=_params(("parallel",)),
        name="outproj_route",
    )(x2, merged, w_out, norm_ffn, w_route, b_route)


def _moe_slots(L):
    return -(-(2 * L + N_EXPERTS * (MOE_ROWS - 1)) // MOE_ROWS) * MOE_ROWS


def _moe_plan_kernel(rt_ref, dest_ref, blk_ref, cnt_ref, carry_ref, start_ref):
    phase, i = pl.program_id(0), pl.program_id(1)
    tR = rt_ref.shape[0]
    lane = lax.broadcasted_iota(I32, (1, LANES), 1)
    lane_f = lane.astype(F32)
    rt = rt_ref[...]
    e0, e1 = rt[:, RT_E0:RT_E0 + 1], rt[:, RT_E1:RT_E1 + 1]
    hit0, hit1 = lane_f == e0, lane_f == e1
    onehot = (hit0 | hit1).astype(F32)
    colsum = jnp.sum(onehot, axis=0, keepdims=True)

    @pl.when((phase == 0) & (i == 0))
    def _():
        cnt_ref[...] = jnp.zeros_like(cnt_ref)

    @pl.when(phase == 0)
    def _():
        cnt_ref[...] += colsum

    @pl.when((phase == 1) & (i == 0))
    def _():
        carry_ref[...] = jnp.zeros_like(carry_ref)
        padded = jnp.floor((cnt_ref[...] + (MOE_ROWS - 1)) * (1.0 / MOE_ROWS)) * MOE_ROWS
        r = lax.broadcasted_iota(I32, (LANES, LANES), 0)
        c = lax.broadcasted_iota(I32, (LANES, LANES), 1)
        upper = (r <= c).astype(F32)
        end = jnp.dot(jnp.broadcast_to(padded, (8, LANES)), upper, preferred_element_type=F32,
                      precision=lax.Precision.HIGHEST)[0:1, :]
        start_ref[...] = end - padded
        n_used = end[:, N_EXPERTS - 1:N_EXPERTS] * (1.0 / MOE_ROWS)
        nb = blk_ref.shape[0]
        b = lax.broadcasted_iota(I32, (nb, 1), 0).astype(F32)
        b_eff = jnp.minimum(b, n_used - 1.0)
        done = ((end <= b_eff * MOE_ROWS) & (lane < N_EXPERTS)).astype(F32)
        blk_e = jnp.minimum(jnp.sum(done, axis=-1, keepdims=True), N_EXPERTS - 1.0)
        blk_ref[...] = jnp.where(lane == 0, blk_e, jnp.where(lane == 1, n_used, 0.0)).astype(I32)

    @pl.when(phase == 1)
    def _():
        r = lax.broadcasted_iota(I32, (tR, tR), 0)
        c = lax.broadcasted_iota(I32, (tR, tR), 1)
        before = _dot((r > c).astype(BF16), onehot.astype(BF16)) + carry_ref[...]
        slot = before + start_ref[...]
        d0 = jnp.sum(jnp.where(hit0, slot, 0.0), axis=-1, keepdims=True)
        d1 = jnp.sum(jnp.where(hit1, slot, 0.0), axis=-1, keepdims=True)
        dest_ref[...] = jnp.where(lane == 0, d0, jnp.where(lane == 1, d1, 0.0)).astype(I32)
        carry_ref[...] += colsum


def _moe_plan_call(route):
    L = route.shape[0]
    tR = min(256, L)
    nb = _moe_slots(L) // MOE_ROWS
    nb_pad = -(-nb // 8) * 8
    return pl.pallas_call(
        _moe_plan_kernel,
        grid=(2, L // tR),
        in_specs=[pl.BlockSpec((tR, LANES), lambda p, i: (i, 0))],
        out_specs=[pl.BlockSpec((tR, LANES), lambda p, i: (p * i, 0)), pl.BlockSpec((nb_pad, LANES), lambda p, i: (0, 0))],
        out_shape=[jax.ShapeDtypeStruct((L, LANES), I32), jax.ShapeDtypeStruct((nb_pad, LANES), I32)],
        scratch_shapes=[pltpu.VMEM((1, LANES), F32)] * 3,
        compiler_params=_params(("arbitrary", "arbitrary")),
        name="moe_plan",
    )(route)


def _moe_scatter_kernel(dest_ref, h_ref, xs_in, xs_out, sem, *, rows):
    del xs_in
    base = pl.program_id(0) * rows

    def row_copy(r, j):
        return pltpu.make_async_copy(h_ref.at[pl.ds(r, 1)], xs_out.at[pl.ds(dest_ref[2 * (base + r) + j], 1)], sem)

    def start(r, c):
        row_copy(r, 0).start()
        row_copy(r, 1).start()
        return c

    def wait(r, c):
        row_copy(r, 0).wait()
        row_copy(r, 1).wait()
        return c

    lax.fori_loop(0, rows, start, 0)
    lax.fori_loop(0, rows, wait, 0)


def _moe_scatter_call(dest_flat, h2, xs_zero):
    L = h2.shape[0]
    rows = min(256, L)
    return pl.pallas_call(
        functools.partial(_moe_scatter_kernel, rows=rows),
        grid_spec=pltpu.PrefetchScalarGridSpec(
            num_scalar_prefetch=1,
            grid=(L // rows,),
            in_specs=[pl.BlockSpec((rows, D_MODEL), lambda i, d: (i, 0)), pl.BlockSpec(memory_space=pl.ANY)],
            out_specs=pl.BlockSpec(memory_space=pl.ANY),
            scratch_shapes=[pltpu.SemaphoreType.DMA(())],
        ),
        out_shape=jax.ShapeDtypeStruct(xs_zero.shape, xs_zero.dtype),
        input_output_aliases={2: 0},
        compiler_params=_params(("arbitrary",)),
        name="moe_scatter",
    )(dest_flat, h2, xs_zero)


def _moe_ffn_kernel(be_ref, x_ref, wg_ref, wu_ref, wd_ref, y_ref, *, nb):
    b = pl.program_id(0)

    @pl.when(b < be_ref[nb])
    def _():
        xb = x_ref[...].astype(BF16)
        gate = _dot(xb, wg_ref[...].astype(BF16))
        up = _dot(xb, wu_ref[...].astype(BF16))
        hidden = (gate * _sigmoid(gate) * up).astype(BF16)
        y_ref[...] = _dot(hidden, wd_ref[...].astype(BF16))

    @pl.when(b >= be_ref[nb])
    def _():
        y_ref[...] = jnp.zeros_like(y_ref)


def _moe_ffn_call(blk_e, xs, w_gate, w_up, w_down):
    P = xs.shape[0]
    nb = P // MOE_ROWS
    return pl.pallas_call(
        functools.partial(_moe_ffn_kernel, nb=nb),
        grid_spec=pltpu.PrefetchScalarGridSpec(
            num_scalar_prefetch=1,
            grid=(nb,),
            in_specs=[
                pl.BlockSpec((MOE_ROWS, D_MODEL), lambda b, be: (b, 0)),
                pl.BlockSpec((None, D_MODEL, EXPERT_HIDDEN), lambda b, be: (be[b], 0, 0)),
                pl.BlockSpec((None, D_MODEL, EXPERT_HIDDEN), lambda b, be: (be[b], 0, 0)),
                pl.BlockSpec((None, EXPERT_HIDDEN, D_MODEL), lambda b, be: (be[b], 0, 0)),
            ],
            out_specs=pl.BlockSpec((MOE_ROWS, D_MODEL), lambda b, be: (b, 0)),
        ),
        out_shape=jax.ShapeDtypeStruct((P, D_MODEL), F32),
        compiler_params=_params(("arbitrary",), vmem_mb=56),
        name="moe_ffn",
    )(blk_e, xs, w_gate, w_up, w_down)


def _moe_combine_kernel(dest_ref, x1_ref, rt_ref, nf_ref, ys_hbm, o_ref, ya_ref, yb_ref, sem, *, rows):
    base = pl.program_id(0) * rows

    def row_copy(r, j):
        dst = (ya_ref, yb_ref)[j]
        return pltpu.make_async_copy(ys_hbm.at[pl.ds(dest_ref[2 * (base + r) + j], 1)], dst.at[pl.ds(r, 1)], sem)

    def start(r, c):
        row_copy(r, 0).start()
        row_copy(r, 1).start()
        return c

    def wait(r, c):
        row_copy(r, 0).wait()
        row_copy(r, 1).wait()
        return c

    lax.fori_loop(0, rows, start, 0)
    lax.fori_loop(0, rows, wait, 0)
    rt = rt_ref[...]
    x = x1_ref[...] + rt[:, RT_W0:RT_W0 + 1] * ya_ref[...] + rt[:, RT_W1:RT_W1 + 1] * yb_ref[...]
    o_ref[...] = x * lax.rsqrt(jnp.mean(x * x, axis=-1, keepdims=True) + EPS) * nf_ref[...]


def _moe_combine_call(dest_flat, x1, route, norm_final, ys):
    L = x1.shape[0]
    rows = min(256, L)
    return pl.pallas_call(
        functools.partial(_moe_combine_kernel, rows=rows),
        grid_spec=pltpu.PrefetchScalarGridSpec(
            num_scalar_prefetch=1,
            grid=(L // rows,),
            in_specs=[
                pl.BlockSpec((rows, D_MODEL), lambda i, d: (i, 0)),
                pl.BlockSpec((rows, LANES), lambda i, d: (i, 0)),
                pl.BlockSpec((1, D_MODEL), lambda i, d: (0, 0)),
                pl.BlockSpec(memory_space=pl.ANY),
            ],
            out_specs=pl.BlockSpec((rows, D_MODEL), lambda i, d: (i, 0)),
            scratch_shapes=[pltpu.VMEM((rows, D_MODEL), F32), pltpu.VMEM((rows, D_MODEL), F32),
                            pltpu.SemaphoreType.DMA(())],
        ),
        out_shape=jax.ShapeDtypeStruct((L, D_MODEL), F32),
        compiler_params=_params(("arbitrary",)),
        name="moe_combine",
    )(dest_flat, x1, route, norm_final, ys)


def _pack_w_in(w):
    s = np.cumsum((0, QKV_COLS, DN_HEADS * DN_DV, DN_HEADS, DN_HEADS, ATT_HEADS * ATT_DIM, ATT_HEADS * ATT_DIM,
                   ATT_HEADS * ATT_DIM, IDX_HEADS * IDX_DIM, IDX_DIM, IDX_HEADS, 2 * D_MODEL))
    seg = lambda n: w[:, int(s[n]):int(s[n + 1])]
    w_main = jnp.concatenate([seg(0), seg(1), seg(4), seg(5), seg(6), seg(7), seg(10)], axis=1).astype(BF16)
    pad = jnp.zeros((w.shape[0], SM_IXK - SM_IXW - IDX_HEADS), w.dtype)
    w_small = jnp.concatenate([seg(2), seg(3), seg(9), pad, seg(8)], axis=1).astype(BF16)
    return w_main, w_small


def _forward(x, positions, norm_mix, w_in, b_gates, dn_conv_w, dn_a_log, dn_dt_bias, dn_norm_w, idx_k_norm,
             w_proj_dn, w_proj_att, w_out, norm_ffn, w_group, b_group, w_router, b_router, w_exp_gate, w_exp_up,
             w_exp_down, norm_final):
    st = {}
    L = x.shape[1]
    x2 = x.reshape(L, D_MODEL)
    w_main, w_small = _pack_w_in(w_in[0])
    proj, small = _proj_call(x2, norm_mix[0].reshape(1, D_MODEL), w_main, w_small)
    st["proj"], st["small"] = proj, small
    q, k, v, g, beta = _gdn_prep_call(proj, small, dn_conv_w[0], dn_a_log[0].reshape(1, DN_HEADS),
                                      dn_dt_bias[0].reshape(1, DN_HEADS))
    st["y_dn"] = _gdn_chunk_call(q, k, v, proj, g, g.T, beta, dn_norm_w[0].reshape(1, DN_DV))

    pos_col = positions.reshape(L, 1).astype(F32)
    kn_lanes = jnp.concatenate([jnp.zeros((SM_IXK,), F32), idx_k_norm[0].astype(F32)]).reshape(1, LANES)
    aq, ak, avt, iq, ik, wt = _dsa_prep_call(proj, small, pos_col, kn_lanes)
    st["y_at"] = _dsa_call(aq, ak, avt, iq, ik, wt, min(TOPK_MAX, L // 4))

    merged = _merge_call(proj, b_gates[0].reshape(1, 2 * D_MODEL), st["y_dn"], st["y_at"],
                         w_proj_dn[0].astype(BF16), w_proj_att[0].astype(BF16))
    st["merged"] = merged
    n_route = N_GROUPS + N_EXPERTS
    w_route = jnp.concatenate([w_group[0], w_router[0], jnp.zeros((D_MODEL, LANES - n_route), F32)], axis=1).astype(BF16)
    b_route = jnp.concatenate([b_group[0], b_router[0], jnp.zeros((LANES - n_route,), F32)]).reshape(1, LANES)
    x1, h2, route = _outproj_call(x2, merged, w_out[0].astype(BF16), norm_ffn[0].reshape(1, D_MODEL), w_route, b_route)
    st["x1"], st["h2"], st["route"] = x1, h2, route

    dest, blk = _moe_plan_call(route)
    dest_flat = dest[:, :2].reshape(2 * L)
    slots = _moe_slots(L)
    nb = slots // MOE_ROWS
    blk_e = jnp.concatenate([blk[:nb, 0], blk[0:1, 1]])
    xs = _moe_scatter_call(dest_flat, h2, jnp.zeros((slots, D_MODEL), F32))
    ys = _moe_ffn_call(blk_e, xs, w_exp_gate[0], w_exp_up[0], w_exp_down[0])
    out = _moe_combine_call(dest_flat, x1, route, norm_final.reshape(1, D_MODEL), ys)
    st["out"] = out.reshape(1, L, D_MODEL)
    return st


def kernel(x, positions, norm_mix, w_in, b_gates, dn_conv_w, dn_a_log, dn_dt_bias, dn_norm_w, idx_k_norm, w_proj_dn,
           w_proj_att, w_out, norm_ffn, w_group, b_group, w_router, b_router, w_exp_gate, w_exp_up, w_exp_down,
           norm_final):
    return _forward(x, positions, norm_mix, w_in, b_gates, dn_conv_w, dn_a_log, dn_dt_bias, dn_norm_w, idx_k_norm,
                    w_proj_dn, w_proj_att, w_out, norm_ffn, w_group, b_group, w_router, b_router, w_exp_gate,
                    w_exp_up, w_exp_down, norm_final)["out"]


def _stages(d, upto=None):
    return _forward(*[d[n] for n in ("x", "positions", "norm_mix", "w_in", "b_gates", "dn_conv_w", "dn_a_log",
                                     "dn_dt_bias", "dn_norm_w", "idx_k_norm", "w_proj_dn", "w_proj_att", "w_out",
                                     "norm_ffn", "w_group", "b_group", "w_router", "b_router", "w_exp_gate",
                                     "w_exp_up", "w_exp_down", "norm_final")])
```

```python
import functools

import jax
import jax.numpy as jnp
import numpy as np
from jax import lax
from jax.experimental import pallas as pl
from jax.experimental.pallas import tpu as pltpu

D_MODEL = 2048
DN_HEADS = 8
DN_DK = 128
DN_DV = 128
DN_CONV = 4
DN_CHUNK = 64
ATT_HEADS = 8
ATT_DIM = 128
IDX_HEADS = 8
IDX_DIM = 64
TOPK_MAX = 256
ROPE_THETA = 500000.0
ROPE_FRACTION = 4
N_GROUPS = 8
EXPERTS_PER_GROUP = 8
N_EXPERTS = N_GROUPS * EXPERTS_PER_GROUP
EXPERT_HIDDEN = 512
EPS = 1e-6

LANES = 128
MOE_ROWS = 256
NEG_BIG = -1e30
LOG2_E = 1.4426950408889634

F32 = jnp.float32
BF16 = jnp.bfloat16
I32 = jnp.int32
I16 = jnp.int16

QKV_COLS = 2 * DN_HEADS * DN_DK + DN_HEADS * DN_DV
COL_QKV = 0
COL_Z = COL_QKV + QKV_COLS
COL_ATQ = COL_Z + DN_HEADS * DN_DV
COL_ATK = COL_ATQ + ATT_HEADS * ATT_DIM
COL_ATV = COL_ATK + ATT_HEADS * ATT_DIM
COL_IXQ = COL_ATV + ATT_HEADS * ATT_DIM
COL_GATE = COL_IXQ + IDX_HEADS * IDX_DIM
MAIN_COLS = COL_GATE + 2 * D_MODEL
SM_B = 0
SM_A = 8
SM_IXW = 16
SM_IXK = 64


def _params(sem, vmem_mb=48):
    return pltpu.CompilerParams(dimension_semantics=sem, vmem_limit_bytes=vmem_mb * 1024 * 1024)


def _sigmoid(x):
    return 1.0 / (1.0 + jnp.exp(-x))


def _dot(a, b):
    return jnp.dot(a, b, preferred_element_type=F32)


def _dot_nt(a, b):
    return lax.dot_general(a, b, (((1,), (1,)), ((), ())), preferred_element_type=F32)


def _proj_kernel(x_ref, g_ref, w_ref, ws_ref, o_ref, os_ref, h_ref):
    @pl.when(pl.program_id(1) == 0)
    def _():
        x = x_ref[...]
        h = x * lax.rsqrt(jnp.mean(x * x, axis=-1, keepdims=True) + EPS) * g_ref[...]
        h_ref[...] = h.astype(BF16)
        os_ref[...] = _dot(h_ref[...], ws_ref[...])

    o_ref[...] = _dot(h_ref[...], w_ref[...]).astype(o_ref.dtype)


def _proj_call(x2, gain, w_main, w_small):
    L, D = x2.shape
    N = w_main.shape[1]
    tm = min(1024, L)
    tn = 512
    return pl.pallas_call(
        _proj_kernel,
        grid=(L // tm, N // tn),
        in_specs=[
            pl.BlockSpec((tm, D), lambda i, j: (i, 0)),
            pl.BlockSpec((1, D), lambda i, j: (0, 0)),
            pl.BlockSpec((D, tn), lambda i, j: (0, j)),
            pl.BlockSpec((D, LANES), lambda i, j: (0, 0)),
        ],
        out_specs=[
            pl.BlockSpec((tm, tn), lambda i, j: (i, j)),
            pl.BlockSpec((tm, LANES), lambda i, j: (i, 0)),
        ],
        out_shape=[jax.ShapeDtypeStruct((L, N), BF16), jax.ShapeDtypeStruct((L, LANES), F32)],
        scratch_shapes=[pltpu.VMEM((tm, D), BF16)],
        compiler_params=_params(("parallel", "arbitrary")),
        name="proj",
    )(x2, gain, w_main, w_small)


def _gdn_prep_kernel(qkv_ref, halo_ref, sm_ref, cw_ref, alog_ref, dtb_ref,
                     q_ref, k_ref, v_ref, g_ref, b_ref, xs_ref):
    tT = qkv_ref.shape[0]
    first = pl.program_id(0) == 0
    xs_ref[8:8 + tT, :] = qkv_ref[...].astype(F32)
    halo = halo_ref[8:16, :].astype(F32)
    xs_ref[0:8, :] = jnp.where(first, 0.0, halo)
    nh = DN_HEADS * DN_DK
    for c in range(QKV_COLS // LANES):
        sl = slice(c * LANES, (c + 1) * LANES)
        y = xs_ref[8:8 + tT, sl] * cw_ref[3:4, sl]
        for j in range(DN_CONV - 1):
            y = y + xs_ref[5 + j:5 + j + tT, sl] * cw_ref[j:j + 1, sl]
        y = y * _sigmoid(y)
        if c < 2 * DN_HEADS:
            y = y * lax.rsqrt(jnp.sum(y * y, axis=-1, keepdims=True) + EPS)
            if c < DN_HEADS:
                q_ref[:, sl] = (y * (DN_DK ** -0.5)).astype(BF16)
            else:
                k_ref[:, c * LANES - nh:(c + 1) * LANES - nh] = y.astype(BF16)
        else:
            v_ref[:, c * LANES - 2 * nh:(c + 1) * LANES - 2 * nh] = y.astype(BF16)
    sm = sm_ref[...]
    b_ref[...] = _sigmoid(sm[:, SM_B:SM_B + DN_HEADS])
    a = sm[:, SM_A:SM_A + DN_HEADS] + dtb_ref[...]
    softplus = jnp.maximum(a, 0.0) + jnp.log1p(jnp.exp(-jnp.abs(a)))
    g_ref[...] = -jnp.exp(alog_ref[...]) * softplus


def _gdn_prep_call(proj, small, conv_w, a_log, dt_bias):
    L = proj.shape[0]
    tT = min(256, L)
    hb = tT // 16
    nh = DN_HEADS * DN_DK
    return pl.pallas_call(
        _gdn_prep_kernel,
        grid=(L // tT,),
        in_specs=[
            pl.BlockSpec((tT, QKV_COLS), lambda i: (i, 0)),
            pl.BlockSpec((16, QKV_COLS), lambda i: (jnp.maximum(i * hb - 1, 0), 0)),
            pl.BlockSpec((tT, LANES), lambda i: (i, 0)),
            pl.BlockSpec((DN_CONV, QKV_COLS), lambda i: (0, 0)),
            pl.BlockSpec((1, DN_HEADS), lambda i: (0, 0)),
            pl.BlockSpec((1, DN_HEADS), lambda i: (0, 0)),
        ],
        out_specs=[
            pl.BlockSpec((tT, nh), lambda i: (i, 0)),
            pl.BlockSpec((tT, nh), lambda i: (i, 0)),
            pl.BlockSpec((tT, nh), lambda i: (i, 0)),
            pl.BlockSpec((tT, DN_HEADS), lambda i: (i, 0)),
            pl.BlockSpec((tT, DN_HEADS), lambda i: (i, 0)),
        ],
        out_shape=[jax.ShapeDtypeStruct((L, nh), BF16)] * 3 + [jax.ShapeDtypeStruct((L, DN_HEADS), F32)] * 2,
        scratch_shapes=[pltpu.VMEM((tT + 8, QKV_COLS), F32)],
        compiler_params=_params(("parallel",)),
        name="gdn_prep",
    )(proj, proj, small, conv_w, a_log, dt_bias)


GDN_STEP = 4 * DN_CHUNK


def _bdot(a, b):
    return lax.dot_general(a, b, (((2,), (1,)), ((0,), (0,))), preferred_element_type=F32)


def _bdot_nt(a, b):
    return lax.dot_general(a, b, (((2,), (2,)), ((0,), (0,))), preferred_element_type=F32)


def _gdn_chunk_kernel(q_ref, k_ref, v_ref, z_ref, gc_ref, gr_ref, bc_ref, nw_ref, y_ref, s_ref):
    C, H = DN_CHUNK, DN_HEADS
    n_chunks = q_ref.shape[0] // C

    @pl.when(pl.program_id(0) == 0)
    def _():
        s_ref[...] = jnp.zeros_like(s_ref)

    row = lax.broadcasted_iota(I32, (C, C), 0)
    col = lax.broadcasted_iota(I32, (C, C), 1)
    incl = row >= col
    strict = row > col
    tri = incl.astype(F32)
    tri_t = (row <= col).astype(F32)
    eye = (row == col).astype(F32)

    def per_head(fn):
        return jnp.stack([fn(slice(c * C, (c + 1) * C), h) for c in range(n_chunks) for h in range(H)])

    head = lambda ref: per_head(lambda rs, h: ref[rs, h * DN_DK:(h + 1) * DN_DK])
    qb, kb, vb = head(q_ref), head(k_ref), head(v_ref)
    G_col = [jnp.dot(tri, gc_ref[c * C:(c + 1) * C, :], preferred_element_type=F32, precision=lax.Precision.HIGHEST)
             for c in range(n_chunks)]
    G_row = [jnp.dot(gr_ref[:, c * C:(c + 1) * C], tri_t, preferred_element_type=F32, precision=lax.Precision.HIGHEST)
             for c in range(n_chunks)]
    Gc = jnp.stack([G_col[c][:, h:h + 1] for c in range(n_chunks) for h in range(H)])
    Gr = jnp.stack([G_row[c][h:h + 1, :] for c in range(n_chunks) for h in range(H)])
    Gl = jnp.stack([G_col[c][C - 1:C, h:h + 1] for c in range(n_chunks) for h in range(H)])
    bcol = per_head(lambda rs, h: bc_ref[rs, h:h + 1])

    decay = jnp.exp(jnp.where(incl[None], Gc - Gr, -jnp.inf))
    A = jnp.where(strict[None], bcol * _bdot_nt(kb, kb) * decay, 0.0)
    M = -A
    T = eye[None] + M
    for _ in range(5):
        Mb = M.astype(BF16)
        M = _bdot(Mb, Mb)
        T = T + _bdot(T.astype(BF16), M.astype(BF16))
    Tb = T.astype(BF16)
    eg = jnp.exp(Gc)
    kf = kb.astype(F32)
    w = _bdot(Tb, (kf * (bcol * eg)).astype(BF16)).astype(BF16)
    u = _bdot(Tb, (vb.astype(F32) * bcol).astype(BF16))
    attn = (_bdot_nt(qb, kb) * decay).astype(BF16)
    q_dec = (qb.astype(F32) * eg).astype(BF16)
    k_dec = (kf * jnp.exp(Gl - Gc)).astype(BF16)
    g_last = jnp.exp(Gl)

    nw = nw_ref[...]
    S = s_ref[...]
    for c in range(n_chunks):
        rs = slice(c * C, (c + 1) * C)
        bs = slice(c * H, (c + 1) * H)
        Sb = S.astype(BF16)
        v_new = (u[bs] - _bdot(w[bs], Sb)).astype(BF16)
        o = _bdot(q_dec[bs], Sb) + _bdot(attn[bs], v_new)
        S = g_last[bs] * S + jnp.stack([
            lax.dot_general(k_dec[c * H + h], v_new[h], (((0,), (0,)), ((), ())), preferred_element_type=F32)
            for h in range(H)])
        on = o * lax.rsqrt(jnp.mean(o * o, axis=-1, keepdims=True) + EPS) * nw
        for h in range(H):
            hs = slice(h * DN_DV, (h + 1) * DN_DV)
            zz = z_ref[rs, hs].astype(F32)
            y_ref[rs, hs] = (on[h] * (zz * _sigmoid(zz))).astype(BF16)
    s_ref[...] = S


def _gdn_chunk_call(q, k, v, proj, g, g_t, beta, norm_w):
    L, nh = q.shape
    R = GDN_STEP
    zc = COL_Z // nh
    return pl.pallas_call(
        _gdn_chunk_kernel,
        grid=(L // R,),
        in_specs=[
            pl.BlockSpec((R, nh), lambda i: (i, 0)),
            pl.BlockSpec((R, nh), lambda i: (i, 0)),
            pl.BlockSpec((R, nh), lambda i: (i, 0)),
            pl.BlockSpec((R, nh), lambda i: (i, zc)),
            pl.BlockSpec((R, DN_HEADS), lambda i: (i, 0)),
            pl.BlockSpec((DN_HEADS, R), lambda i: (0, i)),
            pl.BlockSpec((R, DN_HEADS), lambda i: (i, 0)),
            pl.BlockSpec((1, DN_DV), lambda i: (0, 0)),
        ],
        out_specs=pl.BlockSpec((R, nh), lambda i: (i, 0)),
        out_shape=jax.ShapeDtypeStruct((L, nh), BF16),
        scratch_shapes=[pltpu.VMEM((DN_HEADS, DN_DK, DN_DV), F32)],
        compiler_params=_params(("arbitrary",)),
        name="gdn_chunk",
    )(q, k, v, proj, g, g_t, beta, norm_w)


def _rope_tables(pos, period, lane):
    rot = period // ROPE_FRACTION
    half = rot // 2
    lp = lane % period
    expo = -((lp % half).astype(F32) * 2.0 / rot)
    inv_freq = jnp.power(jnp.float32(ROPE_THETA), expo)
    ang = pos * inv_freq
    cos, sin = jnp.cos(ang), jnp.sin(ang)
    c = jnp.where(lp < rot, cos, 1.0)
    s_lo = jnp.where(lp < half, -sin, 0.0)
    s_hi = jnp.where((lp >= half) & (lp < rot), sin, 0.0)
    return c, s_lo, s_hi, half


def _rope(x, tab):
    c, s_lo, s_hi, half = tab
    return x * c + pltpu.roll(x, LANES - half, 1) * s_lo + pltpu.roll(x, half, 1) * s_hi


def _dsa_prep_kernel(q_ref, k_ref, v_ref, iq_ref, sm_ref, pos_ref, kn_ref,
                     qo_ref, ko_ref, vto_ref, iqo_ref, iko_ref, wto_ref, qno_ref, kno_ref):
    lane = lax.broadcasted_iota(I32, (1, LANES), 1)
    pos = pos_ref[...]
    tab_att = _rope_tables(pos, ATT_DIM, lane)
    tab_idx = _rope_tables(pos, IDX_DIM, lane)
    scale = ATT_DIM ** -0.5 * LOG2_E
    q_norms = jnp.zeros((q_ref.shape[0], LANES), F32)
    k_norms = jnp.zeros((q_ref.shape[0], LANES), F32)
    for h in range(ATT_HEADS):
        hs = slice(h * ATT_DIM, (h + 1) * ATT_DIM)
        qh = (_rope(q_ref[:, hs].astype(F32), tab_att) * scale).astype(BF16)
        kh = _rope(k_ref[:, hs].astype(F32), tab_att).astype(BF16)
        qo_ref[:, hs] = qh
        ko_ref[:, hs] = kh
        vto_ref[hs, :] = v_ref[:, hs].astype(F32).T.astype(BF16)
        norm = lambda t: jnp.sqrt(jnp.sum(t.astype(F32) ** 2, axis=-1, keepdims=True))
        q_norms = jnp.where(lane == h, norm(qh), q_norms)
        k_norms = jnp.where(lane == h, norm(kh), k_norms)
    qno_ref[...] = q_norms.T[0:ATT_HEADS, :]
    kno_ref[...] = k_norms.T[0:ATT_HEADS, :]
    low = lane < IDX_DIM
    for p in range(IDX_HEADS // 2):
        x = _rope(iq_ref[:, p * LANES:(p + 1) * LANES].astype(F32), tab_idx)
        iqo_ref[:, (2 * p) * LANES:(2 * p + 1) * LANES] = jnp.where(low, x, 0.0).astype(BF16)
        iqo_ref[:, (2 * p + 1) * LANES:(2 * p + 2) * LANES] = jnp.where(low, pltpu.roll(x, IDX_DIM, 1), 0.0).astype(BF16)
    sm = sm_ref[...]
    kx = jnp.where(low, 0.0, sm)
    kx = kx * lax.rsqrt(jnp.sum(kx * kx, axis=-1, keepdims=True) * (1.0 / IDX_DIM) + EPS) * kn_ref[...]
    kx = _rope(kx, tab_idx)
    iko_ref[...] = jnp.where(low, pltpu.roll(kx, IDX_DIM, 1), 0.0).astype(BF16)
    wto_ref[...] = (sm * (IDX_HEADS ** -0.5 * IDX_DIM ** -0.5)).T[SM_IXW:SM_IXW + IDX_HEADS, :]


def _dsa_prep_call(proj, small, pos_col, kn_lanes):
    L = proj.shape[0]
    tT = min(256, L)
    na = ATT_HEADS * ATT_DIM
    ni = IDX_HEADS * IDX_DIM
    return pl.pallas_call(
        _dsa_prep_kernel,
        grid=(L // tT,),
        in_specs=[
            pl.BlockSpec((tT, na), lambda i: (i, COL_ATQ // na)),
            pl.BlockSpec((tT, na), lambda i: (i, COL_ATK // na)),
            pl.BlockSpec((tT, na), lambda i: (i, COL_ATV // na)),
            pl.BlockSpec((tT, ni), lambda i: (i, COL_IXQ // ni)),
            pl.BlockSpec((tT, LANES), lambda i: (i, 0)),
            pl.BlockSpec((tT, 1), lambda i: (i, 0)),
            pl.BlockSpec((1, LANES), lambda i: (0, 0)),
        ],
        out_specs=[
            pl.BlockSpec((tT, na), lambda i: (i, 0)),
            pl.BlockSpec((tT, na), lambda i: (i, 0)),
            pl.BlockSpec((na, tT), lambda i: (0, i)),
            pl.BlockSpec((tT, IDX_HEADS * LANES), lambda i: (i, 0)),
            pl.BlockSpec((tT, LANES), lambda i: (i, 0)),
            pl.BlockSpec((IDX_HEADS, tT), lambda i: (0, i)),
            pl.BlockSpec((ATT_HEADS, tT), lambda i: (0, i)),
            pl.BlockSpec((ATT_HEADS, tT), lambda i: (0, i)),
        ],
        out_shape=[jax.ShapeDtypeStruct((L, na), BF16), jax.ShapeDtypeStruct((L, na), BF16),
                   jax.ShapeDtypeStruct((na, L), BF16),
                   jax.ShapeDtypeStruct((L, IDX_HEADS * LANES), BF16), jax.ShapeDtypeStruct((L, LANES), BF16),
                   jax.ShapeDtypeStruct((IDX_HEADS, L), F32), jax.ShapeDtypeStruct((ATT_HEADS, L), F32),
                   jax.ShapeDtypeStruct((ATT_HEADS, L), F32)],
        compiler_params=_params(("parallel",)),
        name="dsa_prep",
    )(proj, proj, proj, proj, small, pos_col, kn_lanes)


def _index_scores_t(ik_blk, iq_heads, w_rows):
    acc = None
    for qh, wh in zip(iq_heads, w_rows):
        term = wh * jnp.maximum(_dot_nt(ik_blk, qh), 0.0)
        acc = term if acc is None else acc + term
    return acc


def _sortable(bits):
    return jnp.where(bits < 0, bits ^ jnp.int32(0x7FFFFFFF), bits)


def _score_keys(s):
    key = _sortable(pltpu.bitcast(s, I32))
    return jnp.where(key == -1, 0, key)


DSA_TQ = 256
DSA_TK = 512
SUM_ROWS = 16
KEY_NEG_INF = int(np.int32(np.array(-np.inf, np.float32).view(np.int32)) ^ np.int32(0x7FFFFFFF))
INT_MIN = -(2 ** 31)
INT_MAX = 2 ** 31 - 1


def _dsa_kernel(q_ref, iq_ref, wt_ref, qn_ref, ik_ref, kn_ref, k_hbm, vt_hbm, o_ref,
                sc_ref, hi_ref, kbuf, vbuf, sem, tie_ref, bias_ref, *acc_refs, topk, tk):
    TQ = q_ref.shape[0]
    L = ik_ref.shape[0]
    t0 = pl.program_id(0) * TQ
    nkt = (t0 + TQ + tk - 1) // tk

    def kv_copies(kt, slot):
        off = pl.multiple_of(kt * tk, tk)
        return (pltpu.make_async_copy(k_hbm.at[pl.ds(off, tk)], kbuf.at[slot], sem.at[0, slot]),
                pltpu.make_async_copy(vt_hbm.at[:, pl.ds(off, tk)], vbuf.at[slot], sem.at[1, slot]))

    for cp in kv_copies(0, 0):
        cp.start()

    iq_heads = [iq_ref[:, h * LANES:(h + 1) * LANES] for h in range(IDX_HEADS)]
    w_rows = [wt_ref[h:h + 1, :] for h in range(IDX_HEADS)]
    qpos = t0 + lax.broadcasted_iota(I32, (tk, TQ), 1)
    krow = lax.broadcasted_iota(I32, (tk, TQ), 0)

    def fill(kt, carry, on_diagonal):
        off = pl.multiple_of(kt * tk, tk)
        s = _index_scores_t(ik_ref[pl.ds(off, tk), :], iq_heads, w_rows)
        if on_diagonal:
            s = jnp.where(krow + off <= qpos, s, -jnp.inf)
        keys = _score_keys(s)
        sc_ref[pl.ds(off, tk), :] = keys
        hi_ref[pl.ds(off, tk), :] = (keys >> 16).astype(I16)
        return carry

    n_below = (t0 + 1) // tk
    lax.fori_loop(0, n_below, functools.partial(fill, on_diagonal=False), 0)
    lax.fori_loop(n_below, nkt, functools.partial(fill, on_diagonal=True), 0)

    def count(pred):
        def body(kt, acc):
            off = pl.multiple_of(kt * tk, tk)
            m = pred(sc_ref[pl.ds(off, tk), :], krow + off).astype(I32)
            return acc + jnp.sum(m.reshape(tk // 32, 32, TQ), axis=0)
        acc = lax.fori_loop(0, nkt, body, jnp.zeros((32, TQ), I32))
        return jnp.sum(acc, axis=0, keepdims=True)

    def count_hi(cand):
        def body(kt, acc):
            off = pl.multiple_of(kt * tk, tk)
            m = (hi_ref[pl.ds(off, tk), :] >= cand).astype(I16)
            for g in range(tk // 32):
                acc = acc + m[g * 32:(g + 1) * 32]
            return acc
        acc = lax.fori_loop(0, nkt, body, jnp.zeros((32, TQ), I16))
        return jnp.sum(acc.astype(I32), axis=0, keepdims=True)

    def bit_cond(st):
        b, _, cnt = st
        return (b >= 0) & (jnp.max(jnp.abs(cnt - topk)) > 0)

    def hi_body(i, st):
        u, cnt = st
        uc = u | (jnp.int32(2 ** 15) >> i)
        c = count_hi((uc - 2 ** 15).astype(I16))
        ok = c >= topk
        return jnp.where(ok, uc, u), jnp.where(ok, c, cnt)

    cnt_all = jnp.zeros((1, TQ), I32) + nkt * tk
    u, cnt = lax.fori_loop(0, 16, hi_body, (jnp.zeros((1, TQ), I32), cnt_all))
    base = (u - 2 ** 15) << 16

    def lo_body(st):
        b, delta, cnt = st
        dc = delta | (jnp.int32(1) << b)
        cand = base + dc
        c = count(lambda keys, _: keys >= cand)
        ok = c >= topk
        return b - 1, jnp.where(ok, dc, delta), jnp.where(ok, c, cnt)

    _, delta, cnt = lax.while_loop(bit_cond, lo_body, (jnp.int32(15), jnp.zeros((1, TQ), I32), cnt))
    theta = base + delta

    none_valid = theta <= KEY_NEG_INF
    tied = (cnt > topk) & jnp.logical_not(none_valid)
    any_tied = jnp.max(tied.astype(I32)) > 0
    tie_ref[0:1, :] = jnp.where(none_valid, 0, INT_MAX)
    tie_ref[1:2, :] = jnp.zeros((1, TQ), I32)

    @pl.when(any_tied)
    def _():
        above = count(lambda keys, _: keys > theta)
        tie_ref[0:1, :] = jnp.where(tied, topk - above, tie_ref[0:1, :])

    need = tie_ref[0:1, :]
    theta_keep = jnp.where(need > 0, theta, theta + 1)

    ones_rows = jnp.ones((SUM_ROWS, tk), BF16)
    heads = [slice(h * ATT_DIM, (h + 1) * ATT_DIM) for h in range(ATT_HEADS)]

    def sweep(step, init):
        for acc_ref in acc_refs:
            acc_ref[...] = jnp.zeros_like(acc_ref)
        tie_ref[1:2, :] = jnp.zeros((1, TQ), I32)

        def body(kt, carry):
            slot = kt % 2
            off = pl.multiple_of(kt * tk, tk)
            for cp in kv_copies(kt, slot):
                cp.wait()

            @pl.when(kt + 1 < nkt)
            def _():
                for cp in kv_copies(kt + 1, 1 - slot):
                    cp.start()

            keys = sc_ref[pl.ds(off, tk), :]

            @pl.when(jnp.logical_not(any_tied))
            def _():
                bias_ref[...] = jnp.where(keys >= theta_keep, 0.0, NEG_BIG)

            @pl.when(any_tied)
            def _():
                tie = keys == theta
                r = lax.broadcasted_iota(I32, (tk, tk), 0)
                c = lax.broadcasted_iota(I32, (tk, tk), 1)
                earlier = _dot((r > c).astype(F32).astype(BF16), jnp.where(tie, 1.0, 0.0).astype(BF16))
                rank = earlier + tie_ref[1:2, :].astype(F32)
                keep = (keys > theta) | (tie & (rank < need.astype(F32)))
                bias_ref[...] = jnp.where(keep, 0.0, NEG_BIG)
                tie_ref[1:2, :] += jnp.sum(tie.astype(I32), axis=0, keepdims=True)

            return step(slot, carry)

        return lax.fori_loop(0, nkt, body, init)

    def v_ext(slot, h):
        return jnp.concatenate([vbuf[slot, heads[h], :], ones_rows], axis=0)

    shift = qn_ref[...] * jnp.max(kn_ref[...], axis=1, keepdims=True)

    def fixed_shift_step(slot, carry):
        probs = [jnp.exp2(_dot_nt(kbuf[slot, :, hs], q_ref[:, hs]) + bias_ref[...] - shift[h:h + 1, :]).astype(BF16)
                 for h, hs in enumerate(heads)]
        for h, acc_ref in enumerate(acc_refs):
            acc_ref[...] += _dot(v_ext(slot, h), probs[h])
        return carry

    sweep(fixed_shift_step, 0)
    norm_min = functools.reduce(jnp.minimum, [acc_ref[ATT_DIM:ATT_DIM + 1, :] for acc_ref in acc_refs])

    @pl.when(jnp.min(norm_min) < 2.0 ** -80)
    def _():
        def running_max_step(slot, m_run):
            logits, tile_max = [], []
            for hs in heads:
                s = _dot_nt(kbuf[slot, :, hs], q_ref[:, hs]) + bias_ref[...]
                logits.append(s)
                tile_max.append(jnp.max(s, axis=0, keepdims=True))
            m_rows = []
            for h, acc_ref in enumerate(acc_refs):
                m_old = m_run[h:h + 1, :]
                m_new = jnp.maximum(m_old, tile_max[h])
                p = jnp.exp2(logits[h] - m_new).astype(BF16)
                acc_ref[...] = jnp.exp2(m_old - m_new) * acc_ref[...] + _dot(v_ext(slot, h), p)
                m_rows.append(m_new)
            return jnp.concatenate(m_rows, axis=0)

        for cp in kv_copies(0, 0):
            cp.start()
        sweep(running_max_step, jnp.full((ATT_HEADS, TQ), NEG_BIG, F32))

    for h, acc_ref in enumerate(acc_refs):
        out_t = acc_ref[0:ATT_DIM, :] / acc_ref[ATT_DIM:ATT_DIM + 1, :]
        o_ref[:, h * ATT_DIM:(h + 1) * ATT_DIM] = out_t.T.astype(BF16)


def _dsa_call(q, k, vt, iq, ik, wt, qn, kn, topk):
    L, na = q.shape
    TQ = min(DSA_TQ, L)
    tk = min(DSA_TK, L)
    assert topk <= tk
    return pl.pallas_call(
        functools.partial(_dsa_kernel, topk=topk, tk=tk),
        grid=(L // TQ,),
        in_specs=[
            pl.BlockSpec((TQ, na), lambda i: (i, 0)),
            pl.BlockSpec((TQ, IDX_HEADS * LANES), lambda i: (i, 0)),
            pl.BlockSpec((IDX_HEADS, TQ), lambda i: (0, i)),
            pl.BlockSpec((ATT_HEADS, TQ), lambda i: (0, i)),
            pl.BlockSpec((L, LANES), lambda i: (0, 0)),
            pl.BlockSpec((ATT_HEADS, L), lambda i: (0, 0)),
            pl.BlockSpec(memory_space=pl.ANY),
            pl.BlockSpec(memory_space=pl.ANY),
        ],
        out_specs=pl.BlockSpec((TQ, na), lambda i: (i, 0)),
        out_shape=jax.ShapeDtypeStruct((L, na), BF16),
        scratch_shapes=[
            pltpu.VMEM((L, TQ), I32),
            pltpu.VMEM((L, TQ), I16),
            pltpu.VMEM((2, tk, na), BF16),
            pltpu.VMEM((2, na, tk), BF16),
            pltpu.SemaphoreType.DMA((2, 2)),
            pltpu.VMEM((8, TQ), I32),
            pltpu.VMEM((tk, TQ), F32),
        ] + [pltpu.VMEM((ATT_DIM + SUM_ROWS, TQ), F32)] * ATT_HEADS,
        compiler_params=_params(("parallel",), vmem_mb=56),
        name="dsa",
    )(q, iq, wt, qn, ik, kn, k, vt)


def _merge_kernel(g0_ref, g1_ref, b0_ref, b1_ref, ydn_ref, yat_ref, wdn_ref, wat_ref, o_ref):
    gate0 = _sigmoid(g0_ref[...].astype(F32) + b0_ref[...])
    gate1 = _sigmoid(g1_ref[...].astype(F32) + b1_ref[...])
    merged = gate0 * _dot(ydn_ref[...], wdn_ref[...]) + gate1 * _dot(yat_ref[...], wat_ref[...])
    o_ref[...] = merged.astype(o_ref.dtype)


def _merge_call(proj, b_gates, y_dn, y_at, w_dn, w_at):
    L = proj.shape[0]
    tm = min(1024, L)
    tn = 512
    nj = D_MODEL // tn
    c0 = COL_GATE // tn
    kd = y_dn.shape[1]
    return pl.pallas_call(
        _merge_kernel,
        grid=(L // tm, nj),
        in_specs=[
            pl.BlockSpec((tm, tn), lambda i, j: (i, c0 + j)),
            pl.BlockSpec((tm, tn), lambda i, j: (i, c0 + nj + j)),
            pl.BlockSpec((1, tn), lambda i, j: (0, j)),
            pl.BlockSpec((1, tn), lambda i, j: (0, nj + j)),
            pl.BlockSpec((tm, kd), lambda i, j: (i, 0)),
            pl.BlockSpec((tm, kd), lambda i, j: (i, 0)),
            pl.BlockSpec((kd, tn), lambda i, j: (0, j)),
            pl.BlockSpec((kd, tn), lambda i, j: (0, j)),
        ],
        out_specs=pl.BlockSpec((tm, tn), lambda i, j: (i, j)),
        out_shape=jax.ShapeDtypeStruct((L, D_MODEL), BF16),
        compiler_params=_params(("parallel", "arbitrary")),
        name="merge",
    )(proj, proj, b_gates, b_gates, y_dn, y_at, w_dn, w_at)


RT_E0, RT_E1, RT_W0, RT_W1 = 0, 1, 2, 3


def _first_lane_of_max(v, lane):
    m = jnp.max(v, axis=-1, keepdims=True)
    return m, jnp.min(jnp.where(v == m, lane, LANES), axis=-1, keepdims=True)


def _outproj_kernel(x_ref, mg_ref, wo_ref, nf_ref, wr_ref, br_ref, x1_ref, h2_ref, rt_ref):
    x1 = x_ref[...] + _dot(mg_ref[...], wo_ref[...])
    x1_ref[...] = x1
    h2 = x1 * lax.rsqrt(jnp.mean(x1 * x1, axis=-1, keepdims=True) + EPS) * nf_ref[...]
    h2_ref[...] = h2
    lg = _dot(h2.astype(BF16), wr_ref[...]) + br_ref[...]
    lane = lax.broadcasted_iota(I32, (1, LANES), 1)
    ninf = -jnp.inf
    gl = jnp.where(lane < N_GROUPS, lg, ninf)
    gmax, g_sel = _first_lane_of_max(gl, lane)
    p_group = 1.0 / jnp.sum(jnp.exp(gl - gmax), axis=-1, keepdims=True)
    ex = lane - N_GROUPS
    in_group = (ex >= 0) & (ex < N_EXPERTS) & ((ex // EXPERTS_PER_GROUP) == g_sel)
    el = jnp.where(in_group, lg, ninf)
    m1, i1 = _first_lane_of_max(el, lane)
    m2, i2 = _first_lane_of_max(jnp.where(lane == i1, ninf, el), lane)
    e2 = jnp.exp(m2 - m1)
    w0 = p_group / (1.0 + e2)
    w1 = p_group * e2 / (1.0 + e2)
    rec = jnp.where(lane == RT_E0, (i1 - N_GROUPS).astype(F32), 0.0)
    rec = jnp.where(lane == RT_E1, (i2 - N_GROUPS).astype(F32), rec)
    rec = jnp.where(lane == RT_W0, w0, rec)
    rt_ref[...] = jnp.where(lane == RT_W1, w1, rec)


def _outproj_call(x2, merged, w_out, norm_ffn, w_route, b_route):
    L = x2.shape[0]
    tm = min(256, L)
    row = lambda i: (i, 0)
    fixed = lambda i: (0, 0)
    return pl.pallas_call(
        _outproj_kernel,
        grid=(L // tm,),
        in_specs=[
            pl.BlockSpec((tm, D_MODEL), row),
            pl.BlockSpec((tm, D_MODEL), row),
            pl.BlockSpec((D_MODEL, D_MODEL), fixed),
            pl.BlockSpec((1, D_MODEL), fixed),
            pl.BlockSpec((D_MODEL, LANES), fixed),
            pl.BlockSpec((1, LANES), fixed),
        ],
        out_specs=[pl.BlockSpec((tm, D_MODEL), row), pl.BlockSpec((tm, D_MODEL), row), pl.BlockSpec((tm, LANES), row)],
        out_shape=[jax.ShapeDtypeStruct((L, D_MODEL), F32), jax.ShapeDtypeStruct((L, D_MODEL), F32),
                   jax.ShapeDtypeStruct((L, LANES), F32)],
        compiler_params=_params(("parallel",)),
        name="outproj_route",
    )(x2, merged, w_out, norm_ffn, w_route, b_route)


def _moe_slots(L):
    return -(-(2 * L + N_EXPERTS * (MOE_ROWS - 1)) // MOE_ROWS) * MOE_ROWS


def _moe_plan_kernel(rt_ref, dest_ref, blk_ref, cnt_ref, carry_ref, start_ref):
    phase, i = pl.program_id(0), pl.program_id(1)
    tR = rt_ref.shape[0]
    lane = lax.broadcasted_iota(I32, (1, LANES), 1)
    lane_f = lane.astype(F32)
    rt = rt_ref[...]
    e0, e1 = rt[:, RT_E0:RT_E0 + 1], rt[:, RT_E1:RT_E1 + 1]
    hit0, hit1 = lane_f == e0, lane_f == e1
    onehot = (hit0 | hit1).astype(F32)
    colsum = jnp.sum(onehot, axis=0, keepdims=True)

    @pl.when((phase == 0) & (i == 0))
    def _():
        cnt_ref[...] = jnp.zeros_like(cnt_ref)

    @pl.when(phase == 0)
    def _():
        cnt_ref[...] += colsum

    @pl.when((phase == 1) & (i == 0))
    def _():
        carry_ref[...] = jnp.zeros_like(carry_ref)
        padded = jnp.floor((cnt_ref[...] + (MOE_ROWS - 1)) * (1.0 / MOE_ROWS)) * MOE_ROWS
        r = lax.broadcasted_iota(I32, (LANES, LANES), 0)
        c = lax.broadcasted_iota(I32, (LANES, LANES), 1)
        upper = (r <= c).astype(F32)
        end = jnp.dot(jnp.broadcast_to(padded, (8, LANES)), upper, preferred_element_type=F32,
                      precision=lax.Precision.HIGHEST)[0:1, :]
        start_ref[...] = end - padded
        n_used = end[:, N_EXPERTS - 1:N_EXPERTS] * (1.0 / MOE_ROWS)
        nb = blk_ref.shape[0] - 8
        b = lax.broadcasted_iota(I32, (nb, 1), 0).astype(F32)
        b_eff = jnp.minimum(b, n_used - 1.0)
        done = ((end <= b_eff * MOE_ROWS) & (lane < N_EXPERTS)).astype(F32)
        blk_e = jnp.minimum(jnp.sum(done, axis=-1, keepdims=True), N_EXPERTS - 1.0)
        blk_ref[0:nb, :] = jnp.where(lane == 0, blk_e, jnp.where(lane == 1, n_used, 0.0)).astype(I32)
        row = lax.broadcasted_iota(I32, (8, LANES), 0)
        seg = jnp.where(row == 0, end - padded, jnp.where(row == 1, cnt_ref[...], jnp.where(row == 2, padded, 0.0)))
        blk_ref[nb:nb + 8, :] = seg.astype(I32)

    @pl.when(phase == 1)
    def _():
        r = lax.broadcasted_iota(I32, (tR, tR), 0)
        c = lax.broadcasted_iota(I32, (tR, tR), 1)
        before = _dot((r > c).astype(BF16), onehot.astype(BF16)) + carry_ref[...]
        slot = before + start_ref[...]
        d0 = jnp.sum(jnp.where(hit0, slot, 0.0), axis=-1, keepdims=True)
        d1 = jnp.sum(jnp.where(hit1, slot, 0.0), axis=-1, keepdims=True)
        dest_ref[...] = jnp.where(lane == 0, d0, jnp.where(lane == 1, d1, 0.0)).astype(I32)
        carry_ref[...] += colsum


def _moe_plan_call(route):
    L = route.shape[0]
    tR = min(256, L)
    nb = _moe_slots(L) // MOE_ROWS
    nb_pad = -(-nb // 8) * 8
    return pl.pallas_call(
        _moe_plan_kernel,
        grid=(2, L // tR),
        in_specs=[pl.BlockSpec((tR, LANES), lambda p, i: (i, 0))],
        out_specs=[pl.BlockSpec((tR, LANES), lambda p, i: (p * i, 0)),
                   pl.BlockSpec((nb_pad + 8, LANES), lambda p, i: (0, 0))],
        out_shape=[jax.ShapeDtypeStruct((L, LANES), I32), jax.ShapeDtypeStruct((nb_pad + 8, LANES), I32)],
        scratch_shapes=[pltpu.VMEM((1, LANES), F32)] * 3,
        compiler_params=_params(("arbitrary", "arbitrary")),
        name="moe_plan",
    )(route)


PLAN_START, PLAN_COUNT, PLAN_PADDED, PLAN_USED = 0, N_EXPERTS, 2 * N_EXPERTS, 3 * N_EXPERTS
PAD_PIECES = tuple(MOE_ROWS >> s for s in range(1, MOE_ROWS.bit_length() - 3))


def _moe_scatter_kernel(dest_ref, plan_ref, h_ref, xs_out, zero_ref, sem, zero_sem, *, rows, nb):
    base = pl.program_id(0) * rows

    def row_copy(r, j):
        return pltpu.make_async_copy(h_ref.at[pl.ds(r, 1)], xs_out.at[pl.ds(dest_ref[2 * (base + r) + j], 1)], sem)

    def start(r, c):
        row_copy(r, 0).start()
        row_copy(r, 1).start()
        return c

    def wait(r, c):
        row_copy(r, 0).wait()
        row_copy(r, 1).wait()
        return c

    lax.fori_loop(0, rows, start, 0)

    @pl.when(pl.program_id(0) == 0)
    def _():
        zero_ref[...] = jnp.zeros_like(zero_ref)

        def for_each_pad_piece(act):
            def per_expert(e, c):
                count = plan_ref[PLAN_COUNT + e]
                pad = plan_ref[PLAN_PADDED + e] - count
                first = plan_ref[PLAN_START + e] + count
                end = first + pad

                def single_row(r, cc):
                    act(pltpu.make_async_copy(zero_ref.at[pl.ds(0, 1)], xs_out.at[pl.ds(first + r, 1)], zero_sem))
                    return cc

                lax.fori_loop(0, pad & 7, single_row, 0)
                for k in PAD_PIECES:
                    @pl.when((pad & k) != 0)
                    def _():
                        at = pl.multiple_of(end - (pad & ~(k - 1)), 8)
                        act(pltpu.make_async_copy(zero_ref.at[pl.ds(0, k)], xs_out.at[pl.ds(at, k)], zero_sem))
                return c
            lax.fori_loop(0, N_EXPERTS, per_expert, 0)

        def for_each_unused_block(act):
            def per_block(b, c):
                act(pltpu.make_async_copy(zero_ref, xs_out.at[pl.ds(b * MOE_ROWS, MOE_ROWS)], zero_sem))
                return c
            lax.fori_loop(plan_ref[PLAN_USED], nb, per_block, 0)

        for_each_pad_piece(lambda cp: cp.start())
        for_each_unused_block(lambda cp: cp.start())
        for_each_pad_piece(lambda cp: cp.wait())
        for_each_unused_block(lambda cp: cp.wait())

    lax.fori_loop(0, rows, wait, 0)


def _moe_scatter_call(dest_flat, plan, h2):
    L = h2.shape[0]
    rows = min(256, L)
    slots = _moe_slots(L)
    return pl.pallas_call(
        functools.partial(_moe_scatter_kernel, rows=rows, nb=slots // MOE_ROWS),
        grid_spec=pltpu.PrefetchScalarGridSpec(
            num_scalar_prefetch=2,
            grid=(L // rows,),
            in_specs=[pl.BlockSpec((rows, D_MODEL), lambda i, d, p: (i, 0))],
            out_specs=pl.BlockSpec(memory_space=pl.ANY),
            scratch_shapes=[pltpu.VMEM((MOE_ROWS, D_MODEL), F32), pltpu.SemaphoreType.DMA(()),
                            pltpu.SemaphoreType.DMA(())],
        ),
        out_shape=jax.ShapeDtypeStruct((slots, D_MODEL), F32),
        compiler_params=_params(("arbitrary",)),
        name="moe_scatter",
    )(dest_flat, plan, h2)


def _moe_ffn_kernel(be_ref, x_ref, wg_ref, wu_ref, wd_ref, y_ref, *, nb):
    b = pl.program_id(0)

    @pl.when(b < be_ref[nb])
    def _():
        xb = x_ref[...].astype(BF16)
        gate = _dot(xb, wg_ref[...].astype(BF16))
        up = _dot(xb, wu_ref[...].astype(BF16))
        hidden = (gate * _sigmoid(gate) * up).astype(BF16)
        y_ref[...] = _dot(hidden, wd_ref[...].astype(BF16))

    @pl.when(b >= be_ref[nb])
    def _():
        y_ref[...] = jnp.zeros_like(y_ref)


def _moe_ffn_call(blk_e, xs, w_gate, w_up, w_down):
    P = xs.shape[0]
    nb = P // MOE_ROWS
    return pl.pallas_call(
        functools.partial(_moe_ffn_kernel, nb=nb),
        grid_spec=pltpu.PrefetchScalarGridSpec(
            num_scalar_prefetch=1,
            grid=(nb,),
            in_specs=[
                pl.BlockSpec((MOE_ROWS, D_MODEL), lambda b, be: (b, 0)),
                pl.BlockSpec((None, D_MODEL, EXPERT_HIDDEN), lambda b, be: (be[b], 0, 0)),
                pl.BlockSpec((None, D_MODEL, EXPERT_HIDDEN), lambda b, be: (be[b], 0, 0)),
                pl.BlockSpec((None, EXPERT_HIDDEN, D_MODEL), lambda b, be: (be[b], 0, 0)),
            ],
            out_specs=pl.BlockSpec((MOE_ROWS, D_MODEL), lambda b, be: (b, 0)),
        ),
        out_shape=jax.ShapeDtypeStruct((P, D_MODEL), F32),
        compiler_params=_params(("arbitrary",), vmem_mb=56),
        name="moe_ffn",
    )(blk_e, xs, w_gate, w_up, w_down)


def _moe_combine_kernel(dest_ref, x1_ref, rt_ref, nf_ref, ys_hbm, o_ref, ya_ref, yb_ref, sem, *, rows):
    step, n_steps = pl.program_id(0), pl.num_programs(0)

    def gather(s, act):
        slot = s % 2

        def body(r, c):
            for j, dst in enumerate((ya_ref, yb_ref)):
                src = ys_hbm.at[pl.ds(dest_ref[2 * (s * rows + r) + j], 1)]
                act(pltpu.make_async_copy(src, dst.at[slot, pl.ds(r, 1)], sem.at[slot]))
            return c

        lax.fori_loop(0, rows, body, 0)

    @pl.when(step == 0)
    def _():
        gather(step, lambda cp: cp.start())

    @pl.when(step + 1 < n_steps)
    def _():
        gather(step + 1, lambda cp: cp.start())

    gather(step, lambda cp: cp.wait())
    slot = step % 2
    rt = rt_ref[...]
    x = x1_ref[...] + rt[:, RT_W0:RT_W0 + 1] * ya_ref[slot] + rt[:, RT_W1:RT_W1 + 1] * yb_ref[slot]
    o_ref[...] = x * lax.rsqrt(jnp.mean(x * x, axis=-1, keepdims=True) + EPS) * nf_ref[...]


def _moe_combine_call(dest_flat, x1, route, norm_final, ys):
    L = x1.shape[0]
    rows = min(256, L)
    return pl.pallas_call(
        functools.partial(_moe_combine_kernel, rows=rows),
        grid_spec=pltpu.PrefetchScalarGridSpec(
            num_scalar_prefetch=1,
            grid=(L // rows,),
            in_specs=[
                pl.BlockSpec((rows, D_MODEL), lambda i, d: (i, 0)),
                pl.BlockSpec((rows, LANES), lambda i, d: (i, 0)),
                pl.BlockSpec((1, D_MODEL), lambda i, d: (0, 0)),
                pl.BlockSpec(memory_space=pl.ANY),
            ],
            out_specs=pl.BlockSpec((rows, D_MODEL), lambda i, d: (i, 0)),
            scratch_shapes=[pltpu.VMEM((2, rows, D_MODEL), F32), pltpu.VMEM((2, rows, D_MODEL), F32),
                            pltpu.SemaphoreType.DMA((2,))],
        ),
        out_shape=jax.ShapeDtypeStruct((L, D_MODEL), F32),
        compiler_params=_params(("arbitrary",)),
        name="moe_combine",
    )(dest_flat, x1, route, norm_final, ys)


def _pack_w_in(w):
    s = np.cumsum((0, QKV_COLS, DN_HEADS * DN_DV, DN_HEADS, DN_HEADS, ATT_HEADS * ATT_DIM, ATT_HEADS * ATT_DIM,
                   ATT_HEADS * ATT_DIM, IDX_HEADS * IDX_DIM, IDX_DIM, IDX_HEADS, 2 * D_MODEL))
    seg = lambda n: w[:, int(s[n]):int(s[n + 1])]
    w_main = jnp.concatenate([seg(0), seg(1), seg(4), seg(5), seg(6), seg(7), seg(10)], axis=1).astype(BF16)
    pad = jnp.zeros((w.shape[0], SM_IXK - SM_IXW - IDX_HEADS), w.dtype)
    w_small = jnp.concatenate([seg(2), seg(3), seg(9), pad, seg(8)], axis=1).astype(BF16)
    return w_main, w_small


def _forward(x, positions, norm_mix, w_in, b_gates, dn_conv_w, dn_a_log, dn_dt_bias, dn_norm_w, idx_k_norm,
             w_proj_dn, w_proj_att, w_out, norm_ffn, w_group, b_group, w_router, b_router, w_exp_gate, w_exp_up,
             w_exp_down, norm_final):
    st = {}
    L = x.shape[1]
    x2 = x.reshape(L, D_MODEL)
    w_main, w_small = _pack_w_in(w_in[0])
    proj, small = _proj_call(x2, norm_mix[0].reshape(1, D_MODEL), w_main, w_small)
    st["proj"], st["small"] = proj, small
    q, k, v, g, beta = _gdn_prep_call(proj, small, dn_conv_w[0], dn_a_log[0].reshape(1, DN_HEADS),
                                      dn_dt_bias[0].reshape(1, DN_HEADS))
    st["y_dn"] = _gdn_chunk_call(q, k, v, proj, g, g.T, beta, dn_norm_w[0].reshape(1, DN_DV))

    pos_col = positions.reshape(L, 1).astype(F32)
    kn_lanes = jnp.concatenate([jnp.zeros((SM_IXK,), F32), idx_k_norm[0].astype(F32)]).reshape(1, LANES)
    aq, ak, avt, iq, ik, wt, qn, kn = _dsa_prep_call(proj, small, pos_col, kn_lanes)
    st["y_at"] = _dsa_call(aq, ak, avt, iq, ik, wt, qn, kn, min(TOPK_MAX, L // 4))

    merged = _merge_call(proj, b_gates[0].reshape(1, 2 * D_MODEL), st["y_dn"], st["y_at"],
                         w_proj_dn[0].astype(BF16), w_proj_att[0].astype(BF16))
    st["merged"] = merged
    n_route = N_GROUPS + N_EXPERTS
    w_route = jnp.concatenate([w_group[0], w_router[0], jnp.zeros((D_MODEL, LANES - n_route), F32)], axis=1).astype(BF16)
    b_route = jnp.concatenate([b_group[0], b_router[0], jnp.zeros((LANES - n_route,), F32)]).reshape(1, LANES)
    x1, h2, route = _outproj_call(x2, merged, w_out[0].astype(BF16), norm_ffn[0].reshape(1, D_MODEL), w_route, b_route)
    st["x1"], st["h2"], st["route"] = x1, h2, route

    dest, blk = _moe_plan_call(route)
    dest_flat = dest[:, :2].reshape(2 * L)
    slots = _moe_slots(L)
    nb = slots // MOE_ROWS
    blk_e = jnp.concatenate([blk[:nb, 0], blk[0:1, 1]])
    seg = blk[blk.shape[0] - 8:blk.shape[0] - 5, :N_EXPERTS]
    plan = jnp.concatenate([seg.reshape(3 * N_EXPERTS), blk[0:1, 1]])
    xs = _moe_scatter_call(dest_flat, plan, h2)
    ys = _moe_ffn_call(blk_e, xs, w_exp_gate[0], w_exp_up[0], w_exp_down[0])
    out = _moe_combine_call(dest_flat, x1, route, norm_final.reshape(1, D_MODEL), ys)
    st["out"] = out.reshape(1, L, D_MODEL)
    return st


def kernel(x, positions, norm_mix, w_in, b_gates, dn_conv_w, dn_a_log, dn_dt_bias, dn_norm_w, idx_k_norm, w_proj_dn,
           w_proj_att, w_out, norm_ffn, w_group, b_group, w_router, b_router, w_exp_gate, w_exp_up, w_exp_down,
           norm_final):
    return _forward(x, positions, norm_mix, w_in, b_gates, dn_conv_w, dn_a_log, dn_dt_bias, dn_norm_w, idx_k_norm,
                    w_proj_dn, w_proj_att, w_out, norm_ffn, w_group, b_group, w_router, b_router, w_exp_gate,
                    w_exp_up, w_exp_down, norm_final)["out"]


def _stages(d, upto=None):
    return _forward(*[d[n] for n in ("x", "positions", "norm_mix", "w_in", "b_gates", "dn_conv_w", "dn_a_log",
                                     "dn_dt_bias", "dn_norm_w", "idx_k_norm", "w_proj_dn", "w_proj_att", "w_out",
                                     "norm_ffn", "w_group", "b_group", "w_router", "b_router", "w_exp_gate",
                                     "w_exp_up", "w_exp_down", "norm_final")])
```

```python
import functools

import jax
import jax.numpy as jnp
import numpy as np
from jax import lax
from jax.experimental import pallas as pl
from jax.experimental.pallas import tpu as pltpu

D_MODEL = 2048
DN_HEADS = 8
DN_DK = 128
DN_DV = 128
DN_CONV = 4
DN_CHUNK = 64
ATT_HEADS = 8
ATT_DIM = 128
IDX_HEADS = 8
IDX_DIM = 64
TOPK_MAX = 256
ROPE_THETA = 500000.0
ROPE_FRACTION = 4
N_GROUPS = 8
EXPERTS_PER_GROUP = 8
N_EXPERTS = N_GROUPS * EXPERTS_PER_GROUP
EXPERT_HIDDEN = 512
EPS = 1e-6

LANES = 128
MOE_ROWS = 256
NEG_BIG = -1e30
LOG2_E = 1.4426950408889634

F32 = jnp.float32
BF16 = jnp.bfloat16
I32 = jnp.int32
I16 = jnp.int16

QKV_COLS = 2 * DN_HEADS * DN_DK + DN_HEADS * DN_DV
COL_QKV = 0
COL_Z = COL_QKV + QKV_COLS
COL_ATQ = COL_Z + DN_HEADS * DN_DV
COL_ATK = COL_ATQ + ATT_HEADS * ATT_DIM
COL_ATV = COL_ATK + ATT_HEADS * ATT_DIM
COL_IXQ = COL_ATV + ATT_HEADS * ATT_DIM
COL_GATE = COL_IXQ + IDX_HEADS * IDX_DIM
MAIN_COLS = COL_GATE + 2 * D_MODEL
SM_B = 0
SM_A = 8
SM_IXW = 16
SM_IXK = 64


def _params(sem, vmem_mb=48):
    return pltpu.CompilerParams(dimension_semantics=sem, vmem_limit_bytes=vmem_mb * 1024 * 1024)


def _sigmoid(x):
    return 1.0 / (1.0 + jnp.exp(-x))


def _dot(a, b):
    return jnp.dot(a, b, preferred_element_type=F32)


def _dot_nt(a, b):
    return lax.dot_general(a, b, (((1,), (1,)), ((), ())), preferred_element_type=F32)


def _proj_kernel(x_ref, g_ref, w_ref, ws_ref, o_ref, os_ref, h_ref):
    @pl.when(pl.program_id(1) == 0)
    def _():
        x = x_ref[...]
        h = x * lax.rsqrt(jnp.mean(x * x, axis=-1, keepdims=True) + EPS) * g_ref[...]
        h_ref[...] = h.astype(BF16)
        os_ref[...] = _dot(h_ref[...], ws_ref[...])

    o_ref[...] = _dot(h_ref[...], w_ref[...]).astype(o_ref.dtype)


def _proj_call(x2, gain, w_main, w_small):
    L, D = x2.shape
    N = w_main.shape[1]
    tm = min(512, L)
    tn = N // 4
    return pl.pallas_call(
        _proj_kernel,
        grid=(L // tm, N // tn),
        in_specs=[
            pl.BlockSpec((tm, D), lambda i, j: (i, 0)),
            pl.BlockSpec((1, D), lambda i, j: (0, 0)),
            pl.BlockSpec((D, tn), lambda i, j: (0, j)),
            pl.BlockSpec((D, LANES), lambda i, j: (0, 0)),
        ],
        out_specs=[
            pl.BlockSpec((tm, tn), lambda i, j: (i, j)),
            pl.BlockSpec((tm, LANES), lambda i, j: (i, 0)),
        ],
        out_shape=[jax.ShapeDtypeStruct((L, N), BF16), jax.ShapeDtypeStruct((L, LANES), F32)],
        scratch_shapes=[pltpu.VMEM((tm, D), BF16)],
        compiler_params=_params(("parallel", "arbitrary")),
        name="proj",
    )(x2, gain, w_main, w_small)


def _gdn_prep_kernel(qkv_ref, halo_ref, sm_ref, cw_ref, alog_ref, dtb_ref,
                     q_ref, k_ref, v_ref, g_ref, b_ref, xs_ref):
    tT = qkv_ref.shape[0]
    first = pl.program_id(0) == 0
    xs_ref[8:8 + tT, :] = qkv_ref[...].astype(F32)
    halo = halo_ref[8:16, :].astype(F32)
    xs_ref[0:8, :] = jnp.where(first, 0.0, halo)
    nh = DN_HEADS * DN_DK
    for c in range(QKV_COLS // LANES):
        sl = slice(c * LANES, (c + 1) * LANES)
        y = xs_ref[8:8 + tT, sl] * cw_ref[3:4, sl]
        for j in range(DN_CONV - 1):
            y = y + xs_ref[5 + j:5 + j + tT, sl] * cw_ref[j:j + 1, sl]
        y = y * _sigmoid(y)
        if c < 2 * DN_HEADS:
            y = y * lax.rsqrt(jnp.sum(y * y, axis=-1, keepdims=True) + EPS)
            if c < DN_HEADS:
                q_ref[:, sl] = (y * (DN_DK ** -0.5)).astype(BF16)
            else:
                k_ref[:, c * LANES - nh:(c + 1) * LANES - nh] = y.astype(BF16)
        else:
            v_ref[:, c * LANES - 2 * nh:(c + 1) * LANES - 2 * nh] = y.astype(BF16)
    sm = sm_ref[...]
    b_ref[...] = _sigmoid(sm[:, SM_B:SM_B + DN_HEADS])
    a = sm[:, SM_A:SM_A + DN_HEADS] + dtb_ref[...]
    softplus = jnp.maximum(a, 0.0) + jnp.log1p(jnp.exp(-jnp.abs(a)))
    g_ref[...] = -jnp.exp(alog_ref[...]) * softplus


def _gdn_prep_call(proj, small, conv_w, a_log, dt_bias):
    L = proj.shape[0]
    tT = min(256, L)
    hb = tT // 16
    nh = DN_HEADS * DN_DK
    return pl.pallas_call(
        _gdn_prep_kernel,
        grid=(L // tT,),
        in_specs=[
            pl.BlockSpec((tT, QKV_COLS), lambda i: (i, 0)),
            pl.BlockSpec((16, QKV_COLS), lambda i: (jnp.maximum(i * hb - 1, 0), 0)),
            pl.BlockSpec((tT, LANES), lambda i: (i, 0)),
            pl.BlockSpec((DN_CONV, QKV_COLS), lambda i: (0, 0)),
            pl.BlockSpec((1, DN_HEADS), lambda i: (0, 0)),
            pl.BlockSpec((1, DN_HEADS), lambda i: (0, 0)),
        ],
        out_specs=[
            pl.BlockSpec((tT, nh), lambda i: (i, 0)),
            pl.BlockSpec((tT, nh), lambda i: (i, 0)),
            pl.BlockSpec((tT, nh), lambda i: (i, 0)),
            pl.BlockSpec((tT, DN_HEADS), lambda i: (i, 0)),
            pl.BlockSpec((tT, DN_HEADS), lambda i: (i, 0)),
        ],
        out_shape=[jax.ShapeDtypeStruct((L, nh), BF16)] * 3 + [jax.ShapeDtypeStruct((L, DN_HEADS), F32)] * 2,
        scratch_shapes=[pltpu.VMEM((tT + 8, QKV_COLS), F32)],
        compiler_params=_params(("parallel",)),
        name="gdn_prep",
    )(proj, proj, small, conv_w, a_log, dt_bias)


GDN_STEP = 4 * DN_CHUNK


def _bdot(a, b):
    return lax.dot_general(a, b, (((2,), (1,)), ((0,), (0,))), preferred_element_type=F32)


def _bdot_nt(a, b):
    return lax.dot_general(a, b, (((2,), (2,)), ((0,), (0,))), preferred_element_type=F32)


def _gdn_chunk_kernel(q_ref, k_ref, v_ref, z_ref, gc_ref, gr_ref, bc_ref, nw_ref, y_ref, s_ref):
    C, H = DN_CHUNK, DN_HEADS
    n_chunks = q_ref.shape[0] // C

    @pl.when(pl.program_id(0) == 0)
    def _():
        s_ref[...] = jnp.zeros_like(s_ref)

    row = lax.broadcasted_iota(I32, (C, C), 0)
    col = lax.broadcasted_iota(I32, (C, C), 1)
    incl = row >= col
    strict = row > col
    tri = incl.astype(F32)
    tri_t = (row <= col).astype(F32)
    eye = (row == col).astype(F32)

    def per_head(fn):
        return jnp.stack([fn(slice(c * C, (c + 1) * C), h) for c in range(n_chunks) for h in range(H)])

    head = lambda ref: per_head(lambda rs, h: ref[rs, h * DN_DK:(h + 1) * DN_DK])
    qb, kb, vb = head(q_ref), head(k_ref), head(v_ref)
    G_col = [jnp.dot(tri, gc_ref[c * C:(c + 1) * C, :], preferred_element_type=F32, precision=lax.Precision.HIGHEST)
             for c in range(n_chunks)]
    G_row = [jnp.dot(gr_ref[:, c * C:(c + 1) * C], tri_t, preferred_element_type=F32, precision=lax.Precision.HIGHEST)
             for c in range(n_chunks)]
    Gc = jnp.stack([G_col[c][:, h:h + 1] for c in range(n_chunks) for h in range(H)])
    Gr = jnp.stack([G_row[c][h:h + 1, :] for c in range(n_chunks) for h in range(H)])
    Gl = jnp.stack([G_col[c][C - 1:C, h:h + 1] for c in range(n_chunks) for h in range(H)])
    bcol = per_head(lambda rs, h: bc_ref[rs, h:h + 1])

    decay = jnp.exp(jnp.where(incl[None], Gc - Gr, -jnp.inf))
    A = jnp.where(strict[None], bcol * _bdot_nt(kb, kb) * decay, 0.0)
    M = -A
    T = eye[None] + M
    for _ in range(5):
        Mb = M.astype(BF16)
        M = _bdot(Mb, Mb)
        T = T + _bdot(T.astype(BF16), M.astype(BF16))
    Tb = T.astype(BF16)
    eg = jnp.exp(Gc)
    kf = kb.astype(F32)
    w = _bdot(Tb, (kf * (bcol * eg)).astype(BF16)).astype(BF16)
    u = _bdot(Tb, (vb.astype(F32) * bcol).astype(BF16))
    attn = (_bdot_nt(qb, kb) * decay).astype(BF16)
    q_dec = (qb.astype(F32) * eg).astype(BF16)
    k_dec = (kf * jnp.exp(Gl - Gc)).astype(BF16)
    g_last = jnp.exp(Gl)

    nw = nw_ref[...]
    S = s_ref[...]
    for c in range(n_chunks):
        rs = slice(c * C, (c + 1) * C)
        bs = slice(c * H, (c + 1) * H)
        Sb = S.astype(BF16)
        v_new = (u[bs] - _bdot(w[bs], Sb)).astype(BF16)
        o = _bdot(q_dec[bs], Sb) + _bdot(attn[bs], v_new)
        S = g_last[bs] * S + jnp.stack([
            lax.dot_general(k_dec[c * H + h], v_new[h], (((0,), (0,)), ((), ())), preferred_element_type=F32)
            for h in range(H)])
        on = o * lax.rsqrt(jnp.mean(o * o, axis=-1, keepdims=True) + EPS) * nw
        for h in range(H):
            hs = slice(h * DN_DV, (h + 1) * DN_DV)
            zz = z_ref[rs, hs].astype(F32)
            y_ref[rs, hs] = (on[h] * (zz * _sigmoid(zz))).astype(BF16)
    s_ref[...] = S


def _gdn_chunk_call(q, k, v, proj, g, g_t, beta, norm_w):
    L, nh = q.shape
    R = GDN_STEP
    zc = COL_Z // nh
    return pl.pallas_call(
        _gdn_chunk_kernel,
        grid=(L // R,),
        in_specs=[
            pl.BlockSpec((R, nh), lambda i: (i, 0)),
            pl.BlockSpec((R, nh), lambda i: (i, 0)),
            pl.BlockSpec((R, nh), lambda i: (i, 0)),
            pl.BlockSpec((R, nh), lambda i: (i, zc)),
            pl.BlockSpec((R, DN_HEADS), lambda i: (i, 0)),
            pl.BlockSpec((DN_HEADS, R), lambda i: (0, i)),
            pl.BlockSpec((R, DN_HEADS), lambda i: (i, 0)),
            pl.BlockSpec((1, DN_DV), lambda i: (0, 0)),
        ],
        out_specs=pl.BlockSpec((R, nh), lambda i: (i, 0)),
        out_shape=jax.ShapeDtypeStruct((L, nh), BF16),
        scratch_shapes=[pltpu.VMEM((DN_HEADS, DN_DK, DN_DV), F32)],
        compiler_params=_params(("arbitrary",)),
        name="gdn_chunk",
    )(q, k, v, proj, g, g_t, beta, norm_w)


def _rope_tables(pos, period, lane):
    rot = period // ROPE_FRACTION
    half = rot // 2
    lp = lane % period
    expo = -((lp % half).astype(F32) * 2.0 / rot)
    inv_freq = jnp.power(jnp.float32(ROPE_THETA), expo)
    ang = pos * inv_freq
    cos, sin = jnp.cos(ang), jnp.sin(ang)
    c = jnp.where(lp < rot, cos, 1.0)
    s_lo = jnp.where(lp < half, -sin, 0.0)
    s_hi = jnp.where((lp >= half) & (lp < rot), sin, 0.0)
    return c, s_lo, s_hi, half


def _rope(x, tab):
    c, s_lo, s_hi, half = tab
    return x * c + pltpu.roll(x, LANES - half, 1) * s_lo + pltpu.roll(x, half, 1) * s_hi


def _dsa_prep_kernel(q_ref, k_ref, v_ref, iq_ref, sm_ref, pos_ref, kn_ref,
                     qo_ref, ko_ref, vto_ref, iqo_ref, iko_ref, wto_ref, qno_ref, kno_ref):
    lane = lax.broadcasted_iota(I32, (1, LANES), 1)
    pos = pos_ref[...]
    tab_att = _rope_tables(pos, ATT_DIM, lane)
    tab_idx = _rope_tables(pos, IDX_DIM, lane)
    scale = ATT_DIM ** -0.5 * LOG2_E
    q_norms = jnp.zeros((q_ref.shape[0], LANES), F32)
    k_norms = jnp.zeros((q_ref.shape[0], LANES), F32)
    for h in range(ATT_HEADS):
        hs = slice(h * ATT_DIM, (h + 1) * ATT_DIM)
        qh = (_rope(q_ref[:, hs].astype(F32), tab_att) * scale).astype(BF16)
        kh = _rope(k_ref[:, hs].astype(F32), tab_att).astype(BF16)
        qo_ref[:, hs] = qh
        ko_ref[:, hs] = kh
        vto_ref[hs, :] = v_ref[:, hs].astype(F32).T.astype(BF16)
        norm = lambda t: jnp.sqrt(jnp.sum(t.astype(F32) ** 2, axis=-1, keepdims=True))
        q_norms = jnp.where(lane == h, norm(qh), q_norms)
        k_norms = jnp.where(lane == h, norm(kh), k_norms)
    qno_ref[...] = q_norms.T[0:ATT_HEADS, :]
    kno_ref[...] = k_norms.T[0:ATT_HEADS, :]
    low = lane < IDX_DIM
    for p in range(IDX_HEADS // 2):
        x = _rope(iq_ref[:, p * LANES:(p + 1) * LANES].astype(F32), tab_idx)
        iqo_ref[:, (2 * p) * LANES:(2 * p + 1) * LANES] = jnp.where(low, x, 0.0).astype(BF16)
        iqo_ref[:, (2 * p + 1) * LANES:(2 * p + 2) * LANES] = jnp.where(low, pltpu.roll(x, IDX_DIM, 1), 0.0).astype(BF16)
    sm = sm_ref[...]
    kx = jnp.where(low, 0.0, sm)
    kx = kx * lax.rsqrt(jnp.sum(kx * kx, axis=-1, keepdims=True) * (1.0 / IDX_DIM) + EPS) * kn_ref[...]
    kx = _rope(kx, tab_idx)
    iko_ref[...] = jnp.where(low, pltpu.roll(kx, IDX_DIM, 1), 0.0).astype(BF16)
    wto_ref[...] = (sm * (IDX_HEADS ** -0.5 * IDX_DIM ** -0.5)).T[SM_IXW:SM_IXW + IDX_HEADS, :]


def _dsa_prep_call(proj, small, pos_col, kn_lanes):
    L = proj.shape[0]
    tT = min(256, L)
    na = ATT_HEADS * ATT_DIM
    ni = IDX_HEADS * IDX_DIM
    return pl.pallas_call(
        _dsa_prep_kernel,
        grid=(L // tT,),
        in_specs=[
            pl.BlockSpec((tT, na), lambda i: (i, COL_ATQ // na)),
            pl.BlockSpec((tT, na), lambda i: (i, COL_ATK // na)),
            pl.BlockSpec((tT, na), lambda i: (i, COL_ATV // na)),
            pl.BlockSpec((tT, ni), lambda i: (i, COL_IXQ // ni)),
            pl.BlockSpec((tT, LANES), lambda i: (i, 0)),
            pl.BlockSpec((tT, 1), lambda i: (i, 0)),
            pl.BlockSpec((1, LANES), lambda i: (0, 0)),
        ],
        out_specs=[
            pl.BlockSpec((tT, na), lambda i: (i, 0)),
            pl.BlockSpec((tT, na), lambda i: (i, 0)),
            pl.BlockSpec((na, tT), lambda i: (0, i)),
            pl.BlockSpec((tT, IDX_HEADS * LANES), lambda i: (i, 0)),
            pl.BlockSpec((tT, LANES), lambda i: (i, 0)),
            pl.BlockSpec((IDX_HEADS, tT), lambda i: (0, i)),
            pl.BlockSpec((ATT_HEADS, tT), lambda i: (0, i)),
            pl.BlockSpec((ATT_HEADS, tT), lambda i: (0, i)),
        ],
        out_shape=[jax.ShapeDtypeStruct((L, na), BF16), jax.ShapeDtypeStruct((L, na), BF16),
                   jax.ShapeDtypeStruct((na, L), BF16),
                   jax.ShapeDtypeStruct((L, IDX_HEADS * LANES), BF16), jax.ShapeDtypeStruct((L, LANES), BF16),
                   jax.ShapeDtypeStruct((IDX_HEADS, L), F32), jax.ShapeDtypeStruct((ATT_HEADS, L), F32),
                   jax.ShapeDtypeStruct((ATT_HEADS, L), F32)],
        compiler_params=_params(("parallel",)),
        name="dsa_prep",
    )(proj, proj, proj, proj, small, pos_col, kn_lanes)


def _index_scores_t(ik_blk, iq_heads, w_rows):
    acc = None
    for qh, wh in zip(iq_heads, w_rows):
        term = wh * jnp.maximum(_dot_nt(ik_blk, qh), 0.0)
        acc = term if acc is None else acc + term
    return acc


def _sortable(bits):
    return jnp.where(bits < 0, bits ^ jnp.int32(0x7FFFFFFF), bits)


def _score_keys(s):
    key = _sortable(pltpu.bitcast(s, I32))
    return jnp.where(key == -1, 0, key)


DSA_TQ = 256
DSA_TK = 512
SUM_ROWS = 16
KEY_NEG_INF = int(np.int32(np.array(-np.inf, np.float32).view(np.int32)) ^ np.int32(0x7FFFFFFF))
INT_MIN = -(2 ** 31)
INT_MAX = 2 ** 31 - 1


def _dsa_kernel(q_ref, iq_ref, wt_ref, qn_ref, ik_ref, kn_ref, k_hbm, vt_hbm, o_ref,
                sc_ref, hi_ref, kbuf, vbuf, sem, tie_ref, bias_ref, *acc_refs, topk, tk):
    TQ = q_ref.shape[0]
    L = ik_ref.shape[0]
    t0 = pl.program_id(0) * TQ
    nkt = (t0 + TQ + tk - 1) // tk

    def kv_copies(kt, slot):
        off = pl.multiple_of(kt * tk, tk)
        return (pltpu.make_async_copy(k_hbm.at[pl.ds(off, tk)], kbuf.at[slot], sem.at[0, slot]),
                pltpu.make_async_copy(vt_hbm.at[:, pl.ds(off, tk)], vbuf.at[slot], sem.at[1, slot]))

    for cp in kv_copies(0, 0):
        cp.start()

    iq_heads = [iq_ref[:, h * LANES:(h + 1) * LANES] for h in range(IDX_HEADS)]
    w_rows = [wt_ref[h:h + 1, :] for h in range(IDX_HEADS)]
    qpos = t0 + lax.broadcasted_iota(I32, (tk, TQ), 1)
    krow = lax.broadcasted_iota(I32, (tk, TQ), 0)

    def fill(kt, carry, on_diagonal):
        off = pl.multiple_of(kt * tk, tk)
        s = _index_scores_t(ik_ref[pl.ds(off, tk), :], iq_heads, w_rows)
        if on_diagonal:
            s = jnp.where(krow + off <= qpos, s, -jnp.inf)
        keys = _score_keys(s)
        sc_ref[pl.ds(off, tk), :] = keys
        hi_ref[pl.ds(off, tk), :] = (keys >> 16).astype(I16)
        return carry

    n_below = (t0 + 1) // tk
    lax.fori_loop(0, n_below, functools.partial(fill, on_diagonal=False), 0)
    lax.fori_loop(n_below, nkt, functools.partial(fill, on_diagonal=True), 0)

    def count(pred):
        def body(kt, acc):
            off = pl.multiple_of(kt * tk, tk)
            m = pred(sc_ref[pl.ds(off, tk), :], krow + off).astype(I32)
            return acc + jnp.sum(m.reshape(tk // 32, 32, TQ), axis=0)
        acc = lax.fori_loop(0, nkt, body, jnp.zeros((32, TQ), I32))
        return jnp.sum(acc, axis=0, keepdims=True)

    def count_hi(cand):
        def body(kt, acc):
            off = pl.multiple_of(kt * tk, tk)
            m = (hi_ref[pl.ds(off, tk), :] >= cand).astype(I16)
            for g in range(tk // 32):
                acc = acc + m[g * 32:(g + 1) * 32]
            return acc
        acc = lax.fori_loop(0, nkt, body, jnp.zeros((32, TQ), I16))
        return jnp.sum(acc.astype(I32), axis=0, keepdims=True)

    def bit_cond(st):
        b, _, cnt = st
        return (b >= 0) & (jnp.max(jnp.abs(cnt - topk)) > 0)

    def hi_body(i, st):
        u, cnt = st
        uc = u | (jnp.int32(2 ** 15) >> i)
        c = count_hi((uc - 2 ** 15).astype(I16))
        ok = c >= topk
        return jnp.where(ok, uc, u), jnp.where(ok, c, cnt)

    cnt_all = jnp.zeros((1, TQ), I32) + nkt * tk
    u, cnt = lax.fori_loop(0, 16, hi_body, (jnp.zeros((1, TQ), I32), cnt_all))
    base = (u - 2 ** 15) << 16

    def lo_body(st):
        b, delta, cnt = st
        dc = delta | (jnp.int32(1) << b)
        cand = base + dc
        c = count(lambda keys, _: keys >= cand)
        ok = c >= topk
        return b - 1, jnp.where(ok, dc, delta), jnp.where(ok, c, cnt)

    _, delta, cnt = lax.while_loop(bit_cond, lo_body, (jnp.int32(15), jnp.zeros((1, TQ), I32), cnt))
    theta = base + delta

    none_valid = theta <= KEY_NEG_INF
    tied = (cnt > topk) & jnp.logical_not(none_valid)
    any_tied = jnp.max(tied.astype(I32)) > 0
    tie_ref[0:1, :] = jnp.where(none_valid, 0, INT_MAX)
    tie_ref[1:2, :] = jnp.zeros((1, TQ), I32)

    @pl.when(any_tied)
    def _():
        above = count(lambda keys, _: keys > theta)
        tie_ref[0:1, :] = jnp.where(tied, topk - above, tie_ref[0:1, :])

    need = tie_ref[0:1, :]
    theta_keep = jnp.where(need > 0, theta, theta + 1)

    ones_rows = jnp.ones((SUM_ROWS, tk), BF16)
    heads = [slice(h * ATT_DIM, (h + 1) * ATT_DIM) for h in range(ATT_HEADS)]

    def sweep(step, init):
        for acc_ref in acc_refs:
            acc_ref[...] = jnp.zeros_like(acc_ref)
        tie_ref[1:2, :] = jnp.zeros((1, TQ), I32)

        def body(kt, carry):
            slot = kt % 2
            off = pl.multiple_of(kt * tk, tk)
            for cp in kv_copies(kt, slot):
                cp.wait()

            @pl.when(kt + 1 < nkt)
            def _():
                for cp in kv_copies(kt + 1, 1 - slot):
                    cp.start()

            keys = sc_ref[pl.ds(off, tk), :]

            @pl.when(jnp.logical_not(any_tied))
            def _():
                bias_ref[...] = jnp.where(keys >= theta_keep, 0.0, NEG_BIG)

            @pl.when(any_tied)
            def _():
                tie = keys == theta
                r = lax.broadcasted_iota(I32, (tk, tk), 0)
                c = lax.broadcasted_iota(I32, (tk, tk), 1)
                earlier = _dot((r > c).astype(F32).astype(BF16), jnp.where(tie, 1.0, 0.0).astype(BF16))
                rank = earlier + tie_ref[1:2, :].astype(F32)
                keep = (keys > theta) | (tie & (rank < need.astype(F32)))
                bias_ref[...] = jnp.where(keep, 0.0, NEG_BIG)
                tie_ref[1:2, :] += jnp.sum(tie.astype(I32), axis=0, keepdims=True)

            return step(slot, carry)

        return lax.fori_loop(0, nkt, body, init)

    def v_ext(slot, h):
        return jnp.concatenate([vbuf[slot, heads[h], :], ones_rows], axis=0)

    shift = qn_ref[...] * jnp.max(kn_ref[...], axis=1, keepdims=True)

    def fixed_shift_step(slot, carry):
        probs = [jnp.exp2(_dot_nt(kbuf[slot, :, hs], q_ref[:, hs]) + bias_ref[...] - shift[h:h + 1, :]).astype(BF16)
                 for h, hs in enumerate(heads)]
        for h, acc_ref in enumerate(acc_refs):
            acc_ref[...] += _dot(v_ext(slot, h), probs[h])
        return carry

    sweep(fixed_shift_step, 0)
    norm_min = functools.reduce(jnp.minimum, [acc_ref[ATT_DIM:ATT_DIM + 1, :] for acc_ref in acc_refs])

    @pl.when(jnp.min(norm_min) < 2.0 ** -80)
    def _():
        def running_max_step(slot, m_run):
            logits, tile_max = [], []
            for hs in heads:
                s = _dot_nt(kbuf[slot, :, hs], q_ref[:, hs]) + bias_ref[...]
                logits.append(s)
                tile_max.append(jnp.max(s, axis=0, keepdims=True))
            m_rows = []
            for h, acc_ref in enumerate(acc_refs):
                m_old = m_run[h:h + 1, :]
                m_new = jnp.maximum(m_old, tile_max[h])
                p = jnp.exp2(logits[h] - m_new).astype(BF16)
                acc_ref[...] = jnp.exp2(m_old - m_new) * acc_ref[...] + _dot(v_ext(slot, h), p)
                m_rows.append(m_new)
            return jnp.concatenate(m_rows, axis=0)

        for cp in kv_copies(0, 0):
            cp.start()
        sweep(running_max_step, jnp.full((ATT_HEADS, TQ), NEG_BIG, F32))

    for h, acc_ref in enumerate(acc_refs):
        out_t = acc_ref[0:ATT_DIM, :] / acc_ref[ATT_DIM:ATT_DIM + 1, :]
        o_ref[:, h * ATT_DIM:(h + 1) * ATT_DIM] = out_t.T.astype(BF16)


def _dsa_call(q, k, vt, iq, ik, wt, qn, kn, topk):
    L, na = q.shape
    TQ = min(DSA_TQ, L)
    tk = min(DSA_TK, L)
    assert topk <= tk
    return pl.pallas_call(
        functools.partial(_dsa_kernel, topk=topk, tk=tk),
        grid=(L // TQ,),
        in_specs=[
            pl.BlockSpec((TQ, na), lambda i: (i, 0)),
            pl.BlockSpec((TQ, IDX_HEADS * LANES), lambda i: (i, 0)),
            pl.BlockSpec((IDX_HEADS, TQ), lambda i: (0, i)),
            pl.BlockSpec((ATT_HEADS, TQ), lambda i: (0, i)),
            pl.BlockSpec((L, LANES), lambda i: (0, 0)),
            pl.BlockSpec((ATT_HEADS, L), lambda i: (0, 0)),
            pl.BlockSpec(memory_space=pl.ANY),
            pl.BlockSpec(memory_space=pl.ANY),
        ],
        out_specs=pl.BlockSpec((TQ, na), lambda i: (i, 0)),
        out_shape=jax.ShapeDtypeStruct((L, na), BF16),
        scratch_shapes=[
            pltpu.VMEM((L, TQ), I32),
            pltpu.VMEM((L, TQ), I16),
            pltpu.VMEM((2, tk, na), BF16),
            pltpu.VMEM((2, na, tk), BF16),
            pltpu.SemaphoreType.DMA((2, 2)),
            pltpu.VMEM((8, TQ), I32),
            pltpu.VMEM((tk, TQ), F32),
        ] + [pltpu.VMEM((ATT_DIM + SUM_ROWS, TQ), F32)] * ATT_HEADS,
        compiler_params=_params(("parallel",), vmem_mb=56),
        name="dsa",
    )(q, iq, wt, qn, ik, kn, k, vt)


def _merge_kernel(g0_ref, g1_ref, b0_ref, b1_ref, ydn_ref, yat_ref, wdn_ref, wat_ref, o_ref):
    gate0 = _sigmoid(g0_ref[...].astype(F32) + b0_ref[...])
    gate1 = _sigmoid(g1_ref[...].astype(F32) + b1_ref[...])
    merged = gate0 * _dot(ydn_ref[...], wdn_ref[...]) + gate1 * _dot(yat_ref[...], wat_ref[...])
    o_ref[...] = merged.astype(o_ref.dtype)


def _merge_call(proj, b_gates, y_dn, y_at, w_dn, w_at):
    L = proj.shape[0]
    tm = min(1024, L)
    tn = 512
    nj = D_MODEL // tn
    c0 = COL_GATE // tn
    kd = y_dn.shape[1]
    return pl.pallas_call(
        _merge_kernel,
        grid=(L // tm, nj),
        in_specs=[
            pl.BlockSpec((tm, tn), lambda i, j: (i, c0 + j)),
            pl.BlockSpec((tm, tn), lambda i, j: (i, c0 + nj + j)),
            pl.BlockSpec((1, tn), lambda i, j: (0, j)),
            pl.BlockSpec((1, tn), lambda i, j: (0, nj + j)),
            pl.BlockSpec((tm, kd), lambda i, j: (i, 0)),
            pl.BlockSpec((tm, kd), lambda i, j: (i, 0)),
            pl.BlockSpec((kd, tn), lambda i, j: (0, j)),
            pl.BlockSpec((kd, tn), lambda i, j: (0, j)),
        ],
        out_specs=pl.BlockSpec((tm, tn), lambda i, j: (i, j)),
        out_shape=jax.ShapeDtypeStruct((L, D_MODEL), BF16),
        compiler_params=_params(("parallel", "arbitrary")),
        name="merge",
    )(proj, proj, b_gates, b_gates, y_dn, y_at, w_dn, w_at)


RT_E0, RT_E1, RT_W0, RT_W1 = 0, 1, 2, 3


def _first_lane_of_max(v, lane):
    m = jnp.max(v, axis=-1, keepdims=True)
    return m, jnp.min(jnp.where(v == m, lane, LANES), axis=-1, keepdims=True)


def _outproj_kernel(x_ref, mg_ref, wo_ref, nf_ref, wr_ref, br_ref, x1_ref, h2_ref, rt_ref):
    x1 = x_ref[...] + _dot(mg_ref[...], wo_ref[...])
    x1_ref[...] = x1
    h2 = x1 * lax.rsqrt(jnp.mean(x1 * x1, axis=-1, keepdims=True) + EPS) * nf_ref[...]
    h2_ref[...] = h2
    lg = _dot(h2.astype(BF16), wr_ref[...]) + br_ref[...]
    lane = lax.broadcasted_iota(I32, (1, LANES), 1)
    ninf = -jnp.inf
    gl = jnp.where(lane < N_GROUPS, lg, ninf)
    gmax, g_sel = _first_lane_of_max(gl, lane)
    p_group = 1.0 / jnp.sum(jnp.exp(gl - gmax), axis=-1, keepdims=True)
    ex = lane - N_GROUPS
    in_group = (ex >= 0) & (ex < N_EXPERTS) & ((ex // EXPERTS_PER_GROUP) == g_sel)
    el = jnp.where(in_group, lg, ninf)
    m1, i1 = _first_lane_of_max(el, lane)
    m2, i2 = _first_lane_of_max(jnp.where(lane == i1, ninf, el), lane)
    e2 = jnp.exp(m2 - m1)
    w0 = p_group / (1.0 + e2)
    w1 = p_group * e2 / (1.0 + e2)
    rec = jnp.where(lane == RT_E0, (i1 - N_GROUPS).astype(F32), 0.0)
    rec = jnp.where(lane == RT_E1, (i2 - N_GROUPS).astype(F32), rec)
    rec = jnp.where(lane == RT_W0, w0, rec)
    rt_ref[...] = jnp.where(lane == RT_W1, w1, rec)


def _outproj_call(x2, merged, w_out, norm_ffn, w_route, b_route):
    L = x2.shape[0]
    tm = min(256, L)
    row = lambda i: (i, 0)
    fixed = lambda i: (0, 0)
    return pl.pallas_call(
        _outproj_kernel,
        grid=(L // tm,),
        in_specs=[
            pl.BlockSpec((tm, D_MODEL), row),
            pl.BlockSpec((tm, D_MODEL), row),
            pl.BlockSpec((D_MODEL, D_MODEL), fixed),
            pl.BlockSpec((1, D_MODEL), fixed),
            pl.BlockSpec((D_MODEL, LANES), fixed),
            pl.BlockSpec((1, LANES), fixed),
        ],
        out_specs=[pl.BlockSpec((tm, D_MODEL), row), pl.BlockSpec((tm, D_MODEL), row), pl.BlockSpec((tm, LANES), row)],
        out_shape=[jax.ShapeDtypeStruct((L, D_MODEL), F32), jax.ShapeDtypeStruct((L, D_MODEL), F32),
                   jax.ShapeDtypeStruct((L, LANES), F32)],
        compiler_params=_params(("parallel",)),
        name="outproj_route",
    )(x2, merged, w_out, norm_ffn, w_route, b_route)


def _moe_slots(L):
    return -(-(2 * L + N_EXPERTS * (MOE_ROWS - 1)) // MOE_ROWS) * MOE_ROWS


def _moe_plan_kernel(rt_ref, dest_ref, blk_ref, cnt_ref, carry_ref, start_ref):
    phase, i = pl.program_id(0), pl.program_id(1)
    tR = rt_ref.shape[0]
    lane = lax.broadcasted_iota(I32, (1, LANES), 1)
    lane_f = lane.astype(F32)
    rt = rt_ref[...]
    e0, e1 = rt[:, RT_E0:RT_E0 + 1], rt[:, RT_E1:RT_E1 + 1]
    hit0, hit1 = lane_f == e0, lane_f == e1
    onehot = (hit0 | hit1).astype(F32)
    colsum = jnp.sum(onehot, axis=0, keepdims=True)

    @pl.when((phase == 0) & (i == 0))
    def _():
        cnt_ref[...] = jnp.zeros_like(cnt_ref)

    @pl.when(phase == 0)
    def _():
        cnt_ref[...] += colsum

    @pl.when((phase == 1) & (i == 0))
    def _():
        carry_ref[...] = jnp.zeros_like(carry_ref)
        padded = jnp.floor((cnt_ref[...] + (MOE_ROWS - 1)) * (1.0 / MOE_ROWS)) * MOE_ROWS
        r = lax.broadcasted_iota(I32, (LANES, LANES), 0)
        c = lax.broadcasted_iota(I32, (LANES, LANES), 1)
        upper = (r <= c).astype(F32)
        end = jnp.dot(jnp.broadcast_to(padded, (8, LANES)), upper, preferred_element_type=F32,
                      precision=lax.Precision.HIGHEST)[0:1, :]
        start_ref[...] = end - padded
        n_used = end[:, N_EXPERTS - 1:N_EXPERTS] * (1.0 / MOE_ROWS)
        nb = blk_ref.shape[0] - 8
        b = lax.broadcasted_iota(I32, (nb, 1), 0).astype(F32)
        b_eff = jnp.minimum(b, n_used - 1.0)
        done = ((end <= b_eff * MOE_ROWS) & (lane < N_EXPERTS)).astype(F32)
        blk_e = jnp.minimum(jnp.sum(done, axis=-1, keepdims=True), N_EXPERTS - 1.0)
        blk_ref[0:nb, :] = jnp.where(lane == 0, blk_e, jnp.where(lane == 1, n_used, 0.0)).astype(I32)
        row = lax.broadcasted_iota(I32, (8, LANES), 0)
        seg = jnp.where(row == 0, end - padded, jnp.where(row == 1, cnt_ref[...], jnp.where(row == 2, padded, 0.0)))
        blk_ref[nb:nb + 8, :] = seg.astype(I32)

    @pl.when(phase == 1)
    def _():
        r = lax.broadcasted_iota(I32, (tR, tR), 0)
        c = lax.broadcasted_iota(I32, (tR, tR), 1)
        before = _dot((r > c).astype(BF16), onehot.astype(BF16)) + carry_ref[...]
        slot = before + start_ref[...]
        d0 = jnp.sum(jnp.where(hit0, slot, 0.0), axis=-1, keepdims=True)
        d1 = jnp.sum(jnp.where(hit1, slot, 0.0), axis=-1, keepdims=True)
        dest_ref[...] = jnp.where(lane == 0, d0, jnp.where(lane == 1, d1, 0.0)).astype(I32)
        carry_ref[...] += colsum


def _moe_plan_call(route):
    L = route.shape[0]
    tR = min(256, L)
    nb = _moe_slots(L) // MOE_ROWS
    nb_pad = -(-nb // 8) * 8
    return pl.pallas_call(
        _moe_plan_kernel,
        grid=(2, L // tR),
        in_specs=[pl.BlockSpec((tR, LANES), lambda p, i: (i, 0))],
        out_specs=[pl.BlockSpec((tR, LANES), lambda p, i: (p * i, 0)),
                   pl.BlockSpec((nb_pad + 8, LANES), lambda p, i: (0, 0))],
        out_shape=[jax.ShapeDtypeStruct((L, LANES), I32), jax.ShapeDtypeStruct((nb_pad + 8, LANES), I32)],
        scratch_shapes=[pltpu.VMEM((1, LANES), F32)] * 3,
        compiler_params=_params(("arbitrary", "arbitrary")),
        name="moe_plan",
    )(route)


PLAN_START, PLAN_COUNT, PLAN_PADDED, PLAN_USED = 0, N_EXPERTS, 2 * N_EXPERTS, 3 * N_EXPERTS
PAD_PIECES = tuple(MOE_ROWS >> s for s in range(1, MOE_ROWS.bit_length() - 3))


def _moe_scatter_kernel(dest_ref, plan_ref, h_ref, xs_out, zero_ref, sem, zero_sem, *, rows, nb):
    base = pl.program_id(0) * rows

    def row_copy(r, j):
        return pltpu.make_async_copy(h_ref.at[pl.ds(r, 1)], xs_out.at[pl.ds(dest_ref[2 * (base + r) + j], 1)], sem)

    def start(r, c):
        row_copy(r, 0).start(priority=0)
        row_copy(r, 1).start(priority=1)
        return c

    def wait(r, c):
        row_copy(r, 0).wait()
        row_copy(r, 1).wait()
        return c

    lax.fori_loop(0, rows, start, 0)

    @pl.when(pl.program_id(0) == 0)
    def _():
        zero_ref[...] = jnp.zeros_like(zero_ref)

        def for_each_pad_piece(act):
            def per_expert(e, c):
                count = plan_ref[PLAN_COUNT + e]
                pad = plan_ref[PLAN_PADDED + e] - count
                first = plan_ref[PLAN_START + e] + count
                end = first + pad

                def single_row(r, cc):
                    act(pltpu.make_async_copy(zero_ref.at[pl.ds(0, 1)], xs_out.at[pl.ds(first + r, 1)], zero_sem))
                    return cc

                lax.fori_loop(0, pad & 7, single_row, 0)
                for k in PAD_PIECES:
                    @pl.when((pad & k) != 0)
                    def _():
                        at = pl.multiple_of(end - (pad & ~(k - 1)), 8)
                        act(pltpu.make_async_copy(zero_ref.at[pl.ds(0, k)], xs_out.at[pl.ds(at, k)], zero_sem))
                return c
            lax.fori_loop(0, N_EXPERTS, per_expert, 0)

        def for_each_unused_block(act):
            def per_block(b, c):
                act(pltpu.make_async_copy(zero_ref, xs_out.at[pl.ds(b * MOE_ROWS, MOE_ROWS)], zero_sem))
                return c
            lax.fori_loop(plan_ref[PLAN_USED], nb, per_block, 0)

        for_each_pad_piece(lambda cp: cp.start())
        for_each_unused_block(lambda cp: cp.start())
        for_each_pad_piece(lambda cp: cp.wait())
        for_each_unused_block(lambda cp: cp.wait())

    lax.fori_loop(0, rows, wait, 0)


def _moe_scatter_call(dest_flat, plan, h2):
    L = h2.shape[0]
    rows = min(256, L)
    slots = _moe_slots(L)
    return pl.pallas_call(
        functools.partial(_moe_scatter_kernel, rows=rows, nb=slots // MOE_ROWS),
        grid_spec=pltpu.PrefetchScalarGridSpec(
            num_scalar_prefetch=2,
            grid=(L // rows,),
            in_specs=[pl.BlockSpec((rows, D_MODEL), lambda i, d, p: (i, 0))],
            out_specs=pl.BlockSpec(memory_space=pl.ANY),
            scratch_shapes=[pltpu.VMEM((MOE_ROWS, D_MODEL), F32), pltpu.SemaphoreType.DMA(()),
                            pltpu.SemaphoreType.DMA(())],
        ),
        out_shape=jax.ShapeDtypeStruct((slots, D_MODEL), F32),
        compiler_params=_params(("arbitrary",)),
        name="moe_scatter",
    )(dest_flat, plan, h2)


def _moe_ffn_kernel(be_ref, x_ref, wg_ref, wu_ref, wd_ref, y_ref, wgb_ref, wub_ref, wdb_ref, *, nb):
    b = pl.program_id(0)
    used = b < be_ref[nb]
    new_expert = (b == 0) | (be_ref[b] != be_ref[jnp.maximum(b - 1, 0)])

    @pl.when(used & new_expert)
    def _():
        wgb_ref[...] = wg_ref[...].astype(BF16)
        wub_ref[...] = wu_ref[...].astype(BF16)
        wdb_ref[...] = wd_ref[...].astype(BF16)

    @pl.when(used)
    def _():
        xb = x_ref[...].astype(BF16)
        gate = _dot(xb, wgb_ref[...])
        up = _dot(xb, wub_ref[...])
        hidden = (gate * _sigmoid(gate) * up).astype(BF16)
        y_ref[...] = _dot(hidden, wdb_ref[...])

    @pl.when(b >= be_ref[nb])
    def _():
        y_ref[...] = jnp.zeros_like(y_ref)


def _moe_ffn_call(blk_e, xs, w_gate, w_up, w_down):
    P = xs.shape[0]
    nb = P // MOE_ROWS
    return pl.pallas_call(
        functools.partial(_moe_ffn_kernel, nb=nb),
        grid_spec=pltpu.PrefetchScalarGridSpec(
            num_scalar_prefetch=1,
            grid=(nb,),
            in_specs=[
                pl.BlockSpec((MOE_ROWS, D_MODEL), lambda b, be: (b, 0)),
                pl.BlockSpec((None, D_MODEL, EXPERT_HIDDEN), lambda b, be: (be[b], 0, 0)),
                pl.BlockSpec((None, D_MODEL, EXPERT_HIDDEN), lambda b, be: (be[b], 0, 0)),
                pl.BlockSpec((None, EXPERT_HIDDEN, D_MODEL), lambda b, be: (be[b], 0, 0)),
            ],
            out_specs=pl.BlockSpec((MOE_ROWS, D_MODEL), lambda b, be: (b, 0)),
            scratch_shapes=[pltpu.VMEM((D_MODEL, EXPERT_HIDDEN), BF16), pltpu.VMEM((D_MODEL, EXPERT_HIDDEN), BF16),
                            pltpu.VMEM((EXPERT_HIDDEN, D_MODEL), BF16)],
        ),
        out_shape=jax.ShapeDtypeStruct((P, D_MODEL), F32),
        compiler_params=_params(("arbitrary",), vmem_mb=56),
        name="moe_ffn",
    )(blk_e, xs, w_gate, w_up, w_down)


def _moe_combine_kernel(dest_ref, x1_ref, rt_ref, nf_ref, ys_hbm, o_ref, ya_ref, yb_ref, sem, *, rows):
    step, n_steps = pl.program_id(0), pl.num_programs(0)

    def gather(s, act):
        slot = s % 2

        def body(r, c):
            for j, dst in enumerate((ya_ref, yb_ref)):
                src = ys_hbm.at[pl.ds(dest_ref[2 * (s * rows + r) + j], 1)]
                act(pltpu.make_async_copy(src, dst.at[slot, pl.ds(r, 1)], sem.at[slot]), j)
            return c

        lax.fori_loop(0, rows, body, 0)

    start = lambda cp, j: cp.start(priority=j)
    wait = lambda cp, j: cp.wait()

    @pl.when(step == 0)
    def _():
        gather(step, start)

    @pl.when(step + 1 < n_steps)
    def _():
        gather(step + 1, start)

    gather(step, wait)
    slot = step % 2
    rt = rt_ref[...]
    x = x1_ref[...] + rt[:, RT_W0:RT_W0 + 1] * ya_ref[slot] + rt[:, RT_W1:RT_W1 + 1] * yb_ref[slot]
    o_ref[...] = x * lax.rsqrt(jnp.mean(x * x, axis=-1, keepdims=True) + EPS) * nf_ref[...]


def _moe_combine_call(dest_flat, x1, route, norm_final, ys):
    L = x1.shape[0]
    rows = min(256, L)
    return pl.pallas_call(
        functools.partial(_moe_combine_kernel, rows=rows),
        grid_spec=pltpu.PrefetchScalarGridSpec(
            num_scalar_prefetch=1,
            grid=(L // rows,),
            in_specs=[
                pl.BlockSpec((rows, D_MODEL), lambda i, d: (i, 0)),
                pl.BlockSpec((rows, LANES), lambda i, d: (i, 0)),
                pl.BlockSpec((1, D_MODEL), lambda i, d: (0, 0)),
                pl.BlockSpec(memory_space=pl.ANY),
            ],
            out_specs=pl.BlockSpec((rows, D_MODEL), lambda i, d: (i, 0)),
            scratch_shapes=[pltpu.VMEM((2, rows, D_MODEL), F32), pltpu.VMEM((2, rows, D_MODEL), F32),
                            pltpu.SemaphoreType.DMA((2,))],
        ),
        out_shape=jax.ShapeDtypeStruct((L, D_MODEL), F32),
        compiler_params=_params(("arbitrary",)),
        name="moe_combine",
    )(dest_flat, x1, route, norm_final, ys)


def _pack_w_in(w):
    s = np.cumsum((0, QKV_COLS, DN_HEADS * DN_DV, DN_HEADS, DN_HEADS, ATT_HEADS * ATT_DIM, ATT_HEADS * ATT_DIM,
                   ATT_HEADS * ATT_DIM, IDX_HEADS * IDX_DIM, IDX_DIM, IDX_HEADS, 2 * D_MODEL))
    seg = lambda n: w[:, int(s[n]):int(s[n + 1])]
    w_main = jnp.concatenate([seg(0), seg(1), seg(4), seg(5), seg(6), seg(7), seg(10)], axis=1).astype(BF16)
    pad = jnp.zeros((w.shape[0], SM_IXK - SM_IXW - IDX_HEADS), w.dtype)
    w_small = jnp.concatenate([seg(2), seg(3), seg(9), pad, seg(8)], axis=1).astype(BF16)
    return w_main, w_small


def _forward(x, positions, norm_mix, w_in, b_gates, dn_conv_w, dn_a_log, dn_dt_bias, dn_norm_w, idx_k_norm,
             w_proj_dn, w_proj_att, w_out, norm_ffn, w_group, b_group, w_router, b_router, w_exp_gate, w_exp_up,
             w_exp_down, norm_final):
    st = {}
    L = x.shape[1]
    x2 = x.reshape(L, D_MODEL)
    w_main, w_small = _pack_w_in(w_in[0])
    proj, small = _proj_call(x2, norm_mix[0].reshape(1, D_MODEL), w_main, w_small)
    st["proj"], st["small"] = proj, small
    q, k, v, g, beta = _gdn_prep_call(proj, small, dn_conv_w[0], dn_a_log[0].reshape(1, DN_HEADS),
                                      dn_dt_bias[0].reshape(1, DN_HEADS))
    st["y_dn"] = _gdn_chunk_call(q, k, v, proj, g, g.T, beta, dn_norm_w[0].reshape(1, DN_DV))

    pos_col = positions.reshape(L, 1).astype(F32)
    kn_lanes = jnp.concatenate([jnp.zeros((SM_IXK,), F32), idx_k_norm[0].astype(F32)]).reshape(1, LANES)
    aq, ak, avt, iq, ik, wt, qn, kn = _dsa_prep_call(proj, small, pos_col, kn_lanes)
    st["y_at"] = _dsa_call(aq, ak, avt, iq, ik, wt, qn, kn, min(TOPK_MAX, L // 4))

    merged = _merge_call(proj, b_gates[0].reshape(1, 2 * D_MODEL), st["y_dn"], st["y_at"],
                         w_proj_dn[0].astype(BF16), w_proj_att[0].astype(BF16))
    st["merged"] = merged
    n_route = N_GROUPS + N_EXPERTS
    w_route = jnp.concatenate([w_group[0], w_router[0], jnp.zeros((D_MODEL, LANES - n_route), F32)], axis=1).astype(BF16)
    b_route = jnp.concatenate([b_group[0], b_router[0], jnp.zeros((LANES - n_route,), F32)]).reshape(1, LANES)
    x1, h2, route = _outproj_call(x2, merged, w_out[0].astype(BF16), norm_ffn[0].reshape(1, D_MODEL), w_route, b_route)
    st["x1"], st["h2"], st["route"] = x1, h2, route

    dest, blk = _moe_plan_call(route)
    dest_flat = dest[:, :2].reshape(2 * L)
    slots = _moe_slots(L)
    nb = slots // MOE_ROWS
    blk_e = jnp.concatenate([blk[:nb, 0], blk[0:1, 1]])
    seg = blk[blk.shape[0] - 8:blk.shape[0] - 5, :N_EXPERTS]
    plan = jnp.concatenate([seg.reshape(3 * N_EXPERTS), blk[0:1, 1]])
    xs = _moe_scatter_call(dest_flat, plan, h2)
    ys = _moe_ffn_call(blk_e, xs, w_exp_gate[0], w_exp_up[0], w_exp_down[0])
    out = _moe_combine_call(dest_flat, x1, route, norm_final.reshape(1, D_MODEL), ys)
    st["out"] = out.reshape(1, L, D_MODEL)
    return st


def kernel(x, positions, norm_mix, w_in, b_gates, dn_conv_w, dn_a_log, dn_dt_bias, dn_norm_w, idx_k_norm, w_proj_dn,
           w_proj_att, w_out, norm_ffn, w_group, b_group, w_router, b_router, w_exp_gate, w_exp_up, w_exp_down,
           norm_final):
    return _forward(x, positions, norm_mix, w_in, b_gates, dn_conv_w, dn_a_log, dn_dt_bias, dn_norm_w, idx_k_norm,
                    w_proj_dn, w_proj_att, w_out, norm_ffn, w_group, b_group, w_router, b_router, w_exp_gate,
                    w_exp_up, w_exp_down, norm_final)["out"]


def _stages(d, upto=None):
    return _forward(*[d[n] for n in ("x", "positions", "norm_mix", "w_in", "b_gates", "dn_conv_w", "dn_a_log",
                                     "dn_dt_bias", "dn_norm_w", "idx_k_norm", "w_proj_dn", "w_proj_att", "w_out",
                                     "norm_ffn", "w_group", "b_group", "w_router", "b_router", "w_exp_gate",
                                     "w_exp_up", "w_exp_down", "norm_final")])
```

```python
import functools

import jax
import jax.numpy as jnp
import numpy as np
from jax import lax
from jax.experimental import pallas as pl
from jax.experimental.pallas import tpu as pltpu

D_MODEL = 2048
DN_HEADS = 8
DN_DK = 128
DN_DV = 128
DN_CONV = 4
DN_CHUNK = 64
ATT_HEADS = 8
ATT_DIM = 128
IDX_HEADS = 8
IDX_DIM = 64
TOPK_MAX = 256
ROPE_THETA = 500000.0
ROPE_FRACTION = 4
N_GROUPS = 8
EXPERTS_PER_GROUP = 8
N_EXPERTS = N_GROUPS * EXPERTS_PER_GROUP
EXPERT_HIDDEN = 512
EPS = 1e-6

LANES = 128
MOE_ROWS = 256
NEG_BIG = -1e30
LOG2_E = 1.4426950408889634

F32 = jnp.float32
BF16 = jnp.bfloat16
I32 = jnp.int32
I16 = jnp.int16

QKV_COLS = 2 * DN_HEADS * DN_DK + DN_HEADS * DN_DV
COL_QKV = 0
COL_Z = COL_QKV + QKV_COLS
COL_ATQ = COL_Z + DN_HEADS * DN_DV
COL_ATK = COL_ATQ + ATT_HEADS * ATT_DIM
COL_ATV = COL_ATK + ATT_HEADS * ATT_DIM
COL_IXQ = COL_ATV + ATT_HEADS * ATT_DIM
COL_GATE = COL_IXQ + IDX_HEADS * IDX_DIM
MAIN_COLS = COL_GATE + 2 * D_MODEL
SM_B = 0
SM_A = 8
SM_IXW = 16
SM_IXK = 64


def _params(sem, vmem_mb=48):
    return pltpu.CompilerParams(dimension_semantics=sem, vmem_limit_bytes=vmem_mb * 1024 * 1024)


def _sigmoid(x):
    return 1.0 / (1.0 + jnp.exp(-x))


def _dot(a, b):
    return jnp.dot(a, b, preferred_element_type=F32)


def _dot_nt(a, b):
    return lax.dot_general(a, b, (((1,), (1,)), ((), ())), preferred_element_type=F32)


def _proj_kernel(x_ref, g_ref, w_ref, ws_ref, o_ref, os_ref, h_ref):
    @pl.when(pl.program_id(1) == 0)
    def _():
        x = x_ref[...]
        h = x * lax.rsqrt(jnp.mean(x * x, axis=-1, keepdims=True) + EPS) * g_ref[...]
        h_ref[...] = h.astype(BF16)
        os_ref[...] = _dot(h_ref[...], ws_ref[...])

    o_ref[...] = _dot(h_ref[...], w_ref[...]).astype(o_ref.dtype)


def _proj_call(x2, gain, w_main, w_small):
    L, D = x2.shape
    N = w_main.shape[1]
    tm = min(512, L)
    tn = N // 4
    return pl.pallas_call(
        _proj_kernel,
        grid=(L // tm, N // tn),
        in_specs=[
            pl.BlockSpec((tm, D), lambda i, j: (i, 0)),
            pl.BlockSpec((1, D), lambda i, j: (0, 0)),
            pl.BlockSpec((D, tn), lambda i, j: (0, j)),
            pl.BlockSpec((D, LANES), lambda i, j: (0, 0)),
        ],
        out_specs=[
            pl.BlockSpec((tm, tn), lambda i, j: (i, j)),
            pl.BlockSpec((tm, LANES), lambda i, j: (i, 0)),
        ],
        out_shape=[jax.ShapeDtypeStruct((L, N), BF16), jax.ShapeDtypeStruct((L, LANES), F32)],
        scratch_shapes=[pltpu.VMEM((tm, D), BF16)],
        compiler_params=_params(("parallel", "arbitrary")),
        name="proj",
    )(x2, gain, w_main, w_small)


def _gdn_prep_kernel(qkv_ref, halo_ref, sm_ref, cw_ref, alog_ref, dtb_ref,
                     q_ref, k_ref, v_ref, g_ref, b_ref, xs_ref):
    tT = qkv_ref.shape[0]
    first = pl.program_id(0) == 0
    xs_ref[8:8 + tT, :] = qkv_ref[...].astype(F32)
    halo = halo_ref[8:16, :].astype(F32)
    xs_ref[0:8, :] = jnp.where(first, 0.0, halo)
    nh = DN_HEADS * DN_DK
    for c in range(QKV_COLS // LANES):
        sl = slice(c * LANES, (c + 1) * LANES)
        y = xs_ref[8:8 + tT, sl] * cw_ref[3:4, sl]
        for j in range(DN_CONV - 1):
            y = y + xs_ref[5 + j:5 + j + tT, sl] * cw_ref[j:j + 1, sl]
        y = y * _sigmoid(y)
        if c < 2 * DN_HEADS:
            y = y * lax.rsqrt(jnp.sum(y * y, axis=-1, keepdims=True) + EPS)
            if c < DN_HEADS:
                q_ref[:, sl] = (y * (DN_DK ** -0.5)).astype(BF16)
            else:
                k_ref[:, c * LANES - nh:(c + 1) * LANES - nh] = y.astype(BF16)
        else:
            v_ref[:, c * LANES - 2 * nh:(c + 1) * LANES - 2 * nh] = y.astype(BF16)
    sm = sm_ref[...]
    b_ref[...] = _sigmoid(sm[:, SM_B:SM_B + DN_HEADS])
    a = sm[:, SM_A:SM_A + DN_HEADS] + dtb_ref[...]
    softplus = jnp.maximum(a, 0.0) + jnp.log1p(jnp.exp(-jnp.abs(a)))
    g_ref[...] = -jnp.exp(alog_ref[...]) * softplus


def _gdn_prep_call(proj, small, conv_w, a_log, dt_bias):
    L = proj.shape[0]
    tT = min(256, L)
    hb = tT // 16
    nh = DN_HEADS * DN_DK
    return pl.pallas_call(
        _gdn_prep_kernel,
        grid=(L // tT,),
        in_specs=[
            pl.BlockSpec((tT, QKV_COLS), lambda i: (i, 0)),
            pl.BlockSpec((16, QKV_COLS), lambda i: (jnp.maximum(i * hb - 1, 0), 0)),
            pl.BlockSpec((tT, LANES), lambda i: (i, 0)),
            pl.BlockSpec((DN_CONV, QKV_COLS), lambda i: (0, 0)),
            pl.BlockSpec((1, DN_HEADS), lambda i: (0, 0)),
            pl.BlockSpec((1, DN_HEADS), lambda i: (0, 0)),
        ],
        out_specs=[
            pl.BlockSpec((tT, nh), lambda i: (i, 0)),
            pl.BlockSpec((tT, nh), lambda i: (i, 0)),
            pl.BlockSpec((tT, nh), lambda i: (i, 0)),
            pl.BlockSpec((tT, DN_HEADS), lambda i: (i, 0)),
            pl.BlockSpec((tT, DN_HEADS), lambda i: (i, 0)),
        ],
        out_shape=[jax.ShapeDtypeStruct((L, nh), BF16)] * 3 + [jax.ShapeDtypeStruct((L, DN_HEADS), F32)] * 2,
        scratch_shapes=[pltpu.VMEM((tT + 8, QKV_COLS), F32)],
        compiler_params=_params(("parallel",)),
        name="gdn_prep",
    )(proj, proj, small, conv_w, a_log, dt_bias)


GDN_STEP = 4 * DN_CHUNK


def _bdot(a, b):
    return lax.dot_general(a, b, (((2,), (1,)), ((0,), (0,))), preferred_element_type=F32)


def _bdot_nt(a, b):
    return lax.dot_general(a, b, (((2,), (2,)), ((0,), (0,))), preferred_element_type=F32)


def _gdn_chunk_kernel(q_ref, k_ref, v_ref, z_ref, gc_ref, gr_ref, bc_ref, nw_ref, y_ref, s_ref):
    C, H = DN_CHUNK, DN_HEADS
    n_chunks = q_ref.shape[0] // C

    @pl.when(pl.program_id(0) == 0)
    def _():
        s_ref[...] = jnp.zeros_like(s_ref)

    row = lax.broadcasted_iota(I32, (C, C), 0)
    col = lax.broadcasted_iota(I32, (C, C), 1)
    incl = row >= col
    strict = row > col
    tri = incl.astype(F32)
    tri_t = (row <= col).astype(F32)
    eye = (row == col).astype(F32)

    def per_head(fn):
        return jnp.stack([fn(slice(c * C, (c + 1) * C), h) for c in range(n_chunks) for h in range(H)])

    head = lambda ref: per_head(lambda rs, h: ref[rs, h * DN_DK:(h + 1) * DN_DK])
    qb, kb, vb = head(q_ref), head(k_ref), head(v_ref)
    G_col = [jnp.dot(tri, gc_ref[c * C:(c + 1) * C, :], preferred_element_type=F32, precision=lax.Precision.HIGHEST)
             for c in range(n_chunks)]
    G_row = [jnp.dot(gr_ref[:, c * C:(c + 1) * C], tri_t, preferred_element_type=F32, precision=lax.Precision.HIGHEST)
             for c in range(n_chunks)]
    Gc = jnp.stack([G_col[c][:, h:h + 1] for c in range(n_chunks) for h in range(H)])
    Gr = jnp.stack([G_row[c][h:h + 1, :] for c in range(n_chunks) for h in range(H)])
    Gl = jnp.stack([G_col[c][C - 1:C, h:h + 1] for c in range(n_chunks) for h in range(H)])
    bcol = per_head(lambda rs, h: bc_ref[rs, h:h + 1])

    decay = jnp.exp(jnp.where(incl[None], Gc - Gr, -jnp.inf))
    A = jnp.where(strict[None], bcol * _bdot_nt(kb, kb) * decay, 0.0)
    M = -A
    T = eye[None] + M
    for _ in range(5):
        Mb = M.astype(BF16)
        M = _bdot(Mb, Mb)
        T = T + _bdot(T.astype(BF16), M.astype(BF16))
    Tb = T.astype(BF16)
    eg = jnp.exp(Gc)
    kf = kb.astype(F32)
    w = _bdot(Tb, (kf * (bcol * eg)).astype(BF16)).astype(BF16)
    u = _bdot(Tb, (vb.astype(F32) * bcol).astype(BF16))
    attn = (_bdot_nt(qb, kb) * decay).astype(BF16)
    q_dec = (qb.astype(F32) * eg).astype(BF16)
    k_dec = (kf * jnp.exp(Gl - Gc)).astype(BF16)
    g_last = jnp.exp(Gl)

    nw = nw_ref[...]
    S = s_ref[...]
    for c in range(n_chunks):
        rs = slice(c * C, (c + 1) * C)
        bs = slice(c * H, (c + 1) * H)
        Sb = S.astype(BF16)
        v_new = (u[bs] - _bdot(w[bs], Sb)).astype(BF16)
        o = _bdot(q_dec[bs], Sb) + _bdot(attn[bs], v_new)
        S = g_last[bs] * S + jnp.stack([
            lax.dot_general(k_dec[c * H + h], v_new[h], (((0,), (0,)), ((), ())), preferred_element_type=F32)
            for h in range(H)])
        on = o * lax.rsqrt(jnp.mean(o * o, axis=-1, keepdims=True) + EPS) * nw
        for h in range(H):
            hs = slice(h * DN_DV, (h + 1) * DN_DV)
            zz = z_ref[rs, hs].astype(F32)
            y_ref[rs, hs] = (on[h] * (zz * _sigmoid(zz))).astype(BF16)
    s_ref[...] = S


def _gdn_chunk_call(q, k, v, proj, g, g_t, beta, norm_w):
    L, nh = q.shape
    R = GDN_STEP
    zc = COL_Z // nh
    return pl.pallas_call(
        _gdn_chunk_kernel,
        grid=(L // R,),
        in_specs=[
            pl.BlockSpec((R, nh), lambda i: (i, 0)),
            pl.BlockSpec((R, nh), lambda i: (i, 0)),
            pl.BlockSpec((R, nh), lambda i: (i, 0)),
            pl.BlockSpec((R, nh), lambda i: (i, zc)),
            pl.BlockSpec((R, DN_HEADS), lambda i: (i, 0)),
            pl.BlockSpec((DN_HEADS, R), lambda i: (0, i)),
            pl.BlockSpec((R, DN_HEADS), lambda i: (i, 0)),
            pl.BlockSpec((1, DN_DV), lambda i: (0, 0)),
        ],
        out_specs=pl.BlockSpec((R, nh), lambda i: (i, 0)),
        out_shape=jax.ShapeDtypeStruct((L, nh), BF16),
        scratch_shapes=[pltpu.VMEM((DN_HEADS, DN_DK, DN_DV), F32)],
        compiler_params=_params(("arbitrary",)),
        name="gdn_chunk",
    )(q, k, v, proj, g, g_t, beta, norm_w)


def _rope_tables(pos, period, lane):
    rot = period // ROPE_FRACTION
    half = rot // 2
    lp = lane % period
    expo = -((lp % half).astype(F32) * 2.0 / rot)
    inv_freq = jnp.power(jnp.float32(ROPE_THETA), expo)
    ang = pos * inv_freq
    cos, sin = jnp.cos(ang), jnp.sin(ang)
    c = jnp.where(lp < rot, cos, 1.0)
    s_lo = jnp.where(lp < half, -sin, 0.0)
    s_hi = jnp.where((lp >= half) & (lp < rot), sin, 0.0)
    return c, s_lo, s_hi, half


def _rope(x, tab):
    c, s_lo, s_hi, half = tab
    return x * c + pltpu.roll(x, LANES - half, 1) * s_lo + pltpu.roll(x, half, 1) * s_hi


def _dsa_prep_kernel(q_ref, k_ref, v_ref, iq_ref, sm_ref, pos_ref, kn_ref,
                     qo_ref, ko_ref, vto_ref, iqo_ref, iko_ref, wto_ref, qno_ref, kno_ref):
    lane = lax.broadcasted_iota(I32, (1, LANES), 1)
    pos = pos_ref[...]
    tab_att = _rope_tables(pos, ATT_DIM, lane)
    tab_idx = _rope_tables(pos, IDX_DIM, lane)
    scale = ATT_DIM ** -0.5 * LOG2_E
    q_norms = jnp.zeros((q_ref.shape[0], LANES), F32)
    k_norms = jnp.zeros((q_ref.shape[0], LANES), F32)
    for h in range(ATT_HEADS):
        hs = slice(h * ATT_DIM, (h + 1) * ATT_DIM)
        qh = (_rope(q_ref[:, hs].astype(F32), tab_att) * scale).astype(BF16)
        kh = _rope(k_ref[:, hs].astype(F32), tab_att).astype(BF16)
        qo_ref[:, hs] = qh
        ko_ref[:, hs] = kh
        vto_ref[hs, :] = v_ref[:, hs].astype(F32).T.astype(BF16)
        norm = lambda t: jnp.sqrt(jnp.sum(t.astype(F32) ** 2, axis=-1, keepdims=True))
        q_norms = jnp.where(lane == h, norm(qh), q_norms)
        k_norms = jnp.where(lane == h, norm(kh), k_norms)
    qno_ref[...] = q_norms.T[0:ATT_HEADS, :]
    kno_ref[...] = k_norms.T[0:ATT_HEADS, :]
    low = lane < IDX_DIM
    for p in range(IDX_HEADS // 2):
        x = _rope(iq_ref[:, p * LANES:(p + 1) * LANES].astype(F32), tab_idx)
        iqo_ref[:, (2 * p) * LANES:(2 * p + 1) * LANES] = jnp.where(low, x, 0.0).astype(BF16)
        iqo_ref[:, (2 * p + 1) * LANES:(2 * p + 2) * LANES] = jnp.where(low, pltpu.roll(x, IDX_DIM, 1), 0.0).astype(BF16)
    sm = sm_ref[...]
    kx = jnp.where(low, 0.0, sm)
    kx = kx * lax.rsqrt(jnp.sum(kx * kx, axis=-1, keepdims=True) * (1.0 / IDX_DIM) + EPS) * kn_ref[...]
    kx = _rope(kx, tab_idx)
    iko_ref[...] = jnp.where(low, pltpu.roll(kx, IDX_DIM, 1), 0.0).astype(BF16)
    wto_ref[...] = (sm * (IDX_HEADS ** -0.5 * IDX_DIM ** -0.5)).T[SM_IXW:SM_IXW + IDX_HEADS, :]


def _dsa_prep_call(proj, small, pos_col, kn_lanes):
    L = proj.shape[0]
    tT = min(256, L)
    na = ATT_HEADS * ATT_DIM
    ni = IDX_HEADS * IDX_DIM
    return pl.pallas_call(
        _dsa_prep_kernel,
        grid=(L // tT,),
        in_specs=[
            pl.BlockSpec((tT, na), lambda i: (i, COL_ATQ // na)),
            pl.BlockSpec((tT, na), lambda i: (i, COL_ATK // na)),
            pl.BlockSpec((tT, na), lambda i: (i, COL_ATV // na)),
            pl.BlockSpec((tT, ni), lambda i: (i, COL_IXQ // ni)),
            pl.BlockSpec((tT, LANES), lambda i: (i, 0)),
            pl.BlockSpec((tT, 1), lambda i: (i, 0)),
            pl.BlockSpec((1, LANES), lambda i: (0, 0)),
        ],
        out_specs=[
            pl.BlockSpec((tT, na), lambda i: (i, 0)),
            pl.BlockSpec((tT, na), lambda i: (i, 0)),
            pl.BlockSpec((na, tT), lambda i: (0, i)),
            pl.BlockSpec((tT, IDX_HEADS * LANES), lambda i: (i, 0)),
            pl.BlockSpec((tT, LANES), lambda i: (i, 0)),
            pl.BlockSpec((IDX_HEADS, tT), lambda i: (0, i)),
            pl.BlockSpec((ATT_HEADS, tT), lambda i: (0, i)),
            pl.BlockSpec((ATT_HEADS, tT), lambda i: (0, i)),
        ],
        out_shape=[jax.ShapeDtypeStruct((L, na), BF16), jax.ShapeDtypeStruct((L, na), BF16),
                   jax.ShapeDtypeStruct((na, L), BF16),
                   jax.ShapeDtypeStruct((L, IDX_HEADS * LANES), BF16), jax.ShapeDtypeStruct((L, LANES), BF16),
                   jax.ShapeDtypeStruct((IDX_HEADS, L), F32), jax.ShapeDtypeStruct((ATT_HEADS, L), F32),
                   jax.ShapeDtypeStruct((ATT_HEADS, L), F32)],
        compiler_params=_params(("parallel",)),
        name="dsa_prep",
    )(proj, proj, proj, proj, small, pos_col, kn_lanes)


def _index_scores_t(ik_blk, iq_heads, w_rows):
    acc = None
    for qh, wh in zip(iq_heads, w_rows):
        term = wh * jnp.maximum(_dot_nt(ik_blk, qh), 0.0)
        acc = term if acc is None else acc + term
    return acc


def _sortable(bits):
    return jnp.where(bits < 0, bits ^ jnp.int32(0x7FFFFFFF), bits)


def _score_keys(s):
    key = _sortable(pltpu.bitcast(s, I32))
    return jnp.where(key == -1, 0, key)


DSA_TQ = 256
DSA_TK = 512
SUM_ROWS = 16
KEY_NEG_INF = int(np.int32(np.array(-np.inf, np.float32).view(np.int32)) ^ np.int32(0x7FFFFFFF))
INT_MIN = -(2 ** 31)
INT_MAX = 2 ** 31 - 1


def _dsa_kernel(q_ref, iq_ref, wt_ref, qn_ref, ik_ref, kn_ref, k_hbm, vt_hbm, o_ref,
                sc_ref, hi_ref, kbuf, vbuf, sem, tie_ref, bias_ref, *acc_refs, topk, tk):
    TQ = q_ref.shape[0]
    L = ik_ref.shape[0]
    t0 = pl.program_id(0) * TQ
    nkt = (t0 + TQ + tk - 1) // tk

    def kv_copies(kt, slot):
        off = pl.multiple_of(kt * tk, tk)
        return (pltpu.make_async_copy(k_hbm.at[pl.ds(off, tk)], kbuf.at[slot], sem.at[0, slot]),
                pltpu.make_async_copy(vt_hbm.at[:, pl.ds(off, tk)], vbuf.at[slot], sem.at[1, slot]))

    for cp in kv_copies(0, 0):
        cp.start()

    iq_heads = [iq_ref[:, h * LANES:(h + 1) * LANES] for h in range(IDX_HEADS)]
    w_rows = [wt_ref[h:h + 1, :] for h in range(IDX_HEADS)]
    qpos = t0 + lax.broadcasted_iota(I32, (tk, TQ), 1)
    krow = lax.broadcasted_iota(I32, (tk, TQ), 0)

    def fill(kt, carry, on_diagonal):
        off = pl.multiple_of(kt * tk, tk)
        s = _index_scores_t(ik_ref[pl.ds(off, tk), :], iq_heads, w_rows)
        if on_diagonal:
            s = jnp.where(krow + off <= qpos, s, -jnp.inf)
        keys = _score_keys(s)
        sc_ref[pl.ds(off, tk), :] = keys
        hi_ref[pl.ds(off, tk), :] = (keys >> 16).astype(I16)
        return carry

    n_below = (t0 + 1) // tk
    lax.fori_loop(0, n_below, functools.partial(fill, on_diagonal=False), 0)
    lax.fori_loop(n_below, nkt, functools.partial(fill, on_diagonal=True), 0)

    def count(pred):
        def body(kt, acc):
            off = pl.multiple_of(kt * tk, tk)
            m = pred(sc_ref[pl.ds(off, tk), :], krow + off).astype(I32)
            return acc + jnp.sum(m.reshape(tk // 32, 32, TQ), axis=0)
        acc = lax.fori_loop(0, nkt, body, jnp.zeros((32, TQ), I32))
        return jnp.sum(acc, axis=0, keepdims=True)

    def count_hi(cand):
        def body(kt, acc):
            off = pl.multiple_of(kt * tk, tk)
            m = (hi_ref[pl.ds(off, tk), :] >= cand).astype(I16)
            for g in range(tk // 32):
                acc = acc + m[g * 32:(g + 1) * 32]
            return acc
        acc = lax.fori_loop(0, nkt, body, jnp.zeros((32, TQ), I16))
        return jnp.sum(acc.astype(I32), axis=0, keepdims=True)

    def hi_body(i, st):
        u, cnt = st
        uc = u | (jnp.int32(2 ** 15) >> i)
        c = count_hi((uc - 2 ** 15).astype(I16))
        ok = c >= topk
        return jnp.where(ok, uc, u), jnp.where(ok, c, cnt)

    cnt_all = jnp.zeros((1, TQ), I32) + nkt * tk
    u, cnt = lax.fori_loop(0, 16, hi_body, (jnp.zeros((1, TQ), I32), cnt_all))
    base = (u - 2 ** 15) << 16

    theta_hi = (u - 2 ** 15).astype(I16)
    above_hi = jnp.where(u >= 2 ** 16 - 1, 0, count_hi((jnp.minimum(u, 2 ** 16 - 2) + 1 - 2 ** 15).astype(I16)))

    def low_halves(kt, carry):
        off = pl.multiple_of(kt * tk, tk)
        low = ((sc_ref[pl.ds(off, tk), :] & 0xFFFF) - 2 ** 15).astype(I16)
        hi_ref[pl.ds(off, tk), :] = jnp.where(hi_ref[pl.ds(off, tk), :] == theta_hi, low, jnp.int16(-2 ** 15))
        return carry

    lax.fori_loop(0, nkt, low_halves, 0)

    def lo_body(i, st):
        delta, cnt = st
        dc = delta | (jnp.int32(2 ** 15) >> i)
        c = above_hi + count_hi((dc - 2 ** 15).astype(I16))
        ok = c >= topk
        return jnp.where(ok, dc, delta), jnp.where(ok, c, cnt)

    LO_GROUP = 4

    def group_cond(st):
        g, _, cnt = st
        return (g < 16 // LO_GROUP) & (jnp.max(jnp.abs(cnt - topk)) > 0)

    def group_body(st):
        g, delta, cnt = st
        delta, cnt = lax.fori_loop(0, LO_GROUP, lambda j, s: lo_body(g * LO_GROUP + j, s), (delta, cnt))
        return g + 1, delta, cnt

    _, delta, cnt = lax.while_loop(group_cond, group_body, (jnp.int32(0), jnp.zeros((1, TQ), I32), cnt))
    theta = base + delta

    none_valid = theta <= KEY_NEG_INF
    tied = (cnt > topk) & jnp.logical_not(none_valid)
    any_tied = jnp.max(tied.astype(I32)) > 0
    tie_ref[0:1, :] = jnp.where(none_valid, 0, INT_MAX)
    tie_ref[1:2, :] = jnp.zeros((1, TQ), I32)

    @pl.when(any_tied)
    def _():
        above = count(lambda keys, _: keys > theta)
        tie_ref[0:1, :] = jnp.where(tied, topk - above, tie_ref[0:1, :])

    need = tie_ref[0:1, :]
    theta_keep = jnp.where(need > 0, theta, theta + 1)

    ones_rows = jnp.ones((SUM_ROWS, tk), BF16)
    heads = [slice(h * ATT_DIM, (h + 1) * ATT_DIM) for h in range(ATT_HEADS)]

    def sweep(step, init):
        for acc_ref in acc_refs:
            acc_ref[...] = jnp.zeros_like(acc_ref)
        tie_ref[1:2, :] = jnp.zeros((1, TQ), I32)

        def body(kt, carry):
            slot = kt % 2
            off = pl.multiple_of(kt * tk, tk)
            for cp in kv_copies(kt, slot):
                cp.wait()

            @pl.when(kt + 1 < nkt)
            def _():
                for cp in kv_copies(kt + 1, 1 - slot):
                    cp.start()

            keys = sc_ref[pl.ds(off, tk), :]

            @pl.when(jnp.logical_not(any_tied))
            def _():
                bias_ref[...] = jnp.where(keys >= theta_keep, 0.0, NEG_BIG)

            @pl.when(any_tied)
            def _():
                tie = keys == theta
                r = lax.broadcasted_iota(I32, (tk, tk), 0)
                c = lax.broadcasted_iota(I32, (tk, tk), 1)
                earlier = _dot((r > c).astype(F32).astype(BF16), jnp.where(tie, 1.0, 0.0).astype(BF16))
                rank = earlier + tie_ref[1:2, :].astype(F32)
                keep = (keys > theta) | (tie & (rank < need.astype(F32)))
                bias_ref[...] = jnp.where(keep, 0.0, NEG_BIG)
                tie_ref[1:2, :] += jnp.sum(tie.astype(I32), axis=0, keepdims=True)

            return step(slot, carry)

        return lax.fori_loop(0, nkt, body, init)

    def v_ext(slot, h):
        return jnp.concatenate([vbuf[slot, heads[h], :], ones_rows], axis=0)

    shift = qn_ref[...] * jnp.max(kn_ref[...], axis=1, keepdims=True)

    def fixed_shift_step(slot, carry):
        probs = [jnp.exp2(_dot_nt(kbuf[slot, :, hs], q_ref[:, hs]) + bias_ref[...] - shift[h:h + 1, :]).astype(BF16)
                 for h, hs in enumerate(heads)]
        for h, acc_ref in enumerate(acc_refs):
            acc_ref[...] += _dot(v_ext(slot, h), probs[h])
        return carry

    sweep(fixed_shift_step, 0)
    norm_min = functools.reduce(jnp.minimum, [acc_ref[ATT_DIM:ATT_DIM + 1, :] for acc_ref in acc_refs])

    @pl.when(jnp.min(norm_min) < 2.0 ** -80)
    def _():
        def running_max_step(slot, m_run):
            logits, tile_max = [], []
            for hs in heads:
                s = _dot_nt(kbuf[slot, :, hs], q_ref[:, hs]) + bias_ref[...]
                logits.append(s)
                tile_max.append(jnp.max(s, axis=0, keepdims=True))
            m_rows = []
            for h, acc_ref in enumerate(acc_refs):
                m_old = m_run[h:h + 1, :]
                m_new = jnp.maximum(m_old, tile_max[h])
                p = jnp.exp2(logits[h] - m_new).astype(BF16)
                acc_ref[...] = jnp.exp2(m_old - m_new) * acc_ref[...] + _dot(v_ext(slot, h), p)
                m_rows.append(m_new)
            return jnp.concatenate(m_rows, axis=0)

        for cp in kv_copies(0, 0):
            cp.start()
        sweep(running_max_step, jnp.full((ATT_HEADS, TQ), NEG_BIG, F32))

    for h, acc_ref in enumerate(acc_refs):
        out_t = acc_ref[0:ATT_DIM, :] / acc_ref[ATT_DIM:ATT_DIM + 1, :]
        o_ref[:, h * ATT_DIM:(h + 1) * ATT_DIM] = out_t.T.astype(BF16)


def _dsa_call(q, k, vt, iq, ik, wt, qn, kn, topk):
    L, na = q.shape
    TQ = min(DSA_TQ, L)
    tk = min(DSA_TK, L)
    assert topk <= tk
    return pl.pallas_call(
        functools.partial(_dsa_kernel, topk=topk, tk=tk),
        grid=(L // TQ,),
        in_specs=[
            pl.BlockSpec((TQ, na), lambda i: (i, 0)),
            pl.BlockSpec((TQ, IDX_HEADS * LANES), lambda i: (i, 0)),
            pl.BlockSpec((IDX_HEADS, TQ), lambda i: (0, i)),
            pl.BlockSpec((ATT_HEADS, TQ), lambda i: (0, i)),
            pl.BlockSpec((L, LANES), lambda i: (0, 0)),
            pl.BlockSpec((ATT_HEADS, L), lambda i: (0, 0)),
            pl.BlockSpec(memory_space=pl.ANY),
            pl.BlockSpec(memory_space=pl.ANY),
        ],
        out_specs=pl.BlockSpec((TQ, na), lambda i: (i, 0)),
        out_shape=jax.ShapeDtypeStruct((L, na), BF16),
        scratch_shapes=[
            pltpu.VMEM((L, TQ), I32),
            pltpu.VMEM((L, TQ), I16),
            pltpu.VMEM((2, tk, na), BF16),
            pltpu.VMEM((2, na, tk), BF16),
            pltpu.SemaphoreType.DMA((2, 2)),
            pltpu.VMEM((8, TQ), I32),
            pltpu.VMEM((tk, TQ), F32),
        ] + [pltpu.VMEM((ATT_DIM + SUM_ROWS, TQ), F32)] * ATT_HEADS,
        compiler_params=_params(("parallel",), vmem_mb=56),
        name="dsa",
    )(q, iq, wt, qn, ik, kn, k, vt)


def _merge_kernel(g0_ref, g1_ref, b0_ref, b1_ref, ydn_ref, yat_ref, wdn_ref, wat_ref, o_ref):
    gate0 = _sigmoid(g0_ref[...].astype(F32) + b0_ref[...])
    gate1 = _sigmoid(g1_ref[...].astype(F32) + b1_ref[...])
    merged = gate0 * _dot(ydn_ref[...], wdn_ref[...]) + gate1 * _dot(yat_ref[...], wat_ref[...])
    o_ref[...] = merged.astype(o_ref.dtype)


def _merge_call(proj, b_gates, y_dn, y_at, w_dn, w_at):
    L = proj.shape[0]
    tm = min(1024, L)
    tn = 512
    nj = D_MODEL // tn
    c0 = COL_GATE // tn
    kd = y_dn.shape[1]
    return pl.pallas_call(
        _merge_kernel,
        grid=(L // tm, nj),
        in_specs=[
            pl.BlockSpec((tm, tn), lambda i, j: (i, c0 + j)),
            pl.BlockSpec((tm, tn), lambda i, j: (i, c0 + nj + j)),
            pl.BlockSpec((1, tn), lambda i, j: (0, j)),
            pl.BlockSpec((1, tn), lambda i, j: (0, nj + j)),
            pl.BlockSpec((tm, kd), lambda i, j: (i, 0)),
            pl.BlockSpec((tm, kd), lambda i, j: (i, 0)),
            pl.BlockSpec((kd, tn), lambda i, j: (0, j)),
            pl.BlockSpec((kd, tn), lambda i, j: (0, j)),
        ],
        out_specs=pl.BlockSpec((tm, tn), lambda i, j: (i, j)),
        out_shape=jax.ShapeDtypeStruct((L, D_MODEL), BF16),
        compiler_params=_params(("parallel", "arbitrary")),
        name="merge",
    )(proj, proj, b_gates, b_gates, y_dn, y_at, w_dn, w_at)


RT_E0, RT_E1, RT_W0, RT_W1 = 0, 1, 2, 3


def _first_lane_of_max(v, lane):
    m = jnp.max(v, axis=-1, keepdims=True)
    return m, jnp.min(jnp.where(v == m, lane, LANES), axis=-1, keepdims=True)


def _outproj_kernel(x_ref, mg_ref, wo_ref, nf_ref, wr_ref, br_ref, x1_ref, h2_ref, rt_ref):
    x1 = x_ref[...] + _dot(mg_ref[...], wo_ref[...])
    x1_ref[...] = x1
    h2 = x1 * lax.rsqrt(jnp.mean(x1 * x1, axis=-1, keepdims=True) + EPS) * nf_ref[...]
    h2_ref[...] = h2
    lg = _dot(h2.astype(BF16), wr_ref[...]) + br_ref[...]
    lane = lax.broadcasted_iota(I32, (1, LANES), 1)
    ninf = -jnp.inf
    gl = jnp.where(lane < N_GROUPS, lg, ninf)
    gmax, g_sel = _first_lane_of_max(gl, lane)
    p_group = 1.0 / jnp.sum(jnp.exp(gl - gmax), axis=-1, keepdims=True)
    ex = lane - N_GROUPS
    in_group = (ex >= 0) & (ex < N_EXPERTS) & ((ex // EXPERTS_PER_GROUP) == g_sel)
    el = jnp.where(in_group, lg, ninf)
    m1, i1 = _first_lane_of_max(el, lane)
    m2, i2 = _first_lane_of_max(jnp.where(lane == i1, ninf, el), lane)
    e2 = jnp.exp(m2 - m1)
    w0 = p_group / (1.0 + e2)
    w1 = p_group * e2 / (1.0 + e2)
    rec = jnp.where(lane == RT_E0, (i1 - N_GROUPS).astype(F32), 0.0)
    rec = jnp.where(lane == RT_E1, (i2 - N_GROUPS).astype(F32), rec)
    rec = jnp.where(lane == RT_W0, w0, rec)
    rt_ref[...] = jnp.where(lane == RT_W1, w1, rec)


def _outproj_call(x2, merged, w_out, norm_ffn, w_route, b_route):
    L = x2.shape[0]
    tm = min(256, L)
    row = lambda i: (i, 0)
    fixed = lambda i: (0, 0)
    return pl.pallas_call(
        _outproj_kernel,
        grid=(L // tm,),
        in_specs=[
            pl.BlockSpec((tm, D_MODEL), row),
            pl.BlockSpec((tm, D_MODEL), row),
            pl.BlockSpec((D_MODEL, D_MODEL), fixed),
            pl.BlockSpec((1, D_MODEL), fixed),
            pl.BlockSpec((D_MODEL, LANES), fixed),
            pl.BlockSpec((1, LANES), fixed),
        ],
        out_specs=[pl.BlockSpec((tm, D_MODEL), row), pl.BlockSpec((tm, D_MODEL), row), pl.BlockSpec((tm, LANES), row)],
        out_shape=[jax.ShapeDtypeStruct((L, D_MODEL), F32), jax.ShapeDtypeStruct((L, D_MODEL), F32),
                   jax.ShapeDtypeStruct((L, LANES), F32)],
        compiler_params=_params(("parallel",)),
        name="outproj_route",
    )(x2, merged, w_out, norm_ffn, w_route, b_route)


def _moe_slots(L):
    return -(-(2 * L + N_EXPERTS * (MOE_ROWS - 1)) // MOE_ROWS) * MOE_ROWS


def _moe_plan_kernel(rt_ref, dest_ref, blk_ref, cnt_ref, carry_ref, start_ref):
    phase, i = pl.program_id(0), pl.program_id(1)
    tR = rt_ref.shape[0]
    lane = lax.broadcasted_iota(I32, (1, LANES), 1)
    lane_f = lane.astype(F32)
    rt = rt_ref[...]
    e0, e1 = rt[:, RT_E0:RT_E0 + 1], rt[:, RT_E1:RT_E1 + 1]
    hit0, hit1 = lane_f == e0, lane_f == e1
    onehot = (hit0 | hit1).astype(F32)
    colsum = jnp.sum(onehot, axis=0, keepdims=True)

    @pl.when((phase == 0) & (i == 0))
    def _():
        cnt_ref[...] = jnp.zeros_like(cnt_ref)

    @pl.when(phase == 0)
    def _():
        cnt_ref[...] += colsum

    @pl.when((phase == 1) & (i == 0))
    def _():
        carry_ref[...] = jnp.zeros_like(carry_ref)
        padded = jnp.floor((cnt_ref[...] + (MOE_ROWS - 1)) * (1.0 / MOE_ROWS)) * MOE_ROWS
        r = lax.broadcasted_iota(I32, (LANES, LANES), 0)
        c = lax.broadcasted_iota(I32, (LANES, LANES), 1)
        upper = (r <= c).astype(F32)
        end = jnp.dot(jnp.broadcast_to(padded, (8, LANES)), upper, preferred_element_type=F32,
                      precision=lax.Precision.HIGHEST)[0:1, :]
        start_ref[...] = end - padded
        n_used = end[:, N_EXPERTS - 1:N_EXPERTS] * (1.0 / MOE_ROWS)
        nb = blk_ref.shape[0] - 8
        b = lax.broadcasted_iota(I32, (nb, 1), 0).astype(F32)
        b_eff = jnp.minimum(b, n_used - 1.0)
        done = ((end <= b_eff * MOE_ROWS) & (lane < N_EXPERTS)).astype(F32)
        blk_e = jnp.minimum(jnp.sum(done, axis=-1, keepdims=True), N_EXPERTS - 1.0)
        blk_ref[0:nb, :] = jnp.where(lane == 0, blk_e, jnp.where(lane == 1, n_used, 0.0)).astype(I32)
        row = lax.broadcasted_iota(I32, (8, LANES), 0)
        seg = jnp.where(row == 0, end - padded, jnp.where(row == 1, cnt_ref[...], jnp.where(row == 2, padded, 0.0)))
        blk_ref[nb:nb + 8, :] = seg.astype(I32)

    @pl.when(phase == 1)
    def _():
        r = lax.broadcasted_iota(I32, (tR, tR), 0)
        c = lax.broadcasted_iota(I32, (tR, tR), 1)
        before = _dot((r > c).astype(BF16), onehot.astype(BF16)) + carry_ref[...]
        slot = before + start_ref[...]
        d0 = jnp.sum(jnp.where(hit0, slot, 0.0), axis=-1, keepdims=True)
        d1 = jnp.sum(jnp.where(hit1, slot, 0.0), axis=-1, keepdims=True)
        dest_ref[...] = jnp.where(lane == 0, d0, jnp.where(lane == 1, d1, 0.0)).astype(I32)
        carry_ref[...] += colsum


def _moe_plan_call(route):
    L = route.shape[0]
    tR = min(256, L)
    nb = _moe_slots(L) // MOE_ROWS
    nb_pad = -(-nb // 8) * 8
    return pl.pallas_call(
        _moe_plan_kernel,
        grid=(2, L // tR),
        in_specs=[pl.BlockSpec((tR, LANES), lambda p, i: (i, 0))],
        out_specs=[pl.BlockSpec((tR, LANES), lambda p, i: (p * i, 0)),
                   pl.BlockSpec((nb_pad + 8, LANES), lambda p, i: (0, 0))],
        out_shape=[jax.ShapeDtypeStruct((L, LANES), I32), jax.ShapeDtypeStruct((nb_pad + 8, LANES), I32)],
        scratch_shapes=[pltpu.VMEM((1, LANES), F32)] * 3,
        compiler_params=_params(("arbitrary", "arbitrary")),
        name="moe_plan",
    )(route)


PLAN_START, PLAN_COUNT, PLAN_PADDED, PLAN_USED = 0, N_EXPERTS, 2 * N_EXPERTS, 3 * N_EXPERTS
PAD_PIECES = tuple(MOE_ROWS >> s for s in range(1, MOE_ROWS.bit_length() - 3))


def _moe_scatter_kernel(dest_ref, plan_ref, h_ref, xs_out, zero_ref, sem, zero_sem, *, rows, nb):
    base = pl.program_id(0) * rows

    def row_copy(r, j):
        return pltpu.make_async_copy(h_ref.at[pl.ds(r, 1)], xs_out.at[pl.ds(dest_ref[2 * (base + r) + j], 1)], sem)

    def start(r, c):
        row_copy(r, 0).start()
        row_copy(r, 1).start()
        return c

    def wait(r, c):
        row_copy(r, 0).wait()
        row_copy(r, 1).wait()
        return c

    lax.fori_loop(0, rows, start, 0)

    @pl.when(pl.program_id(0) == 0)
    def _():
        zero_ref[...] = jnp.zeros_like(zero_ref)

        def for_each_pad_piece(act):
            def per_expert(e, c):
                count = plan_ref[PLAN_COUNT + e]
                pad = plan_ref[PLAN_PADDED + e] - count
                first = plan_ref[PLAN_START + e] + count
                end = first + pad

                def single_row(r, cc):
                    act(pltpu.make_async_copy(zero_ref.at[pl.ds(0, 1)], xs_out.at[pl.ds(first + r, 1)], zero_sem))
                    return cc

                lax.fori_loop(0, pad & 7, single_row, 0)
                for k in PAD_PIECES:
                    @pl.when((pad & k) != 0)
                    def _():
                        at = pl.multiple_of(end - (pad & ~(k - 1)), 8)
                        act(pltpu.make_async_copy(zero_ref.at[pl.ds(0, k)], xs_out.at[pl.ds(at, k)], zero_sem))
                return c
            lax.fori_loop(0, N_EXPERTS, per_expert, 0)

        def for_each_unused_block(act):
            def per_block(b, c):
                act(pltpu.make_async_copy(zero_ref, xs_out.at[pl.ds(b * MOE_ROWS, MOE_ROWS)], zero_sem))
                return c
            lax.fori_loop(plan_ref[PLAN_USED], nb, per_block, 0)

        for_each_pad_piece(lambda cp: cp.start())
        for_each_unused_block(lambda cp: cp.start())
        for_each_pad_piece(lambda cp: cp.wait())
        for_each_unused_block(lambda cp: cp.wait())

    lax.fori_loop(0, rows, wait, 0)


def _moe_scatter_call(dest_flat, plan, h2):
    L = h2.shape[0]
    rows = min(256, L)
    slots = _moe_slots(L)
    return pl.pallas_call(
        functools.partial(_moe_scatter_kernel, rows=rows, nb=slots // MOE_ROWS),
        grid_spec=pltpu.PrefetchScalarGridSpec(
            num_scalar_prefetch=2,
            grid=(L // rows,),
            in_specs=[pl.BlockSpec((rows, D_MODEL), lambda i, d, p: (i, 0))],
            out_specs=pl.BlockSpec(memory_space=pl.ANY),
            scratch_shapes=[pltpu.VMEM((MOE_ROWS, D_MODEL), F32), pltpu.SemaphoreType.DMA(()),
                            pltpu.SemaphoreType.DMA(())],
        ),
        out_shape=jax.ShapeDtypeStruct((slots, D_MODEL), F32),
        compiler_params=_params(("arbitrary",)),
        name="moe_scatter",
    )(dest_flat, plan, h2)


def _moe_ffn_kernel(be_ref, x_ref, wg_ref, wu_ref, wd_ref, y_ref, *, nb):
    b = pl.program_id(0)

    @pl.when(b < be_ref[nb])
    def _():
        xb = x_ref[...].astype(BF16)
        gate = _dot(xb, wg_ref[...].astype(BF16))
        up = _dot(xb, wu_ref[...].astype(BF16))
        hidden = (gate * _sigmoid(gate) * up).astype(BF16)
        y_ref[...] = _dot(hidden, wd_ref[...].astype(BF16))

    @pl.when(b >= be_ref[nb])
    def _():
        y_ref[...] = jnp.zeros_like(y_ref)


def _moe_ffn_call(blk_e, xs, w_gate, w_up, w_down):
    P = xs.shape[0]
    nb = P // MOE_ROWS
    return pl.pallas_call(
        functools.partial(_moe_ffn_kernel, nb=nb),
        grid_spec=pltpu.PrefetchScalarGridSpec(
            num_scalar_prefetch=1,
            grid=(nb,),
            in_specs=[
                pl.BlockSpec((MOE_ROWS, D_MODEL), lambda b, be: (b, 0)),
                pl.BlockSpec((None, D_MODEL, EXPERT_HIDDEN), lambda b, be: (be[b], 0, 0)),
                pl.BlockSpec((None, D_MODEL, EXPERT_HIDDEN), lambda b, be: (be[b], 0, 0)),
                pl.BlockSpec((None, EXPERT_HIDDEN, D_MODEL), lambda b, be: (be[b], 0, 0)),
            ],
            out_specs=pl.BlockSpec((MOE_ROWS, D_MODEL), lambda b, be: (b, 0)),
        ),
        out_shape=jax.ShapeDtypeStruct((P, D_MODEL), F32),
        compiler_params=_params(("arbitrary",), vmem_mb=56),
        name="moe_ffn",
    )(blk_e, xs, w_gate, w_up, w_down)


def _moe_combine_kernel(dest_ref, x1_ref, rt_ref, nf_ref, ys_hbm, o_ref, ya_ref, yb_ref, sem, *, rows):
    step, n_steps = pl.program_id(0), pl.num_programs(0)

    def gather(s, act):
        slot = s % 2

        def body(r, c):
            for j, dst in enumerate((ya_ref, yb_ref)):
                src = ys_hbm.at[pl.ds(dest_ref[2 * (s * rows + r) + j], 1)]
                act(pltpu.make_async_copy(src, dst.at[slot, pl.ds(r, 1)], sem.at[slot]))
            return c

        lax.fori_loop(0, rows, body, 0)

    @pl.when(step == 0)
    def _():
        gather(step, lambda cp: cp.start())

    @pl.when(step + 1 < n_steps)
    def _():
        gather(step + 1, lambda cp: cp.start())

    gather(step, lambda cp: cp.wait())
    slot = step % 2
    rt = rt_ref[...]
    x = x1_ref[...] + rt[:, RT_W0:RT_W0 + 1] * ya_ref[slot] + rt[:, RT_W1:RT_W1 + 1] * yb_ref[slot]
    o_ref[...] = x * lax.rsqrt(jnp.mean(x * x, axis=-1, keepdims=True) + EPS) * nf_ref[...]


def _moe_combine_call(dest_flat, x1, route, norm_final, ys):
    L = x1.shape[0]
    rows = min(256, L)
    return pl.pallas_call(
        functools.partial(_moe_combine_kernel, rows=rows),
        grid_spec=pltpu.PrefetchScalarGridSpec(
            num_scalar_prefetch=1,
            grid=(L // rows,),
            in_specs=[
                pl.BlockSpec((rows, D_MODEL), lambda i, d: (i, 0)),
                pl.BlockSpec((rows, LANES), lambda i, d: (i, 0)),
                pl.BlockSpec((1, D_MODEL), lambda i, d: (0, 0)),
                pl.BlockSpec(memory_space=pl.ANY),
            ],
            out_specs=pl.BlockSpec((rows, D_MODEL), lambda i, d: (i, 0)),
            scratch_shapes=[pltpu.VMEM((2, rows, D_MODEL), F32), pltpu.VMEM((2, rows, D_MODEL), F32),
                            pltpu.SemaphoreType.DMA((2,))],
        ),
        out_shape=jax.ShapeDtypeStruct((L, D_MODEL), F32),
        compiler_params=_params(("arbitrary",)),
        name="moe_combine",
    )(dest_flat, x1, route, norm_final, ys)


def _pack_w_in(w):
    s = np.cumsum((0, QKV_COLS, DN_HEADS * DN_DV, DN_HEADS, DN_HEADS, ATT_HEADS * ATT_DIM, ATT_HEADS * ATT_DIM,
                   ATT_HEADS * ATT_DIM, IDX_HEADS * IDX_DIM, IDX_DIM, IDX_HEADS, 2 * D_MODEL))
    seg = lambda n: w[:, int(s[n]):int(s[n + 1])]
    w_main = jnp.concatenate([seg(0), seg(1), seg(4), seg(5), seg(6), seg(7), seg(10)], axis=1).astype(BF16)
    pad = jnp.zeros((w.shape[0], SM_IXK - SM_IXW - IDX_HEADS), w.dtype)
    w_small = jnp.concatenate([seg(2), seg(3), seg(9), pad, seg(8)], axis=1).astype(BF16)
    return w_main, w_small


def _forward(x, positions, norm_mix, w_in, b_gates, dn_conv_w, dn_a_log, dn_dt_bias, dn_norm_w, idx_k_norm,
             w_proj_dn, w_proj_att, w_out, norm_ffn, w_group, b_group, w_router, b_router, w_exp_gate, w_exp_up,
             w_exp_down, norm_final):
    st = {}
    L = x.shape[1]
    x2 = x.reshape(L, D_MODEL)
    w_main, w_small = _pack_w_in(w_in[0])
    proj, small = _proj_call(x2, norm_mix[0].reshape(1, D_MODEL), w_main, w_small)
    st["proj"], st["small"] = proj, small
    q, k, v, g, beta = _gdn_prep_call(proj, small, dn_conv_w[0], dn_a_log[0].reshape(1, DN_HEADS),
                                      dn_dt_bias[0].reshape(1, DN_HEADS))
    st["y_dn"] = _gdn_chunk_call(q, k, v, proj, g, g.T, beta, dn_norm_w[0].reshape(1, DN_DV))

    pos_col = positions.reshape(L, 1).astype(F32)
    kn_lanes = jnp.concatenate([jnp.zeros((SM_IXK,), F32), idx_k_norm[0].astype(F32)]).reshape(1, LANES)
    aq, ak, avt, iq, ik, wt, qn, kn = _dsa_prep_call(proj, small, pos_col, kn_lanes)
    st["y_at"] = _dsa_call(aq, ak, avt, iq, ik, wt, qn, kn, min(TOPK_MAX, L // 4))

    merged = _merge_call(proj, b_gates[0].reshape(1, 2 * D_MODEL), st["y_dn"], st["y_at"],
                         w_proj_dn[0].astype(BF16), w_proj_att[0].astype(BF16))
    st["merged"] = merged
    n_route = N_GROUPS + N_EXPERTS
    w_route = jnp.concatenate([w_group[0], w_router[0], jnp.zeros((D_MODEL, LANES - n_route), F32)], axis=1).astype(BF16)
    b_route = jnp.concatenate([b_group[0], b_router[0], jnp.zeros((LANES - n_route,), F32)]).reshape(1, LANES)
    x1, h2, route = _outproj_call(x2, merged, w_out[0].astype(BF16), norm_ffn[0].reshape(1, D_MODEL), w_route, b_route)
    st["x1"], st["h2"], st["route"] = x1, h2, route

    dest, blk = _moe_plan_call(route)
    dest_flat = dest[:, :2].reshape(2 * L)
    slots = _moe_slots(L)
    nb = slots // MOE_ROWS
    blk_e = jnp.concatenate([blk[:nb, 0], blk[0:1, 1]])
    seg = blk[blk.shape[0] - 8:blk.shape[0] - 5, :N_EXPERTS]
    plan = jnp.concatenate([seg.reshape(3 * N_EXPERTS), blk[0:1, 1]])
    xs = _moe_scatter_call(dest_flat, plan, h2)
    ys = _moe_ffn_call(blk_e, xs, w_exp_gate[0], w_exp_up[0], w_exp_down[0])
    out = _moe_combine_call(dest_flat, x1, route, norm_final.reshape(1, D_MODEL), ys)
    st["out"] = out.reshape(1, L, D_MODEL)
    return st


def kernel(x, positions, norm_mix, w_in, b_gates, dn_conv_w, dn_a_log, dn_dt_bias, dn_norm_w, idx_k_norm, w_proj_dn,
           w_proj_att, w_out, norm_ffn, w_group, b_group, w_router, b_router, w_exp_gate, w_exp_up, w_exp_down,
           norm_final):
    return _forward(x, positions, norm_mix, w_in, b_gates, dn_conv_w, dn_a_log, dn_dt_bias, dn_norm_w, idx_k_norm,
                    w_proj_dn, w_proj_att, w_out, norm_ffn, w_group, b_group, w_router, b_router, w_exp_gate,
                    w_exp_up, w_exp_down, norm_final)["out"]


def _stages(d, upto=None):
    return _forward(*[d[n] for n in ("x", "positions", "norm_mix", "w_in", "b_gates", "dn_conv_w", "dn_a_log",
                                     "dn_dt_bias", "dn_norm_w", "idx_k_norm", "w_proj_dn", "w_proj_att", "w_out",
                                     "norm_ffn", "w_group", "b_group", "w_router", "b_router", "w_exp_gate",
                                     "w_exp_up", "w_exp_down", "norm_final")])
```

```python
import functools

import jax
import jax.numpy as jnp
import numpy as np
from jax import lax
from jax.experimental import pallas as pl
from jax.experimental.pallas import tpu as pltpu

D_MODEL = 2048
DN_HEADS = 8
DN_DK = 128
DN_DV = 128
DN_CONV = 4
DN_CHUNK = 64
ATT_HEADS = 8
ATT_DIM = 128
IDX_HEADS = 8
IDX_DIM = 64
TOPK_MAX = 256
ROPE_THETA = 500000.0
ROPE_FRACTION = 4
N_GROUPS = 8
EXPERTS_PER_GROUP = 8
N_EXPERTS = N_GROUPS * EXPERTS_PER_GROUP
EXPERT_HIDDEN = 512
EPS = 1e-6

LANES = 128
MOE_ROWS = 256
NEG_BIG = -1e30
LOG2_E = 1.4426950408889634

F32 = jnp.float32
BF16 = jnp.bfloat16
I32 = jnp.int32
I16 = jnp.int16

QKV_COLS = 2 * DN_HEADS * DN_DK + DN_HEADS * DN_DV
COL_QKV = 0
COL_Z = COL_QKV + QKV_COLS
COL_ATQ = COL_Z + DN_HEADS * DN_DV
COL_ATK = COL_ATQ + ATT_HEADS * ATT_DIM
COL_ATV = COL_ATK + ATT_HEADS * ATT_DIM
COL_IXQ = COL_ATV + ATT_HEADS * ATT_DIM
COL_GATE = COL_IXQ + IDX_HEADS * IDX_DIM
MAIN_COLS = COL_GATE + 2 * D_MODEL
SM_B = 0
SM_A = 8
SM_IXW = 16
SM_IXK = 64


def _params(sem, vmem_mb=48):
    return pltpu.CompilerParams(dimension_semantics=sem, vmem_limit_bytes=vmem_mb * 1024 * 1024)


def _sigmoid(x):
    return 1.0 / (1.0 + jnp.exp(-x))


def _dot(a, b):
    return jnp.dot(a, b, preferred_element_type=F32)


def _dot_nt(a, b):
    return lax.dot_general(a, b, (((1,), (1,)), ((), ())), preferred_element_type=F32)


def _proj_kernel(x_ref, g_ref, w_ref, ws_ref, o_ref, os_ref, h_ref):
    @pl.when(pl.program_id(1) == 0)
    def _():
        x = x_ref[...]
        h = x * lax.rsqrt(jnp.mean(x * x, axis=-1, keepdims=True) + EPS) * g_ref[...]
        h_ref[...] = h.astype(BF16)
        os_ref[...] = _dot(h_ref[...], ws_ref[...])

    o_ref[...] = _dot(h_ref[...], w_ref[...]).astype(o_ref.dtype)


def _proj_call(x2, gain, w_main, w_small):
    L, D = x2.shape
    N = w_main.shape[1]
    tm = min(512, L)
    tn = N // 4
    return pl.pallas_call(
        _proj_kernel,
        grid=(L // tm, N // tn),
        in_specs=[
            pl.BlockSpec((tm, D), lambda i, j: (i, 0)),
            pl.BlockSpec((1, D), lambda i, j: (0, 0)),
            pl.BlockSpec((D, tn), lambda i, j: (0, j)),
            pl.BlockSpec((D, LANES), lambda i, j: (0, 0)),
        ],
        out_specs=[
            pl.BlockSpec((tm, tn), lambda i, j: (i, j)),
            pl.BlockSpec((tm, LANES), lambda i, j: (i, 0)),
        ],
        out_shape=[jax.ShapeDtypeStruct((L, N), BF16), jax.ShapeDtypeStruct((L, LANES), F32)],
        scratch_shapes=[pltpu.VMEM((tm, D), BF16)],
        compiler_params=_params(("parallel", "arbitrary")),
        name="proj",
    )(x2, gain, w_main, w_small)


def _gdn_prep_kernel(qkv_ref, halo_ref, sm_ref, cw_ref, alog_ref, dtb_ref,
                     q_ref, k_ref, v_ref, g_ref, b_ref, xs_ref):
    tT = qkv_ref.shape[0]
    first = pl.program_id(0) == 0
    xs_ref[8:8 + tT, :] = qkv_ref[...].astype(F32)
    halo = halo_ref[8:16, :].astype(F32)
    xs_ref[0:8, :] = jnp.where(first, 0.0, halo)
    nh = DN_HEADS * DN_DK
    for c in range(QKV_COLS // LANES):
        sl = slice(c * LANES, (c + 1) * LANES)
        y = xs_ref[8:8 + tT, sl] * cw_ref[3:4, sl]
        for j in range(DN_CONV - 1):
            y = y + xs_ref[5 + j:5 + j + tT, sl] * cw_ref[j:j + 1, sl]
        y = y * _sigmoid(y)
        if c < 2 * DN_HEADS:
            y = y * lax.rsqrt(jnp.sum(y * y, axis=-1, keepdims=True) + EPS)
            if c < DN_HEADS:
                q_ref[:, sl] = (y * (DN_DK ** -0.5)).astype(BF16)
            else:
                k_ref[:, c * LANES - nh:(c + 1) * LANES - nh] = y.astype(BF16)
        else:
            v_ref[:, c * LANES - 2 * nh:(c + 1) * LANES - 2 * nh] = y.astype(BF16)
    sm = sm_ref[...]
    b_ref[...] = _sigmoid(sm[:, SM_B:SM_B + DN_HEADS])
    a = sm[:, SM_A:SM_A + DN_HEADS] + dtb_ref[...]
    softplus = jnp.maximum(a, 0.0) + jnp.log1p(jnp.exp(-jnp.abs(a)))
    g_ref[...] = -jnp.exp(alog_ref[...]) * softplus


def _gdn_prep_call(proj, small, conv_w, a_log, dt_bias):
    L = proj.shape[0]
    tT = min(256, L)
    hb = tT // 16
    nh = DN_HEADS * DN_DK
    return pl.pallas_call(
        _gdn_prep_kernel,
        grid=(L // tT,),
        in_specs=[
            pl.BlockSpec((tT, QKV_COLS), lambda i: (i, 0)),
            pl.BlockSpec((16, QKV_COLS), lambda i: (jnp.maximum(i * hb - 1, 0), 0)),
            pl.BlockSpec((tT, LANES), lambda i: (i, 0)),
            pl.BlockSpec((DN_CONV, QKV_COLS), lambda i: (0, 0)),
            pl.BlockSpec((1, DN_HEADS), lambda i: (0, 0)),
            pl.BlockSpec((1, DN_HEADS), lambda i: (0, 0)),
        ],
        out_specs=[
            pl.BlockSpec((tT, nh), lambda i: (i, 0)),
            pl.BlockSpec((tT, nh), lambda i: (i, 0)),
            pl.BlockSpec((tT, nh), lambda i: (i, 0)),
            pl.BlockSpec((tT, DN_HEADS), lambda i: (i, 0)),
            pl.BlockSpec((tT, DN_HEADS), lambda i: (i, 0)),
        ],
        out_shape=[jax.ShapeDtypeStruct((L, nh), BF16)] * 3 + [jax.ShapeDtypeStruct((L, DN_HEADS), F32)] * 2,
        scratch_shapes=[pltpu.VMEM((tT + 8, QKV_COLS), F32)],
        compiler_params=_params(("parallel",)),
        name="gdn_prep",
    )(proj, proj, small, conv_w, a_log, dt_bias)


GDN_STEP = 4 * DN_CHUNK


def _bdot(a, b):
    return lax.dot_general(a, b, (((2,), (1,)), ((0,), (0,))), preferred_element_type=F32)


def _bdot_nt(a, b):
    return lax.dot_general(a, b, (((2,), (2,)), ((0,), (0,))), preferred_element_type=F32)


def _gdn_chunk_kernel(q_ref, k_ref, v_ref, z_ref, gc_ref, gr_ref, bc_ref, nw_ref, y_ref, s_ref):
    C, H = DN_CHUNK, DN_HEADS
    n_chunks = q_ref.shape[0] // C

    @pl.when(pl.program_id(0) == 0)
    def _():
        s_ref[...] = jnp.zeros_like(s_ref)

    row = lax.broadcasted_iota(I32, (C, C), 0)
    col = lax.broadcasted_iota(I32, (C, C), 1)
    incl = row >= col
    strict = row > col
    tri = incl.astype(F32)
    tri_t = (row <= col).astype(F32)
    eye = (row == col).astype(F32)

    def per_head(fn):
        return jnp.stack([fn(slice(c * C, (c + 1) * C), h) for c in range(n_chunks) for h in range(H)])

    head = lambda ref: per_head(lambda rs, h: ref[rs, h * DN_DK:(h + 1) * DN_DK])
    qb, kb, vb = head(q_ref), head(k_ref), head(v_ref)
    G_col = [jnp.dot(tri, gc_ref[c * C:(c + 1) * C, :], preferred_element_type=F32, precision=lax.Precision.HIGHEST)
             for c in range(n_chunks)]
    G_row = [jnp.dot(gr_ref[:, c * C:(c + 1) * C], tri_t, preferred_element_type=F32, precision=lax.Precision.HIGHEST)
             for c in range(n_chunks)]
    Gc = jnp.stack([G_col[c][:, h:h + 1] for c in range(n_chunks) for h in range(H)])
    Gr = jnp.stack([G_row[c][h:h + 1, :] for c in range(n_chunks) for h in range(H)])
    Gl = jnp.stack([G_col[c][C - 1:C, h:h + 1] for c in range(n_chunks) for h in range(H)])
    bcol = per_head(lambda rs, h: bc_ref[rs, h:h + 1])

    decay = jnp.exp(jnp.where(incl[None], Gc - Gr, -jnp.inf))
    A = jnp.where(strict[None], bcol * _bdot_nt(kb, kb) * decay, 0.0)
    M = -A
    T = eye[None] + M
    for _ in range(5):
        Mb = M.astype(BF16)
        M = _bdot(Mb, Mb)
        T = T + _bdot(T.astype(BF16), M.astype(BF16))
    Tb = T.astype(BF16)
    eg = jnp.exp(Gc)
    kf = kb.astype(F32)
    w = _bdot(Tb, (kf * (bcol * eg)).astype(BF16)).astype(BF16)
    u = _bdot(Tb, (vb.astype(F32) * bcol).astype(BF16))
    attn = (_bdot_nt(qb, kb) * decay).astype(BF16)
    q_dec = (qb.astype(F32) * eg).astype(BF16)
    k_dec = (kf * jnp.exp(Gl - Gc)).astype(BF16)
    g_last = jnp.exp(Gl)

    nw = nw_ref[...]
    S = s_ref[...]
    for c in range(n_chunks):
        rs = slice(c * C, (c + 1) * C)
        bs = slice(c * H, (c + 1) * H)
        Sb = S.astype(BF16)
        v_new = (u[bs] - _bdot(w[bs], Sb)).astype(BF16)
        o = _bdot(q_dec[bs], Sb) + _bdot(attn[bs], v_new)
        S = g_last[bs] * S + jnp.stack([
            lax.dot_general(k_dec[c * H + h], v_new[h], (((0,), (0,)), ((), ())), preferred_element_type=F32)
            for h in range(H)])
        on = o * lax.rsqrt(jnp.mean(o * o, axis=-1, keepdims=True) + EPS) * nw
        for h in range(H):
            hs = slice(h * DN_DV, (h + 1) * DN_DV)
            zz = z_ref[rs, hs].astype(F32)
            y_ref[rs, hs] = (on[h] * (zz * _sigmoid(zz))).astype(BF16)
    s_ref[...] = S


def _gdn_chunk_call(q, k, v, proj, g, g_t, beta, norm_w):
    L, nh = q.shape
    R = GDN_STEP
    zc = COL_Z // nh
    return pl.pallas_call(
        _gdn_chunk_kernel,
        grid=(L // R,),
        in_specs=[
            pl.BlockSpec((R, nh), lambda i: (i, 0)),
            pl.BlockSpec((R, nh), lambda i: (i, 0)),
            pl.BlockSpec((R, nh), lambda i: (i, 0)),
            pl.BlockSpec((R, nh), lambda i: (i, zc)),
            pl.BlockSpec((R, DN_HEADS), lambda i: (i, 0)),
            pl.BlockSpec((DN_HEADS, R), lambda i: (0, i)),
            pl.BlockSpec((R, DN_HEADS), lambda i: (i, 0)),
            pl.BlockSpec((1, DN_DV), lambda i: (0, 0)),
        ],
        out_specs=pl.BlockSpec((R, nh), lambda i: (i, 0)),
        out_shape=jax.ShapeDtypeStruct((L, nh), BF16),
        scratch_shapes=[pltpu.VMEM((DN_HEADS, DN_DK, DN_DV), F32)],
        compiler_params=_params(("arbitrary",)),
        name="gdn_chunk",
    )(q, k, v, proj, g, g_t, beta, norm_w)


def _rope_tables(pos, period, lane):
    rot = period // ROPE_FRACTION
    half = rot // 2
    lp = lane % period
    expo = -((lp % half).astype(F32) * 2.0 / rot)
    inv_freq = jnp.power(jnp.float32(ROPE_THETA), expo)
    ang = pos * inv_freq
    cos, sin = jnp.cos(ang), jnp.sin(ang)
    c = jnp.where(lp < rot, cos, 1.0)
    s_lo = jnp.where(lp < half, -sin, 0.0)
    s_hi = jnp.where((lp >= half) & (lp < rot), sin, 0.0)
    return c, s_lo, s_hi, half


def _rope(x, tab):
    c, s_lo, s_hi, half = tab
    return x * c + pltpu.roll(x, LANES - half, 1) * s_lo + pltpu.roll(x, half, 1) * s_hi


def _dsa_prep_kernel(q_ref, k_ref, v_ref, iq_ref, sm_ref, pos_ref, kn_ref,
                     qo_ref, ko_ref, vto_ref, iqo_ref, iko_ref, wto_ref, qno_ref, kno_ref):
    lane = lax.broadcasted_iota(I32, (1, LANES), 1)
    pos = pos_ref[...]
    tab_att = _rope_tables(pos, ATT_DIM, lane)
    tab_idx = _rope_tables(pos, IDX_DIM, lane)
    scale = ATT_DIM ** -0.5 * LOG2_E
    q_norms = jnp.zeros((q_ref.shape[0], LANES), F32)
    k_norms = jnp.zeros((q_ref.shape[0], LANES), F32)
    for h in range(ATT_HEADS):
        hs = slice(h * ATT_DIM, (h + 1) * ATT_DIM)
        qh = (_rope(q_ref[:, hs].astype(F32), tab_att) * scale).astype(BF16)
        kh = _rope(k_ref[:, hs].astype(F32), tab_att).astype(BF16)
        qo_ref[:, hs] = qh
        ko_ref[:, hs] = kh
        vto_ref[hs, :] = v_ref[:, hs].astype(F32).T.astype(BF16)
        norm = lambda t: jnp.sqrt(jnp.sum(t.astype(F32) ** 2, axis=-1, keepdims=True))
        q_norms = jnp.where(lane == h, norm(qh), q_norms)
        k_norms = jnp.where(lane == h, norm(kh), k_norms)
    qno_ref[...] = q_norms.T[0:ATT_HEADS, :]
    kno_ref[...] = k_norms.T[0:ATT_HEADS, :]
    low = lane < IDX_DIM
    for p in range(IDX_HEADS // 2):
        x = _rope(iq_ref[:, p * LANES:(p + 1) * LANES].astype(F32), tab_idx)
        iqo_ref[:, (2 * p) * LANES:(2 * p + 1) * LANES] = jnp.where(low, x, 0.0).astype(BF16)
        iqo_ref[:, (2 * p + 1) * LANES:(2 * p + 2) * LANES] = jnp.where(low, pltpu.roll(x, IDX_DIM, 1), 0.0).astype(BF16)
    sm = sm_ref[...]
    kx = jnp.where(low, 0.0, sm)
    kx = kx * lax.rsqrt(jnp.sum(kx * kx, axis=-1, keepdims=True) * (1.0 / IDX_DIM) + EPS) * kn_ref[...]
    kx = _rope(kx, tab_idx)
    iko_ref[...] = jnp.where(low, pltpu.roll(kx, IDX_DIM, 1), 0.0).astype(BF16)
    wto_ref[...] = (sm * (IDX_HEADS ** -0.5 * IDX_DIM ** -0.5)).T[SM_IXW:SM_IXW + IDX_HEADS, :]


def _dsa_prep_call(proj, small, pos_col, kn_lanes):
    L = proj.shape[0]
    tT = min(256, L)
    na = ATT_HEADS * ATT_DIM
    ni = IDX_HEADS * IDX_DIM
    return pl.pallas_call(
        _dsa_prep_kernel,
        grid=(L // tT,),
        in_specs=[
            pl.BlockSpec((tT, na), lambda i: (i, COL_ATQ // na)),
            pl.BlockSpec((tT, na), lambda i: (i, COL_ATK // na)),
            pl.BlockSpec((tT, na), lambda i: (i, COL_ATV // na)),
            pl.BlockSpec((tT, ni), lambda i: (i, COL_IXQ // ni)),
            pl.BlockSpec((tT, LANES), lambda i: (i, 0)),
            pl.BlockSpec((tT, 1), lambda i: (i, 0)),
            pl.BlockSpec((1, LANES), lambda i: (0, 0)),
        ],
        out_specs=[
            pl.BlockSpec((tT, na), lambda i: (i, 0)),
            pl.BlockSpec((tT, na), lambda i: (i, 0)),
            pl.BlockSpec((na, tT), lambda i: (0, i)),
            pl.BlockSpec((tT, IDX_HEADS * LANES), lambda i: (i, 0)),
            pl.BlockSpec((tT, LANES), lambda i: (i, 0)),
            pl.BlockSpec((IDX_HEADS, tT), lambda i: (0, i)),
            pl.BlockSpec((ATT_HEADS, tT), lambda i: (0, i)),
            pl.BlockSpec((ATT_HEADS, tT), lambda i: (0, i)),
        ],
        out_shape=[jax.ShapeDtypeStruct((L, na), BF16), jax.ShapeDtypeStruct((L, na), BF16),
                   jax.ShapeDtypeStruct((na, L), BF16),
                   jax.ShapeDtypeStruct((L, IDX_HEADS * LANES), BF16), jax.ShapeDtypeStruct((L, LANES), BF16),
                   jax.ShapeDtypeStruct((IDX_HEADS, L), F32), jax.ShapeDtypeStruct((ATT_HEADS, L), F32),
                   jax.ShapeDtypeStruct((ATT_HEADS, L), F32)],
        compiler_params=_params(("parallel",)),
        name="dsa_prep",
    )(proj, proj, proj, proj, small, pos_col, kn_lanes)


def _index_scores_t(ik_blk, iq_heads, w_rows):
    acc = None
    for qh, wh in zip(iq_heads, w_rows):
        term = wh * jnp.maximum(_dot_nt(ik_blk, qh), 0.0)
        acc = term if acc is None else acc + term
    return acc


def _sortable(bits):
    return jnp.where(bits < 0, bits ^ jnp.int32(0x7FFFFFFF), bits)


def _score_keys(s):
    key = _sortable(pltpu.bitcast(s, I32))
    return jnp.where(key == -1, 0, key)


DSA_TQ = 256
DSA_TK = 512
SUM_ROWS = 16
KEY_NEG_INF = int(np.int32(np.array(-np.inf, np.float32).view(np.int32)) ^ np.int32(0x7FFFFFFF))
INT_MIN = -(2 ** 31)
INT_MAX = 2 ** 31 - 1


def _dsa_kernel(q_ref, iq_ref, wt_ref, qn_ref, ik_ref, kn_ref, k_hbm, vt_hbm, o_ref,
                sc_ref, hi_ref, kbuf, vbuf, sem, tie_ref, bias_ref, *acc_refs, topk, tk):
    TQ = q_ref.shape[0]
    L = ik_ref.shape[0]
    t0 = pl.program_id(0) * TQ
    nkt = (t0 + TQ + tk - 1) // tk

    def kv_copies(kt, slot):
        off = pl.multiple_of(kt * tk, tk)
        return (pltpu.make_async_copy(k_hbm.at[pl.ds(off, tk)], kbuf.at[slot], sem.at[0, slot]),
                pltpu.make_async_copy(vt_hbm.at[:, pl.ds(off, tk)], vbuf.at[slot], sem.at[1, slot]))

    for cp in kv_copies(0, 0):
        cp.start()

    iq_heads = [iq_ref[:, h * LANES:(h + 1) * LANES] for h in range(IDX_HEADS)]
    w_rows = [wt_ref[h:h + 1, :] for h in range(IDX_HEADS)]
    qpos = t0 + lax.broadcasted_iota(I32, (tk, TQ), 1)
    krow = lax.broadcasted_iota(I32, (tk, TQ), 0)

    def fill(kt, carry, on_diagonal):
        off = pl.multiple_of(kt * tk, tk)
        s = _index_scores_t(ik_ref[pl.ds(off, tk), :], iq_heads, w_rows)
        if on_diagonal:
            s = jnp.where(krow + off <= qpos, s, -jnp.inf)
        keys = _score_keys(s)
        sc_ref[pl.ds(off, tk), :] = keys
        hi_ref[pl.ds(off, tk), :] = (keys >> 16).astype(I16)
        return carry

    n_below = (t0 + 1) // tk
    lax.fori_loop(0, n_below, functools.partial(fill, on_diagonal=False), 0)
    lax.fori_loop(n_below, nkt, functools.partial(fill, on_diagonal=True), 0)

    def count(pred):
        def body(kt, acc):
            off = pl.multiple_of(kt * tk, tk)
            m = pred(sc_ref[pl.ds(off, tk), :], krow + off).astype(I32)
            return acc + jnp.sum(m.reshape(tk // 32, 32, TQ), axis=0)
        acc = lax.fori_loop(0, nkt, body, jnp.zeros((32, TQ), I32))
        return jnp.sum(acc, axis=0, keepdims=True)

    def count_hi(cand):
        def body(kt, acc):
            off = pl.multiple_of(kt * tk, tk)
            m = (hi_ref[pl.ds(off, tk), :] >= cand).astype(I16)
            for g in range(tk // 32):
                acc = acc + m[g * 32:(g + 1) * 32]
            return acc
        acc = lax.fori_loop(0, nkt, body, jnp.zeros((32, TQ), I16))
        return jnp.sum(acc.astype(I32), axis=0, keepdims=True)

    def hi_body(i, st):
        u, cnt = st
        uc = u | (jnp.int32(2 ** 15) >> i)
        c = count_hi((uc - 2 ** 15).astype(I16))
        ok = c >= topk
        return jnp.where(ok, uc, u), jnp.where(ok, c, cnt)

    cnt_all = jnp.zeros((1, TQ), I32) + nkt * tk
    u, cnt = lax.fori_loop(0, 16, hi_body, (jnp.zeros((1, TQ), I32), cnt_all))
    base = (u - 2 ** 15) << 16

    theta_hi = (u - 2 ** 15).astype(I16)
    above_hi = jnp.where(u >= 2 ** 16 - 1, 0, count_hi((jnp.minimum(u, 2 ** 16 - 2) + 1 - 2 ** 15).astype(I16)))

    def low_halves(kt, carry):
        off = pl.multiple_of(kt * tk, tk)
        low = ((sc_ref[pl.ds(off, tk), :] & 0xFFFF) - 2 ** 15).astype(I16)
        hi_ref[pl.ds(off, tk), :] = jnp.where(hi_ref[pl.ds(off, tk), :] == theta_hi, low, jnp.int16(-2 ** 15))
        return carry

    lax.fori_loop(0, nkt, low_halves, 0)

    def lo_body(i, st):
        delta, cnt = st
        dc = delta | (jnp.int32(2 ** 15) >> i)
        c = above_hi + count_hi((dc - 2 ** 15).astype(I16))
        ok = c >= topk
        return jnp.where(ok, dc, delta), jnp.where(ok, c, cnt)

    LO_GROUP = 4

    def group_cond(st):
        g, _, cnt = st
        return (g < 16 // LO_GROUP) & (jnp.max(jnp.abs(cnt - topk)) > 0)

    def group_body(st):
        g, delta, cnt = st
        delta, cnt = lax.fori_loop(0, LO_GROUP, lambda j, s: lo_body(g * LO_GROUP + j, s), (delta, cnt))
        return g + 1, delta, cnt

    _, delta, cnt = lax.while_loop(group_cond, group_body, (jnp.int32(0), jnp.zeros((1, TQ), I32), cnt))
    theta = base + delta

    none_valid = theta <= KEY_NEG_INF
    tied = (cnt > topk) & jnp.logical_not(none_valid)
    any_tied = jnp.max(tied.astype(I32)) > 0
    tie_ref[0:1, :] = jnp.where(none_valid, 0, INT_MAX)
    tie_ref[1:2, :] = jnp.zeros((1, TQ), I32)

    @pl.when(any_tied)
    def _():
        above = count(lambda keys, _: keys > theta)
        tie_ref[0:1, :] = jnp.where(tied, topk - above, tie_ref[0:1, :])

    need = tie_ref[0:1, :]
    theta_keep = jnp.where(need > 0, theta, theta + 1)

    ones_rows = jnp.ones((SUM_ROWS, tk), BF16)
    heads = [slice(h * ATT_DIM, (h + 1) * ATT_DIM) for h in range(ATT_HEADS)]

    def sweep(step, init):
        for acc_ref in acc_refs:
            acc_ref[...] = jnp.zeros_like(acc_ref)
        tie_ref[1:2, :] = jnp.zeros((1, TQ), I32)

        def body(kt, carry):
            slot = kt % 2
            off = pl.multiple_of(kt * tk, tk)
            for cp in kv_copies(kt, slot):
                cp.wait()

            @pl.when(kt + 1 < nkt)
            def _():
                for cp in kv_copies(kt + 1, 1 - slot):
                    cp.start()

            keys = sc_ref[pl.ds(off, tk), :]

            @pl.when(jnp.logical_not(any_tied))
            def _():
                bias_ref[...] = jnp.where(keys >= theta_keep, 0.0, NEG_BIG)

            @pl.when(any_tied)
            def _():
                tie = keys == theta
                r = lax.broadcasted_iota(I32, (tk, tk), 0)
                c = lax.broadcasted_iota(I32, (tk, tk), 1)
                earlier = _dot((r > c).astype(F32).astype(BF16), jnp.where(tie, 1.0, 0.0).astype(BF16))
                rank = earlier + tie_ref[1:2, :].astype(F32)
                keep = (keys > theta) | (tie & (rank < need.astype(F32)))
                bias_ref[...] = jnp.where(keep, 0.0, NEG_BIG)
                tie_ref[1:2, :] += jnp.sum(tie.astype(I32), axis=0, keepdims=True)

            return step(slot, carry)

        return lax.fori_loop(0, nkt, body, init)

    def v_ext(slot, h):
        return jnp.concatenate([vbuf[slot, heads[h], :], ones_rows], axis=0)

    shift = qn_ref[...] * jnp.max(kn_ref[...], axis=1, keepdims=True)

    def fixed_shift_step(slot, carry):
        probs = [jnp.exp2(_dot_nt(kbuf[slot, :, hs], q_ref[:, hs]) + bias_ref[...] - shift[h:h + 1, :]).astype(BF16)
                 for h, hs in enumerate(heads)]
        for h, acc_ref in enumerate(acc_refs):
            acc_ref[...] += _dot(v_ext(slot, h), probs[h])
        return carry

    sweep(fixed_shift_step, 0)
    norm_min = functools.reduce(jnp.minimum, [acc_ref[ATT_DIM:ATT_DIM + 1, :] for acc_ref in acc_refs])

    @pl.when(jnp.min(norm_min) < 2.0 ** -80)
    def _():
        def running_max_step(slot, m_run):
            logits, tile_max = [], []
            for hs in heads:
                s = _dot_nt(kbuf[slot, :, hs], q_ref[:, hs]) + bias_ref[...]
                logits.append(s)
                tile_max.append(jnp.max(s, axis=0, keepdims=True))
            m_rows = []
            for h, acc_ref in enumerate(acc_refs):
                m_old = m_run[h:h + 1, :]
                m_new = jnp.maximum(m_old, tile_max[h])
                p = jnp.exp2(logits[h] - m_new).astype(BF16)
                acc_ref[...] = jnp.exp2(m_old - m_new) * acc_ref[...] + _dot(v_ext(slot, h), p)
                m_rows.append(m_new)
            return jnp.concatenate(m_rows, axis=0)

        for cp in kv_copies(0, 0):
            cp.start()
        sweep(running_max_step, jnp.full((ATT_HEADS, TQ), NEG_BIG, F32))

    for h, acc_ref in enumerate(acc_refs):
        out_t = acc_ref[0:ATT_DIM, :] / acc_ref[ATT_DIM:ATT_DIM + 1, :]
        o_ref[:, h * ATT_DIM:(h + 1) * ATT_DIM] = out_t.T.astype(BF16)


def _dsa_call(q, k, vt, iq, ik, wt, qn, kn, topk):
    L, na = q.shape
    TQ = min(DSA_TQ, L)
    tk = min(DSA_TK, L)
    assert topk <= tk
    return pl.pallas_call(
        functools.partial(_dsa_kernel, topk=topk, tk=tk),
        grid=(L // TQ,),
        in_specs=[
            pl.BlockSpec((TQ, na), lambda i: (i, 0)),
            pl.BlockSpec((TQ, IDX_HEADS * LANES), lambda i: (i, 0)),
            pl.BlockSpec((IDX_HEADS, TQ), lambda i: (0, i)),
            pl.BlockSpec((ATT_HEADS, TQ), lambda i: (0, i)),
            pl.BlockSpec((L, LANES), lambda i: (0, 0)),
            pl.BlockSpec((ATT_HEADS, L), lambda i: (0, 0)),
            pl.BlockSpec(memory_space=pl.ANY),
            pl.BlockSpec(memory_space=pl.ANY),
        ],
        out_specs=pl.BlockSpec((TQ, na), lambda i: (i, 0)),
        out_shape=jax.ShapeDtypeStruct((L, na), BF16),
        scratch_shapes=[
            pltpu.VMEM((L, TQ), I32),
            pltpu.VMEM((L, TQ), I16),
            pltpu.VMEM((2, tk, na), BF16),
            pltpu.VMEM((2, na, tk), BF16),
            pltpu.SemaphoreType.DMA((2, 2)),
            pltpu.VMEM((8, TQ), I32),
            pltpu.VMEM((tk, TQ), F32),
        ] + [pltpu.VMEM((ATT_DIM + SUM_ROWS, TQ), F32)] * ATT_HEADS,
        compiler_params=_params(("parallel",), vmem_mb=56),
        name="dsa",
    )(q, iq, wt, qn, ik, kn, k, vt)


def _merge_kernel(g0_ref, g1_ref, b0_ref, b1_ref, ydn_ref, yat_ref, wdn_ref, wat_ref, o_ref):
    gate0 = _sigmoid(g0_ref[...].astype(F32) + b0_ref[...])
    gate1 = _sigmoid(g1_ref[...].astype(F32) + b1_ref[...])
    merged = gate0 * _dot(ydn_ref[...], wdn_ref[...]) + gate1 * _dot(yat_ref[...], wat_ref[...])
    o_ref[...] = merged.astype(o_ref.dtype)


def _merge_call(proj, b_gates, y_dn, y_at, w_dn, w_at):
    L = proj.shape[0]
    tm = min(1024, L)
    tn = 512
    nj = D_MODEL // tn
    c0 = COL_GATE // tn
    kd = y_dn.shape[1]
    return pl.pallas_call(
        _merge_kernel,
        grid=(L // tm, nj),
        in_specs=[
            pl.BlockSpec((tm, tn), lambda i, j: (i, c0 + j)),
            pl.BlockSpec((tm, tn), lambda i, j: (i, c0 + nj + j)),
            pl.BlockSpec((1, tn), lambda i, j: (0, j)),
            pl.BlockSpec((1, tn), lambda i, j: (0, nj + j)),
            pl.BlockSpec((tm, kd), lambda i, j: (i, 0)),
            pl.BlockSpec((tm, kd), lambda i, j: (i, 0)),
            pl.BlockSpec((kd, tn), lambda i, j: (0, j)),
            pl.BlockSpec((kd, tn), lambda i, j: (0, j)),
        ],
        out_specs=pl.BlockSpec((tm, tn), lambda i, j: (i, j)),
        out_shape=jax.ShapeDtypeStruct((L, D_MODEL), BF16),
        compiler_params=_params(("parallel", "arbitrary")),
        name="merge",
    )(proj, proj, b_gates, b_gates, y_dn, y_at, w_dn, w_at)


RT_E0, RT_E1, RT_W0, RT_W1 = 0, 1, 2, 3


def _first_lane_of_max(v, lane):
    m = jnp.max(v, axis=-1, keepdims=True)
    return m, jnp.min(jnp.where(v == m, lane, LANES), axis=-1, keepdims=True)


def _outproj_kernel(x_ref, mg_ref, wo_ref, nf_ref, wr_ref, br_ref, x1_ref, h2_ref, rt_ref):
    x1 = x_ref[...] + _dot(mg_ref[...], wo_ref[...])
    x1_ref[...] = x1
    h2 = x1 * lax.rsqrt(jnp.mean(x1 * x1, axis=-1, keepdims=True) + EPS) * nf_ref[...]
    h2_ref[...] = h2
    lg = _dot(h2.astype(BF16), wr_ref[...]) + br_ref[...]
    lane = lax.broadcasted_iota(I32, (1, LANES), 1)
    ninf = -jnp.inf
    gl = jnp.where(lane < N_GROUPS, lg, ninf)
    gmax, g_sel = _first_lane_of_max(gl, lane)
    p_group = 1.0 / jnp.sum(jnp.exp(gl - gmax), axis=-1, keepdims=True)
    ex = lane - N_GROUPS
    in_group = (ex >= 0) & (ex < N_EXPERTS) & ((ex // EXPERTS_PER_GROUP) == g_sel)
    el = jnp.where(in_group, lg, ninf)
    m1, i1 = _first_lane_of_max(el, lane)
    m2, i2 = _first_lane_of_max(jnp.where(lane == i1, ninf, el), lane)
    e2 = jnp.exp(m2 - m1)
    w0 = p_group / (1.0 + e2)
    w1 = p_group * e2 / (1.0 + e2)
    rec = jnp.where(lane == RT_E0, (i1 - N_GROUPS).astype(F32), 0.0)
    rec = jnp.where(lane == RT_E1, (i2 - N_GROUPS).astype(F32), rec)
    rec = jnp.where(lane == RT_W0, w0, rec)
    rt_ref[...] = jnp.where(lane == RT_W1, w1, rec)


def _outproj_call(x2, merged, w_out, norm_ffn, w_route, b_route):
    L = x2.shape[0]
    tm = min(256, L)
    row = lambda i: (i, 0)
    fixed = lambda i: (0, 0)
    return pl.pallas_call(
        _outproj_kernel,
        grid=(L // tm,),
        in_specs=[
            pl.BlockSpec((tm, D_MODEL), row),
            pl.BlockSpec((tm, D_MODEL), row),
            pl.BlockSpec((D_MODEL, D_MODEL), fixed),
            pl.BlockSpec((1, D_MODEL), fixed),
            pl.BlockSpec((D_MODEL, LANES), fixed),
            pl.BlockSpec((1, LANES), fixed),
        ],
        out_specs=[pl.BlockSpec((tm, D_MODEL), row), pl.BlockSpec((tm, D_MODEL), row), pl.BlockSpec((tm, LANES), row)],
        out_shape=[jax.ShapeDtypeStruct((L, D_MODEL), F32), jax.ShapeDtypeStruct((L, D_MODEL), F32),
                   jax.ShapeDtypeStruct((L, LANES), F32)],
        compiler_params=_params(("parallel",)),
        name="outproj_route",
    )(x2, merged, w_out, norm_ffn, w_route, b_route)


def _moe_slots(L):
    return -(-(2 * L + N_EXPERTS * (MOE_ROWS - 1)) // MOE_ROWS) * MOE_ROWS


def _moe_plan_kernel(rt_ref, dest_ref, blk_ref, cnt_ref, carry_ref, start_ref):
    phase, i = pl.program_id(0), pl.program_id(1)
    tR = rt_ref.shape[0]
    lane = lax.broadcasted_iota(I32, (1, LANES), 1)
    lane_f = lane.astype(F32)
    rt = rt_ref[...]
    e0, e1 = rt[:, RT_E0:RT_E0 + 1], rt[:, RT_E1:RT_E1 + 1]
    hit0, hit1 = lane_f == e0, lane_f == e1
    onehot = (hit0 | hit1).astype(F32)
    colsum = jnp.sum(onehot, axis=0, keepdims=True)

    @pl.when((phase == 0) & (i == 0))
    def _():
        cnt_ref[...] = jnp.zeros_like(cnt_ref)

    @pl.when(phase == 0)
    def _():
        cnt_ref[...] += colsum

    @pl.when((phase == 1) & (i == 0))
    def _():
        carry_ref[...] = jnp.zeros_like(carry_ref)
        padded = jnp.floor((cnt_ref[...] + (MOE_ROWS - 1)) * (1.0 / MOE_ROWS)) * MOE_ROWS
        r = lax.broadcasted_iota(I32, (LANES, LANES), 0)
        c = lax.broadcasted_iota(I32, (LANES, LANES), 1)
        upper = (r <= c).astype(F32)
        end = jnp.dot(jnp.broadcast_to(padded, (8, LANES)), upper, preferred_element_type=F32,
                      precision=lax.Precision.HIGHEST)[0:1, :]
        start_ref[...] = end - padded
        n_used = end[:, N_EXPERTS - 1:N_EXPERTS] * (1.0 / MOE_ROWS)
        nb = blk_ref.shape[0] - 8
        b = lax.broadcasted_iota(I32, (nb, 1), 0).astype(F32)
        b_eff = jnp.minimum(b, n_used - 1.0)
        done = ((end <= b_eff * MOE_ROWS) & (lane < N_EXPERTS)).astype(F32)
        blk_e = jnp.minimum(jnp.sum(done, axis=-1, keepdims=True), N_EXPERTS - 1.0)
        blk_ref[0:nb, :] = jnp.where(lane == 0, blk_e, jnp.where(lane == 1, n_used, 0.0)).astype(I32)
        row = lax.broadcasted_iota(I32, (8, LANES), 0)
        seg = jnp.where(row == 0, end - padded, jnp.where(row == 1, cnt_ref[...], jnp.where(row == 2, padded, 0.0)))
        blk_ref[nb:nb + 8, :] = seg.astype(I32)

    @pl.when(phase == 1)
    def _():
        r = lax.broadcasted_iota(I32, (tR, tR), 0)
        c = lax.broadcasted_iota(I32, (tR, tR), 1)
        before = _dot((r > c).astype(BF16), onehot.astype(BF16)) + carry_ref[...]
        slot = before + start_ref[...]
        d0 = jnp.sum(jnp.where(hit0, slot, 0.0), axis=-1, keepdims=True)
        d1 = jnp.sum(jnp.where(hit1, slot, 0.0), axis=-1, keepdims=True)
        dest_ref[...] = jnp.where(lane == 0, d0, jnp.where(lane == 1, d1, 0.0)).astype(I32)
        carry_ref[...] += colsum


def _moe_plan_call(route):
    L = route.shape[0]
    tR = min(256, L)
    nb = _moe_slots(L) // MOE_ROWS
    nb_pad = -(-nb // 8) * 8
    return pl.pallas_call(
        _moe_plan_kernel,
        grid=(2, L // tR),
        in_specs=[pl.BlockSpec((tR, LANES), lambda p, i: (i, 0))],
        out_specs=[pl.BlockSpec((tR, LANES), lambda p, i: (p * i, 0)),
                   pl.BlockSpec((nb_pad + 8, LANES), lambda p, i: (0, 0))],
        out_shape=[jax.ShapeDtypeStruct((L, LANES), I32), jax.ShapeDtypeStruct((nb_pad + 8, LANES), I32)],
        scratch_shapes=[pltpu.VMEM((1, LANES), F32)] * 3,
        compiler_params=_params(("arbitrary", "arbitrary")),
        name="moe_plan",
    )(route)


PLAN_START, PLAN_COUNT, PLAN_PADDED, PLAN_USED = 0, N_EXPERTS, 2 * N_EXPERTS, 3 * N_EXPERTS
PAD_PIECES = tuple(MOE_ROWS >> s for s in range(1, MOE_ROWS.bit_length() - 3))


def _moe_scatter_kernel(dest_ref, plan_ref, h_ref, xs_out, zero_ref, sem, zero_sem, *, rows, nb):
    base = pl.program_id(0) * rows

    def row_copy(r, j):
        return pltpu.make_async_copy(h_ref.at[pl.ds(r, 1)], xs_out.at[pl.ds(dest_ref[2 * (base + r) + j], 1)], sem)

    def start(r, c):
        row_copy(r, 0).start()
        row_copy(r, 1).start()
        return c

    lax.fori_loop(0, rows, start, 0)

    @pl.when(pl.program_id(0) == 0)
    def _():
        zero_ref[...] = jnp.zeros_like(zero_ref)

        def for_each_pad_piece(act):
            def per_expert(e, c):
                count = plan_ref[PLAN_COUNT + e]
                pad = plan_ref[PLAN_PADDED + e] - count
                first = plan_ref[PLAN_START + e] + count
                end = first + pad

                def single_row(r, cc):
                    act(pltpu.make_async_copy(zero_ref.at[pl.ds(0, 1)], xs_out.at[pl.ds(first + r, 1)], zero_sem))
                    return cc

                lax.fori_loop(0, pad & 7, single_row, 0)
                for k in PAD_PIECES:
                    @pl.when((pad & k) != 0)
                    def _():
                        at = pl.multiple_of(end - (pad & ~(k - 1)), 8)
                        act(pltpu.make_async_copy(zero_ref.at[pl.ds(0, k)], xs_out.at[pl.ds(at, k)], zero_sem))
                return c
            lax.fori_loop(0, N_EXPERTS, per_expert, 0)

        def for_each_unused_block(act):
            def per_block(b, c):
                act(pltpu.make_async_copy(zero_ref, xs_out.at[pl.ds(b * MOE_ROWS, MOE_ROWS)], zero_sem))
                return c
            lax.fori_loop(plan_ref[PLAN_USED], nb, per_block, 0)

        for_each_pad_piece(lambda cp: cp.start())
        for_each_unused_block(lambda cp: cp.start())
        for_each_pad_piece(lambda cp: cp.wait())
        for_each_unused_block(lambda cp: cp.wait())

    for _ in range(2):
        pltpu.make_async_copy(h_ref, xs_out.at[pl.ds(0, rows)], sem).wait()


def _moe_scatter_call(dest_flat, plan, h2):
    L = h2.shape[0]
    rows = min(256, L)
    slots = _moe_slots(L)
    return pl.pallas_call(
        functools.partial(_moe_scatter_kernel, rows=rows, nb=slots // MOE_ROWS),
        grid_spec=pltpu.PrefetchScalarGridSpec(
            num_scalar_prefetch=2,
            grid=(L // rows,),
            in_specs=[pl.BlockSpec((rows, D_MODEL), lambda i, d, p: (i, 0))],
            out_specs=pl.BlockSpec(memory_space=pl.ANY),
            scratch_shapes=[pltpu.VMEM((MOE_ROWS, D_MODEL), F32), pltpu.SemaphoreType.DMA(()),
                            pltpu.SemaphoreType.DMA(())],
        ),
        out_shape=jax.ShapeDtypeStruct((slots, D_MODEL), F32),
        compiler_params=_params(("arbitrary",)),
        name="moe_scatter",
    )(dest_flat, plan, h2)


def _moe_ffn_kernel(be_ref, x_ref, wg_ref, wu_ref, wd_ref, y_ref, *, nb):
    b = pl.program_id(0)

    @pl.when(b < be_ref[nb])
    def _():
        xb = x_ref[...].astype(BF16)
        gate = _dot(xb, wg_ref[...].astype(BF16))
        up = _dot(xb, wu_ref[...].astype(BF16))
        hidden = (gate * _sigmoid(gate) * up).astype(BF16)
        y_ref[...] = _dot(hidden, wd_ref[...].astype(BF16))

    @pl.when(b >= be_ref[nb])
    def _():
        y_ref[...] = jnp.zeros_like(y_ref)


def _moe_ffn_call(blk_e, xs, w_gate, w_up, w_down):
    P = xs.shape[0]
    nb = P // MOE_ROWS
    return pl.pallas_call(
        functools.partial(_moe_ffn_kernel, nb=nb),
        grid_spec=pltpu.PrefetchScalarGridSpec(
            num_scalar_prefetch=1,
            grid=(nb,),
            in_specs=[
                pl.BlockSpec((MOE_ROWS, D_MODEL), lambda b, be: (b, 0)),
                pl.BlockSpec((None, D_MODEL, EXPERT_HIDDEN), lambda b, be: (be[b], 0, 0)),
                pl.BlockSpec((None, D_MODEL, EXPERT_HIDDEN), lambda b, be: (be[b], 0, 0)),
                pl.BlockSpec((None, EXPERT_HIDDEN, D_MODEL), lambda b, be: (be[b], 0, 0)),
            ],
            out_specs=pl.BlockSpec((MOE_ROWS, D_MODEL), lambda b, be: (b, 0)),
        ),
        out_shape=jax.ShapeDtypeStruct((P, D_MODEL), F32),
        compiler_params=_params(("arbitrary",), vmem_mb=56),
        name="moe_ffn",
    )(blk_e, xs, w_gate, w_up, w_down)


def _moe_combine_kernel(dest_ref, x1_ref, rt_ref, nf_ref, ys_hbm, o_ref, ya_ref, yb_ref, sem, *, rows):
    step, n_steps = pl.program_id(0), pl.num_programs(0)

    def gather(s, act):
        slot = s % 2

        def body(r, c):
            for j, dst in enumerate((ya_ref, yb_ref)):
                src = ys_hbm.at[pl.ds(dest_ref[2 * (s * rows + r) + j], 1)]
                act(pltpu.make_async_copy(src, dst.at[slot, pl.ds(r, 1)], sem.at[slot]))
            return c

        lax.fori_loop(0, rows, body, 0)

    @pl.when(step == 0)
    def _():
        gather(step, lambda cp: cp.start())

    @pl.when(step + 1 < n_steps)
    def _():
        gather(step + 1, lambda cp: cp.start())

    slot = step % 2
    for dst in (ya_ref, yb_ref):
        pltpu.make_async_copy(ys_hbm.at[pl.ds(0, rows)], dst.at[slot], sem.at[slot]).wait()
    rt = rt_ref[...]
    x = x1_ref[...] + rt[:, RT_W0:RT_W0 + 1] * ya_ref[slot] + rt[:, RT_W1:RT_W1 + 1] * yb_ref[slot]
    o_ref[...] = x * lax.rsqrt(jnp.mean(x * x, axis=-1, keepdims=True) + EPS) * nf_ref[...]


def _moe_combine_call(dest_flat, x1, route, norm_final, ys):
    L = x1.shape[0]
    rows = min(256, L)
    return pl.pallas_call(
        functools.partial(_moe_combine_kernel, rows=rows),
        grid_spec=pltpu.PrefetchScalarGridSpec(
            num_scalar_prefetch=1,
            grid=(L // rows,),
            in_specs=[
                pl.BlockSpec((rows, D_MODEL), lambda i, d: (i, 0)),
                pl.BlockSpec((rows, LANES), lambda i, d: (i, 0)),
                pl.BlockSpec((1, D_MODEL), lambda i, d: (0, 0)),
                pl.BlockSpec(memory_space=pl.ANY),
            ],
            out_specs=pl.BlockSpec((rows, D_MODEL), lambda i, d: (i, 0)),
            scratch_shapes=[pltpu.VMEM((2, rows, D_MODEL), F32), pltpu.VMEM((2, rows, D_MODEL), F32),
                            pltpu.SemaphoreType.DMA((2,))],
        ),
        out_shape=jax.ShapeDtypeStruct((L, D_MODEL), F32),
        compiler_params=_params(("arbitrary",)),
        name="moe_combine",
    )(dest_flat, x1, route, norm_final, ys)


def _pack_w_in(w):
    s = np.cumsum((0, QKV_COLS, DN_HEADS * DN_DV, DN_HEADS, DN_HEADS, ATT_HEADS * ATT_DIM, ATT_HEADS * ATT_DIM,
                   ATT_HEADS * ATT_DIM, IDX_HEADS * IDX_DIM, IDX_DIM, IDX_HEADS, 2 * D_MODEL))
    seg = lambda n: w[:, int(s[n]):int(s[n + 1])]
    w_main = jnp.concatenate([seg(0), seg(1), seg(4), seg(5), seg(6), seg(7), seg(10)], axis=1).astype(BF16)
    pad = jnp.zeros((w.shape[0], SM_IXK - SM_IXW - IDX_HEADS), w.dtype)
    w_small = jnp.concatenate([seg(2), seg(3), seg(9), pad, seg(8)], axis=1).astype(BF16)
    return w_main, w_small


def _forward(x, positions, norm_mix, w_in, b_gates, dn_conv_w, dn_a_log, dn_dt_bias, dn_norm_w, idx_k_norm,
             w_proj_dn, w_proj_att, w_out, norm_ffn, w_group, b_group, w_router, b_router, w_exp_gate, w_exp_up,
             w_exp_down, norm_final):
    st = {}
    L = x.shape[1]
    x2 = x.reshape(L, D_MODEL)
    w_main, w_small = _pack_w_in(w_in[0])
    proj, small = _proj_call(x2, norm_mix[0].reshape(1, D_MODEL), w_main, w_small)
    st["proj"], st["small"] = proj, small
    q, k, v, g, beta = _gdn_prep_call(proj, small, dn_conv_w[0], dn_a_log[0].reshape(1, DN_HEADS),
                                      dn_dt_bias[0].reshape(1, DN_HEADS))
    st["y_dn"] = _gdn_chunk_call(q, k, v, proj, g, g.T, beta, dn_norm_w[0].reshape(1, DN_DV))

    pos_col = positions.reshape(L, 1).astype(F32)
    kn_lanes = jnp.concatenate([jnp.zeros((SM_IXK,), F32), idx_k_norm[0].astype(F32)]).reshape(1, LANES)
    aq, ak, avt, iq, ik, wt, qn, kn = _dsa_prep_call(proj, small, pos_col, kn_lanes)
    st["y_at"] = _dsa_call(aq, ak, avt, iq, ik, wt, qn, kn, min(TOPK_MAX, L // 4))

    merged = _merge_call(proj, b_gates[0].reshape(1, 2 * D_MODEL), st["y_dn"], st["y_at"],
                         w_proj_dn[0].astype(BF16), w_proj_att[0].astype(BF16))
    st["merged"] = merged
    n_route = N_GROUPS + N_EXPERTS
    w_route = jnp.concatenate([w_group[0], w_router[0], jnp.zeros((D_MODEL, LANES - n_route), F32)], axis=1).astype(BF16)
    b_route = jnp.concatenate([b_group[0], b_router[0], jnp.zeros((LANES - n_route,), F32)]).reshape(1, LANES)
    x1, h2, route = _outproj_call(x2, merged, w_out[0].astype(BF16), norm_ffn[0].reshape(1, D_MODEL), w_route, b_route)
    st["x1"], st["h2"], st["route"] = x1, h2, route

    dest, blk = _moe_plan_call(route)
    dest_flat = dest[:, :2].reshape(2 * L)
    slots = _moe_slots(L)
    nb = slots // MOE_ROWS
    blk_e = jnp.concatenate([blk[:nb, 0], blk[0:1, 1]])
    seg = blk[blk.shape[0] - 8:blk.shape[0] - 5, :N_EXPERTS]
    plan = jnp.concatenate([seg.reshape(3 * N_EXPERTS), blk[0:1, 1]])
    xs = _moe_scatter_call(dest_flat, plan, h2)
    ys = _moe_ffn_call(blk_e, xs, w_exp_gate[0], w_exp_up[0], w_exp_down[0])
    out = _moe_combine_call(dest_flat, x1, route, norm_final.reshape(1, D_MODEL), ys)
    st["out"] = out.reshape(1, L, D_MODEL)
    return st


def kernel(x, positions, norm_mix, w_in, b_gates, dn_conv_w, dn_a_log, dn_dt_bias, dn_norm_w, idx_k_norm, w_proj_dn,
           w_proj_att, w_out, norm_ffn, w_group, b_group, w_router, b_router, w_exp_gate, w_exp_up, w_exp_down,
           norm_final):
    return _forward(x, positions, norm_mix, w_in, b_gates, dn_conv_w, dn_a_log, dn_dt_bias, dn_norm_w, idx_k_norm,
                    w_proj_dn, w_proj_att, w_out, norm_ffn, w_group, b_group, w_router, b_router, w_exp_gate,
                    w_exp_up, w_exp_down, norm_final)["out"]


def _stages(d, upto=None):
    return _forward(*[d[n] for n in ("x", "positions", "norm_mix", "w_in", "b_gates", "dn_conv_w", "dn_a_log",
                                     "dn_dt_bias", "dn_norm_w", "idx_k_norm", "w_proj_dn", "w_proj_att", "w_out",
                                     "norm_ffn", "w_group", "b_group", "w_router", "b_router", "w_exp_gate",
                                     "w_exp_up", "w_exp_down", "norm_final")])
```

```python
import functools

import jax
import jax.numpy as jnp
import numpy as np
from jax import lax
from jax.experimental import pallas as pl
from jax.experimental.pallas import tpu as pltpu

D_MODEL = 2048
DN_HEADS = 8
DN_DK = 128
DN_DV = 128
DN_CONV = 4
DN_CHUNK = 64
ATT_HEADS = 8
ATT_DIM = 128
IDX_HEADS = 8
IDX_DIM = 64
TOPK_MAX = 256
ROPE_THETA = 500000.0
ROPE_FRACTION = 4
N_GROUPS = 8
EXPERTS_PER_GROUP = 8
N_EXPERTS = N_GROUPS * EXPERTS_PER_GROUP
EXPERT_HIDDEN = 512
EPS = 1e-6

LANES = 128
MOE_ROWS = 256
NEG_BIG = -1e30
LOG2_E = 1.4426950408889634

F32 = jnp.float32
BF16 = jnp.bfloat16
I32 = jnp.int32
I16 = jnp.int16

QKV_COLS = 2 * DN_HEADS * DN_DK + DN_HEADS * DN_DV
COL_QKV = 0
COL_Z = COL_QKV + QKV_COLS
COL_ATQ = COL_Z + DN_HEADS * DN_DV
COL_ATK = COL_ATQ + ATT_HEADS * ATT_DIM
COL_ATV = COL_ATK + ATT_HEADS * ATT_DIM
COL_IXQ = COL_ATV + ATT_HEADS * ATT_DIM
COL_GATE = COL_IXQ + IDX_HEADS * IDX_DIM
MAIN_COLS = COL_GATE + 2 * D_MODEL
SM_B = 0
SM_A = 8
SM_IXW = 16
SM_IXK = 64


def _params(sem, vmem_mb=48):
    return pltpu.CompilerParams(dimension_semantics=sem, vmem_limit_bytes=vmem_mb * 1024 * 1024)


def _sigmoid(x):
    return 1.0 / (1.0 + jnp.exp(-x))


def _dot(a, b):
    return jnp.dot(a, b, preferred_element_type=F32)


def _dot_nt(a, b):
    return lax.dot_general(a, b, (((1,), (1,)), ((), ())), preferred_element_type=F32)


def _proj_kernel(x_ref, g_ref, w_ref, ws_ref, o_ref, os_ref, h_ref):
    @pl.when(pl.program_id(1) == 0)
    def _():
        x = x_ref[...]
        h = x * lax.rsqrt(jnp.mean(x * x, axis=-1, keepdims=True) + EPS) * g_ref[...]
        h_ref[...] = h.astype(BF16)
        os_ref[...] = _dot(h_ref[...], ws_ref[...])

    o_ref[...] = _dot(h_ref[...], w_ref[...]).astype(o_ref.dtype)


def _proj_call(x2, gain, w_main, w_small):
    L, D = x2.shape
    N = w_main.shape[1]
    tm = min(512, L)
    tn = N // 4
    return pl.pallas_call(
        _proj_kernel,
        grid=(L // tm, N // tn),
        in_specs=[
            pl.BlockSpec((tm, D), lambda i, j: (i, 0)),
            pl.BlockSpec((1, D), lambda i, j: (0, 0)),
            pl.BlockSpec((D, tn), lambda i, j: (0, j)),
            pl.BlockSpec((D, LANES), lambda i, j: (0, 0)),
        ],
        out_specs=[
            pl.BlockSpec((tm, tn), lambda i, j: (i, j)),
            pl.BlockSpec((tm, LANES), lambda i, j: (i, 0)),
        ],
        out_shape=[jax.ShapeDtypeStruct((L, N), BF16), jax.ShapeDtypeStruct((L, LANES), F32)],
        scratch_shapes=[pltpu.VMEM((tm, D), BF16)],
        compiler_params=_params(("parallel", "arbitrary")),
        name="proj",
    )(x2, gain, w_main, w_small)


def _gdn_prep_kernel(qkv_ref, halo_ref, sm_ref, cw_ref, alog_ref, dtb_ref,
                     q_ref, k_ref, v_ref, g_ref, b_ref, xs_ref):
    tT = qkv_ref.shape[0]
    first = pl.program_id(0) == 0
    xs_ref[8:8 + tT, :] = qkv_ref[...].astype(F32)
    halo = halo_ref[8:16, :].astype(F32)
    xs_ref[0:8, :] = jnp.where(first, 0.0, halo)
    nh = DN_HEADS * DN_DK
    for c in range(QKV_COLS // LANES):
        sl = slice(c * LANES, (c + 1) * LANES)
        y = xs_ref[8:8 + tT, sl] * cw_ref[3:4, sl]
        for j in range(DN_CONV - 1):
            y = y + xs_ref[5 + j:5 + j + tT, sl] * cw_ref[j:j + 1, sl]
        y = y * _sigmoid(y)
        if c < 2 * DN_HEADS:
            y = y * lax.rsqrt(jnp.sum(y * y, axis=-1, keepdims=True) + EPS)
            if c < DN_HEADS:
                q_ref[:, sl] = (y * (DN_DK ** -0.5)).astype(BF16)
            else:
                k_ref[:, c * LANES - nh:(c + 1) * LANES - nh] = y.astype(BF16)
        else:
            v_ref[:, c * LANES - 2 * nh:(c + 1) * LANES - 2 * nh] = y.astype(BF16)
    sm = sm_ref[...]
    b_ref[...] = _sigmoid(sm[:, SM_B:SM_B + DN_HEADS])
    a = sm[:, SM_A:SM_A + DN_HEADS] + dtb_ref[...]
    softplus = jnp.maximum(a, 0.0) + jnp.log1p(jnp.exp(-jnp.abs(a)))
    g_ref[...] = -jnp.exp(alog_ref[...]) * softplus


def _gdn_prep_call(proj, small, conv_w, a_log, dt_bias):
    L = proj.shape[0]
    tT = min(256, L)
    hb = tT // 16
    nh = DN_HEADS * DN_DK
    return pl.pallas_call(
        _gdn_prep_kernel,
        grid=(L // tT,),
        in_specs=[
            pl.BlockSpec((tT, QKV_COLS), lambda i: (i, 0)),
            pl.BlockSpec((16, QKV_COLS), lambda i: (jnp.maximum(i * hb - 1, 0), 0)),
            pl.BlockSpec((tT, LANES), lambda i: (i, 0)),
            pl.BlockSpec((DN_CONV, QKV_COLS), lambda i: (0, 0)),
            pl.BlockSpec((1, DN_HEADS), lambda i: (0, 0)),
            pl.BlockSpec((1, DN_HEADS), lambda i: (0, 0)),
        ],
        out_specs=[
            pl.BlockSpec((tT, nh), lambda i: (i, 0)),
            pl.BlockSpec((tT, nh), lambda i: (i, 0)),
            pl.BlockSpec((tT, nh), lambda i: (i, 0)),
            pl.BlockSpec((tT, DN_HEADS), lambda i: (i, 0)),
            pl.BlockSpec((tT, DN_HEADS), lambda i: (i, 0)),
        ],
        out_shape=[jax.ShapeDtypeStruct((L, nh), BF16)] * 3 + [jax.ShapeDtypeStruct((L, DN_HEADS), F32)] * 2,
        scratch_shapes=[pltpu.VMEM((tT + 8, QKV_COLS), F32)],
        compiler_params=_params(("parallel",)),
        name="gdn_prep",
    )(proj, proj, small, conv_w, a_log, dt_bias)


GDN_STEP = 4 * DN_CHUNK


def _bdot(a, b):
    return lax.dot_general(a, b, (((2,), (1,)), ((0,), (0,))), preferred_element_type=F32)


def _bdot_nt(a, b):
    return lax.dot_general(a, b, (((2,), (2,)), ((0,), (0,))), preferred_element_type=F32)


def _gdn_chunk_kernel(q_ref, k_ref, v_ref, z_ref, gc_ref, gr_ref, bc_ref, nw_ref, y_ref, s_ref):
    C, H = DN_CHUNK, DN_HEADS
    n_chunks = q_ref.shape[0] // C

    @pl.when(pl.program_id(0) == 0)
    def _():
        s_ref[...] = jnp.zeros_like(s_ref)

    row = lax.broadcasted_iota(I32, (C, C), 0)
    col = lax.broadcasted_iota(I32, (C, C), 1)
    incl = row >= col
    strict = row > col
    tri = incl.astype(F32)
    tri_t = (row <= col).astype(F32)
    eye = (row == col).astype(F32)

    def per_head(fn):
        return jnp.stack([fn(slice(c * C, (c + 1) * C), h) for c in range(n_chunks) for h in range(H)])

    head = lambda ref: per_head(lambda rs, h: ref[rs, h * DN_DK:(h + 1) * DN_DK])
    qb, kb, vb = head(q_ref), head(k_ref), head(v_ref)
    G_col = [jnp.dot(tri, gc_ref[c * C:(c + 1) * C, :], preferred_element_type=F32, precision=lax.Precision.HIGHEST)
             for c in range(n_chunks)]
    G_row = [jnp.dot(gr_ref[:, c * C:(c + 1) * C], tri_t, preferred_element_type=F32, precision=lax.Precision.HIGHEST)
             for c in range(n_chunks)]
    Gc = jnp.stack([G_col[c][:, h:h + 1] for c in range(n_chunks) for h in range(H)])
    Gr = jnp.stack([G_row[c][h:h + 1, :] for c in range(n_chunks) for h in range(H)])
    Gl = jnp.stack([G_col[c][C - 1:C, h:h + 1] for c in range(n_chunks) for h in range(H)])
    bcol = per_head(lambda rs, h: bc_ref[rs, h:h + 1])

    decay = jnp.exp(jnp.where(incl[None], Gc - Gr, -jnp.inf))
    A = jnp.where(strict[None], bcol * _bdot_nt(kb, kb) * decay, 0.0)
    M = -A
    T = eye[None] + M
    for _ in range(5):
        Mb = M.astype(BF16)
        M = _bdot(Mb, Mb)
        T = T + _bdot(T.astype(BF16), M.astype(BF16))
    Tb = T.astype(BF16)
    eg = jnp.exp(Gc)
    kf = kb.astype(F32)
    w = _bdot(Tb, (kf * (bcol * eg)).astype(BF16)).astype(BF16)
    u = _bdot(Tb, (vb.astype(F32) * bcol).astype(BF16))
    attn = (_bdot_nt(qb, kb) * decay).astype(BF16)
    q_dec = (qb.astype(F32) * eg).astype(BF16)
    k_dec = (kf * jnp.exp(Gl - Gc)).astype(BF16)
    g_last = jnp.exp(Gl)

    nw = nw_ref[...]
    S = s_ref[...]
    for c in range(n_chunks):
        rs = slice(c * C, (c + 1) * C)
        bs = slice(c * H, (c + 1) * H)
        Sb = S.astype(BF16)
        v_new = (u[bs] - _bdot(w[bs], Sb)).astype(BF16)
        o = _bdot(q_dec[bs], Sb) + _bdot(attn[bs], v_new)
        S = g_last[bs] * S + jnp.stack([
            lax.dot_general(k_dec[c * H + h], v_new[h], (((0,), (0,)), ((), ())), preferred_element_type=F32)
            for h in range(H)])
        on = o * lax.rsqrt(jnp.mean(o * o, axis=-1, keepdims=True) + EPS) * nw
        for h in range(H):
            hs = slice(h * DN_DV, (h + 1) * DN_DV)
            zz = z_ref[rs, hs].astype(F32)
            y_ref[rs, hs] = (on[h] * (zz * _sigmoid(zz))).astype(BF16)
    s_ref[...] = S


def _gdn_chunk_call(q, k, v, proj, g, g_t, beta, norm_w):
    L, nh = q.shape
    R = GDN_STEP
    zc = COL_Z // nh
    return pl.pallas_call(
        _gdn_chunk_kernel,
        grid=(L // R,),
        in_specs=[
            pl.BlockSpec((R, nh), lambda i: (i, 0)),
            pl.BlockSpec((R, nh), lambda i: (i, 0)),
            pl.BlockSpec((R, nh), lambda i: (i, 0)),
            pl.BlockSpec((R, nh), lambda i: (i, zc)),
            pl.BlockSpec((R, DN_HEADS), lambda i: (i, 0)),
            pl.BlockSpec((DN_HEADS, R), lambda i: (0, i)),
            pl.BlockSpec((R, DN_HEADS), lambda i: (i, 0)),
            pl.BlockSpec((1, DN_DV), lambda i: (0, 0)),
        ],
        out_specs=pl.BlockSpec((R, nh), lambda i: (i, 0)),
        out_shape=jax.ShapeDtypeStruct((L, nh), BF16),
        scratch_shapes=[pltpu.VMEM((DN_HEADS, DN_DK, DN_DV), F32)],
        compiler_params=_params(("arbitrary",)),
        name="gdn_chunk",
    )(q, k, v, proj, g, g_t, beta, norm_w)


def _rope_tables(pos, period, lane):
    rot = period // ROPE_FRACTION
    half = rot // 2
    lp = lane % period
    expo = -((lp % half).astype(F32) * 2.0 / rot)
    inv_freq = jnp.power(jnp.float32(ROPE_THETA), expo)
    ang = pos * inv_freq
    cos, sin = jnp.cos(ang), jnp.sin(ang)
    c = jnp.where(lp < rot, cos, 1.0)
    s_lo = jnp.where(lp < half, -sin, 0.0)
    s_hi = jnp.where((lp >= half) & (lp < rot), sin, 0.0)
    return c, s_lo, s_hi, half


def _rope(x, tab):
    c, s_lo, s_hi, half = tab
    return x * c + pltpu.roll(x, LANES - half, 1) * s_lo + pltpu.roll(x, half, 1) * s_hi


def _dsa_prep_kernel(q_ref, k_ref, v_ref, iq_ref, sm_ref, pos_ref, kn_ref,
                     qo_ref, ko_ref, vto_ref, iqo_ref, iko_ref, wto_ref, qno_ref, kno_ref):
    lane = lax.broadcasted_iota(I32, (1, LANES), 1)
    pos = pos_ref[...]
    tab_att = _rope_tables(pos, ATT_DIM, lane)
    tab_idx = _rope_tables(pos, IDX_DIM, lane)
    scale = ATT_DIM ** -0.5 * LOG2_E
    q_norms = jnp.zeros((q_ref.shape[0], LANES), F32)
    k_norms = jnp.zeros((q_ref.shape[0], LANES), F32)
    for h in range(ATT_HEADS):
        hs = slice(h * ATT_DIM, (h + 1) * ATT_DIM)
        qh = (_rope(q_ref[:, hs].astype(F32), tab_att) * scale).astype(BF16)
        kh = _rope(k_ref[:, hs].astype(F32), tab_att).astype(BF16)
        qo_ref[:, hs] = qh
        ko_ref[:, hs] = kh
        vto_ref[hs, :] = v_ref[:, hs].astype(F32).T.astype(BF16)
        norm = lambda t: jnp.sqrt(jnp.sum(t.astype(F32) ** 2, axis=-1, keepdims=True))
        q_norms = jnp.where(lane == h, norm(qh), q_norms)
        k_norms = jnp.where(lane == h, norm(kh), k_norms)
    qno_ref[...] = q_norms.T[0:ATT_HEADS, :]
    kno_ref[...] = k_norms.T[0:ATT_HEADS, :]
    low = lane < IDX_DIM
    for p in range(IDX_HEADS // 2):
        x = _rope(iq_ref[:, p * LANES:(p + 1) * LANES].astype(F32), tab_idx)
        iqo_ref[:, (2 * p) * LANES:(2 * p + 1) * LANES] = jnp.where(low, x, 0.0).astype(BF16)
        iqo_ref[:, (2 * p + 1) * LANES:(2 * p + 2) * LANES] = jnp.where(low, pltpu.roll(x, IDX_DIM, 1), 0.0).astype(BF16)
    sm = sm_ref[...]
    kx = jnp.where(low, 0.0, sm)
    kx = kx * lax.rsqrt(jnp.sum(kx * kx, axis=-1, keepdims=True) * (1.0 / IDX_DIM) + EPS) * kn_ref[...]
    kx = _rope(kx, tab_idx)
    iko_ref[...] = jnp.where(low, pltpu.roll(kx, IDX_DIM, 1), 0.0).astype(BF16)
    wto_ref[...] = (sm * (IDX_HEADS ** -0.5 * IDX_DIM ** -0.5)).T[SM_IXW:SM_IXW + IDX_HEADS, :]


def _dsa_prep_call(proj, small, pos_col, kn_lanes):
    L = proj.shape[0]
    tT = min(256, L)
    na = ATT_HEADS * ATT_DIM
    ni = IDX_HEADS * IDX_DIM
    return pl.pallas_call(
        _dsa_prep_kernel,
        grid=(L // tT,),
        in_specs=[
            pl.BlockSpec((tT, na), lambda i: (i, COL_ATQ // na)),
            pl.BlockSpec((tT, na), lambda i: (i, COL_ATK // na)),
            pl.BlockSpec((tT, na), lambda i: (i, COL_ATV // na)),
            pl.BlockSpec((tT, ni), lambda i: (i, COL_IXQ // ni)),
            pl.BlockSpec((tT, LANES), lambda i: (i, 0)),
            pl.BlockSpec((tT, 1), lambda i: (i, 0)),
            pl.BlockSpec((1, LANES), lambda i: (0, 0)),
        ],
        out_specs=[
            pl.BlockSpec((tT, na), lambda i: (i, 0)),
            pl.BlockSpec((tT, na), lambda i: (i, 0)),
            pl.BlockSpec((na, tT), lambda i: (0, i)),
            pl.BlockSpec((tT, IDX_HEADS * LANES), lambda i: (i, 0)),
            pl.BlockSpec((tT, LANES), lambda i: (i, 0)),
            pl.BlockSpec((IDX_HEADS, tT), lambda i: (0, i)),
            pl.BlockSpec((ATT_HEADS, tT), lambda i: (0, i)),
            pl.BlockSpec((ATT_HEADS, tT), lambda i: (0, i)),
        ],
        out_shape=[jax.ShapeDtypeStruct((L, na), BF16), jax.ShapeDtypeStruct((L, na), BF16),
                   jax.ShapeDtypeStruct((na, L), BF16),
                   jax.ShapeDtypeStruct((L, IDX_HEADS * LANES), BF16), jax.ShapeDtypeStruct((L, LANES), BF16),
                   jax.ShapeDtypeStruct((IDX_HEADS, L), F32), jax.ShapeDtypeStruct((ATT_HEADS, L), F32),
                   jax.ShapeDtypeStruct((ATT_HEADS, L), F32)],
        compiler_params=_params(("parallel",)),
        name="dsa_prep",
    )(proj, proj, proj, proj, small, pos_col, kn_lanes)


def _index_scores_t(ik_blk, iq_heads, w_rows):
    acc = None
    for qh, wh in zip(iq_heads, w_rows):
        term = wh * jnp.maximum(_dot_nt(ik_blk, qh), 0.0)
        acc = term if acc is None else acc + term
    return acc


def _sortable(bits):
    return jnp.where(bits < 0, bits ^ jnp.int32(0x7FFFFFFF), bits)


def _score_keys(s):
    key = _sortable(pltpu.bitcast(s, I32))
    return jnp.where(key == -1, 0, key)


DSA_TQ = 256
DSA_TK = 512
SUM_ROWS = 16
KEY_NEG_INF = int(np.int32(np.array(-np.inf, np.float32).view(np.int32)) ^ np.int32(0x7FFFFFFF))
INT_MIN = -(2 ** 31)
INT_MAX = 2 ** 31 - 1


def _dsa_kernel(q_ref, iq_ref, wt_ref, qn_ref, ik_ref, kn_ref, k_hbm, vt_hbm, o_ref,
                sc_ref, hi_ref, kbuf, vbuf, sem, tie_ref, bias_ref, *acc_refs, topk, tk):
    TQ = q_ref.shape[0]
    L = ik_ref.shape[0]
    t0 = pl.program_id(0) * TQ
    nkt = (t0 + TQ + tk - 1) // tk

    def kv_copies(kt, slot):
        off = pl.multiple_of(kt * tk, tk)
        return (pltpu.make_async_copy(k_hbm.at[pl.ds(off, tk)], kbuf.at[slot], sem.at[0, slot]),
                pltpu.make_async_copy(vt_hbm.at[:, pl.ds(off, tk)], vbuf.at[slot], sem.at[1, slot]))

    for cp in kv_copies(0, 0):
        cp.start()

    iq_heads = [iq_ref[:, h * LANES:(h + 1) * LANES] for h in range(IDX_HEADS)]
    w_rows = [wt_ref[h:h + 1, :] for h in range(IDX_HEADS)]
    qpos = t0 + lax.broadcasted_iota(I32, (tk, TQ), 1)
    krow = lax.broadcasted_iota(I32, (tk, TQ), 0)

    def fill(kt, carry, on_diagonal):
        off = pl.multiple_of(kt * tk, tk)
        s = _index_scores_t(ik_ref[pl.ds(off, tk), :], iq_heads, w_rows)
        if on_diagonal:
            s = jnp.where(krow + off <= qpos, s, -jnp.inf)
        keys = _score_keys(s)
        sc_ref[pl.ds(off, tk), :] = keys
        hi_ref[pl.ds(off, tk), :] = (keys >> 16).astype(I16)
        return carry

    n_below = (t0 + 1) // tk
    lax.fori_loop(0, n_below, functools.partial(fill, on_diagonal=False), 0)
    lax.fori_loop(n_below, nkt, functools.partial(fill, on_diagonal=True), 0)

    def count(pred):
        def body(kt, acc):
            off = pl.multiple_of(kt * tk, tk)
            m = pred(sc_ref[pl.ds(off, tk), :], krow + off).astype(I32)
            return acc + jnp.sum(m.reshape(tk // 32, 32, TQ), axis=0)
        acc = lax.fori_loop(0, nkt, body, jnp.zeros((32, TQ), I32))
        return jnp.sum(acc, axis=0, keepdims=True)

    def count_hi(cand):
        def body(kt, acc):
            off = pl.multiple_of(kt * tk, tk)
            m = (hi_ref[pl.ds(off, tk), :] >= cand).astype(I16)
            for g in range(tk // 32):
                acc = acc + m[g * 32:(g + 1) * 32]
            return acc
        acc = lax.fori_loop(0, nkt, body, jnp.zeros((32, TQ), I16))
        return jnp.sum(acc.astype(I32), axis=0, keepdims=True)

    def hi_body(i, st):
        u, cnt = st
        uc = u | (jnp.int32(2 ** 15) >> i)
        c = count_hi((uc - 2 ** 15).astype(I16))
        ok = c >= topk
        return jnp.where(ok, uc, u), jnp.where(ok, c, cnt)

    cnt_all = jnp.zeros((1, TQ), I32) + nkt * tk
    u, cnt = lax.fori_loop(0, 16, hi_body, (jnp.zeros((1, TQ), I32), cnt_all))
    base = (u - 2 ** 15) << 16

    theta_hi = (u - 2 ** 15).astype(I16)
    above_hi = jnp.where(u >= 2 ** 16 - 1, 0, count_hi((jnp.minimum(u, 2 ** 16 - 2) + 1 - 2 ** 15).astype(I16)))

    def low_halves(kt, carry):
        off = pl.multiple_of(kt * tk, tk)
        low = ((sc_ref[pl.ds(off, tk), :] & 0xFFFF) - 2 ** 15).astype(I16)
        hi_ref[pl.ds(off, tk), :] = jnp.where(hi_ref[pl.ds(off, tk), :] == theta_hi, low, jnp.int16(-2 ** 15))
        return carry

    lax.fori_loop(0, nkt, low_halves, 0)

    def lo_body(i, st):
        delta, cnt = st
        dc = delta | (jnp.int32(2 ** 15) >> i)
        c = above_hi + count_hi((dc - 2 ** 15).astype(I16))
        ok = c >= topk
        return jnp.where(ok, dc, delta), jnp.where(ok, c, cnt)

    LO_GROUP = 4

    def group_cond(st):
        g, _, cnt = st
        return (g < 16 // LO_GROUP) & (jnp.max(jnp.abs(cnt - topk)) > 0)

    def group_body(st):
        g, delta, cnt = st
        delta, cnt = lax.fori_loop(0, LO_GROUP, lambda j, s: lo_body(g * LO_GROUP + j, s), (delta, cnt))
        return g + 1, delta, cnt

    _, delta, cnt = lax.while_loop(group_cond, group_body, (jnp.int32(0), jnp.zeros((1, TQ), I32), cnt))
    theta = base + delta

    none_valid = theta <= KEY_NEG_INF
    tied = (cnt > topk) & jnp.logical_not(none_valid)
    any_tied = jnp.max(tied.astype(I32)) > 0
    tie_ref[0:1, :] = jnp.where(none_valid, 0, INT_MAX)
    tie_ref[1:2, :] = jnp.zeros((1, TQ), I32)

    @pl.when(any_tied)
    def _():
        above = count(lambda keys, _: keys > theta)
        tie_ref[0:1, :] = jnp.where(tied, topk - above, tie_ref[0:1, :])

    need = tie_ref[0:1, :]
    theta_keep = jnp.where(need > 0, theta, theta + 1)

    ones_rows = jnp.ones((SUM_ROWS, tk), BF16)
    heads = [slice(h * ATT_DIM, (h + 1) * ATT_DIM) for h in range(ATT_HEADS)]

    def sweep(step, init):
        for acc_ref in acc_refs:
            acc_ref[...] = jnp.zeros_like(acc_ref)
        tie_ref[1:2, :] = jnp.zeros((1, TQ), I32)

        def body(kt, carry):
            slot = kt % 2
            off = pl.multiple_of(kt * tk, tk)
            for cp in kv_copies(kt, slot):
                cp.wait()

            @pl.when(kt + 1 < nkt)
            def _():
                for cp in kv_copies(kt + 1, 1 - slot):
                    cp.start()

            keys = sc_ref[pl.ds(off, tk), :]

            @pl.when(jnp.logical_not(any_tied))
            def _():
                bias_ref[...] = jnp.where(keys >= theta_keep, 0.0, NEG_BIG)

            @pl.when(any_tied)
            def _():
                tie = keys == theta
                r = lax.broadcasted_iota(I32, (tk, tk), 0)
                c = lax.broadcasted_iota(I32, (tk, tk), 1)
                earlier = _dot((r > c).astype(F32).astype(BF16), jnp.where(tie, 1.0, 0.0).astype(BF16))
                rank = earlier + tie_ref[1:2, :].astype(F32)
                keep = (keys > theta) | (tie & (rank < need.astype(F32)))
                bias_ref[...] = jnp.where(keep, 0.0, NEG_BIG)
                tie_ref[1:2, :] += jnp.sum(tie.astype(I32), axis=0, keepdims=True)

            return step(slot, carry)

        return lax.fori_loop(0, nkt, body, init)

    def v_ext(slot, h):
        return jnp.concatenate([vbuf[slot, heads[h], :], ones_rows], axis=0)

    shift = qn_ref[...] * jnp.max(kn_ref[...], axis=1, keepdims=True)

    def fixed_shift_step(slot, carry):
        probs = [jnp.exp2(_dot_nt(kbuf[slot, :, hs], q_ref[:, hs]) + bias_ref[...] - shift[h:h + 1, :]).astype(BF16)
                 for h, hs in enumerate(heads)]
        for h, acc_ref in enumerate(acc_refs):
            acc_ref[...] += _dot(v_ext(slot, h), probs[h])
        return carry

    sweep(fixed_shift_step, 0)
    norm_min = functools.reduce(jnp.minimum, [acc_ref[ATT_DIM:ATT_DIM + 1, :] for acc_ref in acc_refs])

    @pl.when(jnp.min(norm_min) < 2.0 ** -80)
    def _():
        def running_max_step(slot, m_run):
            logits, tile_max = [], []
            for hs in heads:
                s = _dot_nt(kbuf[slot, :, hs], q_ref[:, hs]) + bias_ref[...]
                logits.append(s)
                tile_max.append(jnp.max(s, axis=0, keepdims=True))
            m_rows = []
            for h, acc_ref in enumerate(acc_refs):
                m_old = m_run[h:h + 1, :]
                m_new = jnp.maximum(m_old, tile_max[h])
                p = jnp.exp2(logits[h] - m_new).astype(BF16)
                acc_ref[...] = jnp.exp2(m_old - m_new) * acc_ref[...] + _dot(v_ext(slot, h), p)
                m_rows.append(m_new)
            return jnp.concatenate(m_rows, axis=0)

        for cp in kv_copies(0, 0):
            cp.start()
        sweep(running_max_step, jnp.full((ATT_HEADS, TQ), NEG_BIG, F32))

    for h, acc_ref in enumerate(acc_refs):
        out_t = acc_ref[0:ATT_DIM, :] / acc_ref[ATT_DIM:ATT_DIM + 1, :]
        o_ref[:, h * ATT_DIM:(h + 1) * ATT_DIM] = out_t.T.astype(BF16)


def _dsa_call(q, k, vt, iq, ik, wt, qn, kn, topk):
    L, na = q.shape
    TQ = min(DSA_TQ, L)
    tk = min(DSA_TK, L)
    assert topk <= tk
    return pl.pallas_call(
        functools.partial(_dsa_kernel, topk=topk, tk=tk),
        grid=(L // TQ,),
        in_specs=[
            pl.BlockSpec((TQ, na), lambda i: (i, 0)),
            pl.BlockSpec((TQ, IDX_HEADS * LANES), lambda i: (i, 0)),
            pl.BlockSpec((IDX_HEADS, TQ), lambda i: (0, i)),
            pl.BlockSpec((ATT_HEADS, TQ), lambda i: (0, i)),
            pl.BlockSpec((L, LANES), lambda i: (0, 0)),
            pl.BlockSpec((ATT_HEADS, L), lambda i: (0, 0)),
            pl.BlockSpec(memory_space=pl.ANY),
            pl.BlockSpec(memory_space=pl.ANY),
        ],
        out_specs=pl.BlockSpec((TQ, na), lambda i: (i, 0)),
        out_shape=jax.ShapeDtypeStruct((L, na), BF16),
        scratch_shapes=[
            pltpu.VMEM((L, TQ), I32),
            pltpu.VMEM((L, TQ), I16),
            pltpu.VMEM((2, tk, na), BF16),
            pltpu.VMEM((2, na, tk), BF16),
            pltpu.SemaphoreType.DMA((2, 2)),
            pltpu.VMEM((8, TQ), I32),
            pltpu.VMEM((tk, TQ), F32),
        ] + [pltpu.VMEM((ATT_DIM + SUM_ROWS, TQ), F32)] * ATT_HEADS,
        compiler_params=_params(("parallel",), vmem_mb=56),
        name="dsa",
    )(q, iq, wt, qn, ik, kn, k, vt)


def _merge_kernel(g0_ref, g1_ref, b0_ref, b1_ref, ydn_ref, yat_ref, wdn_ref, wat_ref, o_ref):
    gate0 = _sigmoid(g0_ref[...].astype(F32) + b0_ref[...])
    gate1 = _sigmoid(g1_ref[...].astype(F32) + b1_ref[...])
    merged = gate0 * _dot(ydn_ref[...], wdn_ref[...]) + gate1 * _dot(yat_ref[...], wat_ref[...])
    o_ref[...] = merged.astype(o_ref.dtype)


def _merge_call(proj, b_gates, y_dn, y_at, w_dn, w_at):
    L = proj.shape[0]
    tm = min(1024, L)
    tn = 512
    nj = D_MODEL // tn
    c0 = COL_GATE // tn
    kd = y_dn.shape[1]
    return pl.pallas_call(
        _merge_kernel,
        grid=(L // tm, nj),
        in_specs=[
            pl.BlockSpec((tm, tn), lambda i, j: (i, c0 + j)),
            pl.BlockSpec((tm, tn), lambda i, j: (i, c0 + nj + j)),
            pl.BlockSpec((1, tn), lambda i, j: (0, j)),
            pl.BlockSpec((1, tn), lambda i, j: (0, nj + j)),
            pl.BlockSpec((tm, kd), lambda i, j: (i, 0)),
            pl.BlockSpec((tm, kd), lambda i, j: (i, 0)),
            pl.BlockSpec((kd, tn), lambda i, j: (0, j)),
            pl.BlockSpec((kd, tn), lambda i, j: (0, j)),
        ],
        out_specs=pl.BlockSpec((tm, tn), lambda i, j: (i, j)),
        out_shape=jax.ShapeDtypeStruct((L, D_MODEL), BF16),
        compiler_params=_params(("parallel", "arbitrary")),
        name="merge",
    )(proj, proj, b_gates, b_gates, y_dn, y_at, w_dn, w_at)


RT_E0, RT_E1, RT_W0, RT_W1 = 0, 1, 2, 3


def _first_lane_of_max(v, lane):
    m = jnp.max(v, axis=-1, keepdims=True)
    return m, jnp.min(jnp.where(v == m, lane, LANES), axis=-1, keepdims=True)


def _outproj_kernel(x_ref, mg_ref, wo_ref, nf_ref, wr_ref, br_ref, x1_ref, h2_ref, rt_ref):
    x1 = x_ref[...] + _dot(mg_ref[...], wo_ref[...])
    x1_ref[...] = x1
    h2 = x1 * lax.rsqrt(jnp.mean(x1 * x1, axis=-1, keepdims=True) + EPS) * nf_ref[...]
    h2_ref[...] = h2
    lg = _dot(h2.astype(BF16), wr_ref[...]) + br_ref[...]
    lane = lax.broadcasted_iota(I32, (1, LANES), 1)
    ninf = -jnp.inf
    gl = jnp.where(lane < N_GROUPS, lg, ninf)
    gmax, g_sel = _first_lane_of_max(gl, lane)
    p_group = 1.0 / jnp.sum(jnp.exp(gl - gmax), axis=-1, keepdims=True)
    ex = lane - N_GROUPS
    in_group = (ex >= 0) & (ex < N_EXPERTS) & ((ex // EXPERTS_PER_GROUP) == g_sel)
    el = jnp.where(in_group, lg, ninf)
    m1, i1 = _first_lane_of_max(el, lane)
    m2, i2 = _first_lane_of_max(jnp.where(lane == i1, ninf, el), lane)
    e2 = jnp.exp(m2 - m1)
    w0 = p_group / (1.0 + e2)
    w1 = p_group * e2 / (1.0 + e2)
    rec = jnp.where(lane == RT_E0, (i1 - N_GROUPS).astype(F32), 0.0)
    rec = jnp.where(lane == RT_E1, (i2 - N_GROUPS).astype(F32), rec)
    rec = jnp.where(lane == RT_W0, w0, rec)
    rt_ref[...] = jnp.where(lane == RT_W1, w1, rec)


def _outproj_call(x2, merged, w_out, norm_ffn, w_route, b_route):
    L = x2.shape[0]
    tm = min(256, L)
    row = lambda i: (i, 0)
    fixed = lambda i: (0, 0)
    return pl.pallas_call(
        _outproj_kernel,
        grid=(L // tm,),
        in_specs=[
            pl.BlockSpec((tm, D_MODEL), row),
            pl.BlockSpec((tm, D_MODEL), row),
            pl.BlockSpec((D_MODEL, D_MODEL), fixed),
            pl.BlockSpec((1, D_MODEL), fixed),
            pl.BlockSpec((D_MODEL, LANES), fixed),
            pl.BlockSpec((1, LANES), fixed),
        ],
        out_specs=[pl.BlockSpec((tm, D_MODEL), row), pl.BlockSpec((tm, D_MODEL), row), pl.BlockSpec((tm, LANES), row)],
        out_shape=[jax.ShapeDtypeStruct((L, D_MODEL), F32), jax.ShapeDtypeStruct((L, D_MODEL), F32),
                   jax.ShapeDtypeStruct((L, LANES), F32)],
        compiler_params=_params(("parallel",)),
        name="outproj_route",
    )(x2, merged, w_out, norm_ffn, w_route, b_route)


def _moe_slots(L):
    return -(-(2 * L + N_EXPERTS * (MOE_ROWS - 1)) // MOE_ROWS) * MOE_ROWS


def _moe_plan_kernel(rt_ref, dest_ref, blk_ref, cnt_ref, carry_ref, start_ref):
    phase, i = pl.program_id(0), pl.program_id(1)
    tR = rt_ref.shape[0]
    lane = lax.broadcasted_iota(I32, (1, LANES), 1)
    lane_f = lane.astype(F32)
    rt = rt_ref[...]
    e0, e1 = rt[:, RT_E0:RT_E0 + 1], rt[:, RT_E1:RT_E1 + 1]
    hit0, hit1 = lane_f == e0, lane_f == e1
    onehot = (hit0 | hit1).astype(F32)
    colsum = jnp.sum(onehot, axis=0, keepdims=True)

    @pl.when((phase == 0) & (i == 0))
    def _():
        cnt_ref[...] = jnp.zeros_like(cnt_ref)

    @pl.when(phase == 0)
    def _():
        cnt_ref[...] += colsum

    @pl.when((phase == 1) & (i == 0))
    def _():
        carry_ref[...] = jnp.zeros_like(carry_ref)
        padded = jnp.floor((cnt_ref[...] + (MOE_ROWS - 1)) * (1.0 / MOE_ROWS)) * MOE_ROWS
        r = lax.broadcasted_iota(I32, (LANES, LANES), 0)
        c = lax.broadcasted_iota(I32, (LANES, LANES), 1)
        upper = (r <= c).astype(F32)
        end = jnp.dot(jnp.broadcast_to(padded, (8, LANES)), upper, preferred_element_type=F32,
                      precision=lax.Precision.HIGHEST)[0:1, :]
        start_ref[...] = end - padded
        n_used = end[:, N_EXPERTS - 1:N_EXPERTS] * (1.0 / MOE_ROWS)
        nb = blk_ref.shape[0] - 8
        b = lax.broadcasted_iota(I32, (nb, 1), 0).astype(F32)
        b_eff = jnp.minimum(b, n_used - 1.0)
        done = ((end <= b_eff * MOE_ROWS) & (lane < N_EXPERTS)).astype(F32)
        blk_e = jnp.minimum(jnp.sum(done, axis=-1, keepdims=True), N_EXPERTS - 1.0)
        blk_ref[0:nb, :] = jnp.where(lane == 0, blk_e, jnp.where(lane == 1, n_used, 0.0)).astype(I32)
        row = lax.broadcasted_iota(I32, (8, LANES), 0)
        seg = jnp.where(row == 0, end - padded, jnp.where(row == 1, cnt_ref[...], jnp.where(row == 2, padded, 0.0)))
        blk_ref[nb:nb + 8, :] = seg.astype(I32)

    @pl.when(phase == 1)
    def _():
        r = lax.broadcasted_iota(I32, (tR, tR), 0)
        c = lax.broadcasted_iota(I32, (tR, tR), 1)
        before = _dot((r > c).astype(BF16), onehot.astype(BF16)) + carry_ref[...]
        slot = before + start_ref[...]
        d0 = jnp.sum(jnp.where(hit0, slot, 0.0), axis=-1, keepdims=True)
        d1 = jnp.sum(jnp.where(hit1, slot, 0.0), axis=-1, keepdims=True)
        dest_ref[...] = jnp.where(lane == 0, d0, jnp.where(lane == 1, d1, 0.0)).astype(I32)
        carry_ref[...] += colsum


def _moe_plan_call(route):
    L = route.shape[0]
    tR = min(256, L)
    nb = _moe_slots(L) // MOE_ROWS
    nb_pad = -(-nb // 8) * 8
    return pl.pallas_call(
        _moe_plan_kernel,
        grid=(2, L // tR),
        in_specs=[pl.BlockSpec((tR, LANES), lambda p, i: (i, 0))],
        out_specs=[pl.BlockSpec((tR, LANES), lambda p, i: (p * i, 0)),
                   pl.BlockSpec((nb_pad + 8, LANES), lambda p, i: (0, 0))],
        out_shape=[jax.ShapeDtypeStruct((L, LANES), I32), jax.ShapeDtypeStruct((nb_pad + 8, LANES), I32)],
        scratch_shapes=[pltpu.VMEM((1, LANES), F32)] * 3,
        compiler_params=_params(("arbitrary", "arbitrary")),
        name="moe_plan",
    )(route)


PLAN_START, PLAN_COUNT, PLAN_PADDED, PLAN_USED = 0, N_EXPERTS, 2 * N_EXPERTS, 3 * N_EXPERTS
PAD_PIECES = tuple(MOE_ROWS >> s for s in range(1, MOE_ROWS.bit_length() - 3))


def _moe_scatter_kernel(dest_ref, plan_ref, h_ref, xs_out, zero_ref, sem, zero_sem, *, rows, nb):
    base = pl.program_id(0) * rows

    def row_copy(r, j):
        return pltpu.make_async_copy(h_ref.at[pl.ds(r, 1)], xs_out.at[pl.ds(dest_ref[2 * (base + r) + j], 1)], sem)

    def start(r, c):
        row_copy(r, 0).start()
        row_copy(r, 1).start()
        return c

    lax.fori_loop(0, rows, start, 0, unroll=4)

    @pl.when(pl.program_id(0) == 0)
    def _():
        zero_ref[...] = jnp.zeros_like(zero_ref)

        def for_each_pad_piece(act):
            def per_expert(e, c):
                count = plan_ref[PLAN_COUNT + e]
                pad = plan_ref[PLAN_PADDED + e] - count
                first = plan_ref[PLAN_START + e] + count
                end = first + pad

                def single_row(r, cc):
                    act(pltpu.make_async_copy(zero_ref.at[pl.ds(0, 1)], xs_out.at[pl.ds(first + r, 1)], zero_sem))
                    return cc

                lax.fori_loop(0, pad & 7, single_row, 0)
                for k in PAD_PIECES:
                    @pl.when((pad & k) != 0)
                    def _():
                        at = pl.multiple_of(end - (pad & ~(k - 1)), 8)
                        act(pltpu.make_async_copy(zero_ref.at[pl.ds(0, k)], xs_out.at[pl.ds(at, k)], zero_sem))
                return c
            lax.fori_loop(0, N_EXPERTS, per_expert, 0)

        def for_each_unused_block(act):
            def per_block(b, c):
                act(pltpu.make_async_copy(zero_ref, xs_out.at[pl.ds(b * MOE_ROWS, MOE_ROWS)], zero_sem))
                return c
            lax.fori_loop(plan_ref[PLAN_USED], nb, per_block, 0)

        for_each_pad_piece(lambda cp: cp.start())
        for_each_unused_block(lambda cp: cp.start())
        for_each_pad_piece(lambda cp: cp.wait())
        for_each_unused_block(lambda cp: cp.wait())

    for _ in range(2):
        pltpu.make_async_copy(h_ref, xs_out.at[pl.ds(0, rows)], sem).wait()


def _moe_scatter_call(dest_flat, plan, h2):
    L = h2.shape[0]
    rows = min(256, L)
    slots = _moe_slots(L)
    return pl.pallas_call(
        functools.partial(_moe_scatter_kernel, rows=rows, nb=slots // MOE_ROWS),
        grid_spec=pltpu.PrefetchScalarGridSpec(
            num_scalar_prefetch=2,
            grid=(L // rows,),
            in_specs=[pl.BlockSpec((rows, D_MODEL), lambda i, d, p: (i, 0))],
            out_specs=pl.BlockSpec(memory_space=pl.ANY),
            scratch_shapes=[pltpu.VMEM((MOE_ROWS, D_MODEL), F32), pltpu.SemaphoreType.DMA(()),
                            pltpu.SemaphoreType.DMA(())],
        ),
        out_shape=jax.ShapeDtypeStruct((slots, D_MODEL), F32),
        compiler_params=_params(("arbitrary",)),
        name="moe_scatter",
    )(dest_flat, plan, h2)


def _moe_ffn_kernel(be_ref, x_ref, wg_ref, wu_ref, wd_ref, y_ref, *, nb):
    b = pl.program_id(0)

    @pl.when(b < be_ref[nb])
    def _():
        xb = x_ref[...].astype(BF16)
        gate = _dot(xb, wg_ref[...].astype(BF16))
        up = _dot(xb, wu_ref[...].astype(BF16))
        hidden = (gate * _sigmoid(gate) * up).astype(BF16)
        y_ref[...] = _dot(hidden, wd_ref[...].astype(BF16))

    @pl.when(b >= be_ref[nb])
    def _():
        y_ref[...] = jnp.zeros_like(y_ref)


def _moe_ffn_call(blk_e, xs, w_gate, w_up, w_down):
    P = xs.shape[0]
    nb = P // MOE_ROWS
    return pl.pallas_call(
        functools.partial(_moe_ffn_kernel, nb=nb),
        grid_spec=pltpu.PrefetchScalarGridSpec(
            num_scalar_prefetch=1,
            grid=(nb,),
            in_specs=[
                pl.BlockSpec((MOE_ROWS, D_MODEL), lambda b, be: (jnp.minimum(b, be[nb] - 1), 0)),
                pl.BlockSpec((None, D_MODEL, EXPERT_HIDDEN), lambda b, be: (be[b], 0, 0)),
                pl.BlockSpec((None, D_MODEL, EXPERT_HIDDEN), lambda b, be: (be[b], 0, 0)),
                pl.BlockSpec((None, EXPERT_HIDDEN, D_MODEL), lambda b, be: (be[b], 0, 0)),
            ],
            out_specs=pl.BlockSpec((MOE_ROWS, D_MODEL), lambda b, be: (b, 0)),
        ),
        out_shape=jax.ShapeDtypeStruct((P, D_MODEL), F32),
        compiler_params=_params(("arbitrary",), vmem_mb=56),
        name="moe_ffn",
    )(blk_e, xs, w_gate, w_up, w_down)


def _moe_combine_kernel(dest_ref, x1_ref, rt_ref, nf_ref, ys_hbm, o_ref, ya_ref, yb_ref, sem, *, rows):
    step, n_steps = pl.program_id(0), pl.num_programs(0)

    def gather(s, act):
        slot = s % 2

        def body(r, c):
            for j, dst in enumerate((ya_ref, yb_ref)):
                src = ys_hbm.at[pl.ds(dest_ref[2 * (s * rows + r) + j], 1)]
                act(pltpu.make_async_copy(src, dst.at[slot, pl.ds(r, 1)], sem.at[slot]))
            return c

        lax.fori_loop(0, rows, body, 0, unroll=4)

    @pl.when(step == 0)
    def _():
        gather(step, lambda cp: cp.start())

    @pl.when(step + 1 < n_steps)
    def _():
        gather(step + 1, lambda cp: cp.start())

    slot = step % 2
    for dst in (ya_ref, yb_ref):
        pltpu.make_async_copy(ys_hbm.at[pl.ds(0, rows)], dst.at[slot], sem.at[slot]).wait()
    rt = rt_ref[...]
    x = x1_ref[...] + rt[:, RT_W0:RT_W0 + 1] * ya_ref[slot] + rt[:, RT_W1:RT_W1 + 1] * yb_ref[slot]
    o_ref[...] = x * lax.rsqrt(jnp.mean(x * x, axis=-1, keepdims=True) + EPS) * nf_ref[...]


def _moe_combine_call(dest_flat, x1, route, norm_final, ys):
    L = x1.shape[0]
    rows = min(256, L)
    return pl.pallas_call(
        functools.partial(_moe_combine_kernel, rows=rows),
        grid_spec=pltpu.PrefetchScalarGridSpec(
            num_scalar_prefetch=1,
            grid=(L // rows,),
            in_specs=[
                pl.BlockSpec((rows, D_MODEL), lambda i, d: (i, 0)),
                pl.BlockSpec((rows, LANES), lambda i, d: (i, 0)),
                pl.BlockSpec((1, D_MODEL), lambda i, d: (0, 0)),
                pl.BlockSpec(memory_space=pl.ANY),
            ],
            out_specs=pl.BlockSpec((rows, D_MODEL), lambda i, d: (i, 0)),
            scratch_shapes=[pltpu.VMEM((2, rows, D_MODEL), F32), pltpu.VMEM((2, rows, D_MODEL), F32),
                            pltpu.SemaphoreType.DMA((2,))],
        ),
        out_shape=jax.ShapeDtypeStruct((L, D_MODEL), F32),
        compiler_params=_params(("arbitrary",)),
        name="moe_combine",
    )(dest_flat, x1, route, norm_final, ys)


def _pack_w_in(w):
    s = np.cumsum((0, QKV_COLS, DN_HEADS * DN_DV, DN_HEADS, DN_HEADS, ATT_HEADS * ATT_DIM, ATT_HEADS * ATT_DIM,
                   ATT_HEADS * ATT_DIM, IDX_HEADS * IDX_DIM, IDX_DIM, IDX_HEADS, 2 * D_MODEL))
    seg = lambda n: w[:, int(s[n]):int(s[n + 1])]
    w_main = jnp.concatenate([seg(0), seg(1), seg(4), seg(5), seg(6), seg(7), seg(10)], axis=1).astype(BF16)
    pad = jnp.zeros((w.shape[0], SM_IXK - SM_IXW - IDX_HEADS), w.dtype)
    w_small = jnp.concatenate([seg(2), seg(3), seg(9), pad, seg(8)], axis=1).astype(BF16)
    return w_main, w_small


def _forward(x, positions, norm_mix, w_in, b_gates, dn_conv_w, dn_a_log, dn_dt_bias, dn_norm_w, idx_k_norm,
             w_proj_dn, w_proj_att, w_out, norm_ffn, w_group, b_group, w_router, b_router, w_exp_gate, w_exp_up,
             w_exp_down, norm_final):
    st = {}
    L = x.shape[1]
    x2 = x.reshape(L, D_MODEL)
    w_main, w_small = _pack_w_in(w_in[0])
    proj, small = _proj_call(x2, norm_mix[0].reshape(1, D_MODEL), w_main, w_small)
    st["proj"], st["small"] = proj, small
    q, k, v, g, beta = _gdn_prep_call(proj, small, dn_conv_w[0], dn_a_log[0].reshape(1, DN_HEADS),
                                      dn_dt_bias[0].reshape(1, DN_HEADS))
    st["y_dn"] = _gdn_chunk_call(q, k, v, proj, g, g.T, beta, dn_norm_w[0].reshape(1, DN_DV))

    pos_col = positions.reshape(L, 1).astype(F32)
    kn_lanes = jnp.concatenate([jnp.zeros((SM_IXK,), F32), idx_k_norm[0].astype(F32)]).reshape(1, LANES)
    aq, ak, avt, iq, ik, wt, qn, kn = _dsa_prep_call(proj, small, pos_col, kn_lanes)
    st["y_at"] = _dsa_call(aq, ak, avt, iq, ik, wt, qn, kn, min(TOPK_MAX, L // 4))

    merged = _merge_call(proj, b_gates[0].reshape(1, 2 * D_MODEL), st["y_dn"], st["y_at"],
                         w_proj_dn[0].astype(BF16), w_proj_att[0].astype(BF16))
    st["merged"] = merged
    n_route = N_GROUPS + N_EXPERTS
    w_route = jnp.concatenate([w_group[0], w_router[0], jnp.zeros((D_MODEL, LANES - n_route), F32)], axis=1).astype(BF16)
    b_route = jnp.concatenate([b_group[0], b_router[0], jnp.zeros((LANES - n_route,), F32)]).reshape(1, LANES)
    x1, h2, route = _outproj_call(x2, merged, w_out[0].astype(BF16), norm_ffn[0].reshape(1, D_MODEL), w_route, b_route)
    st["x1"], st["h2"], st["route"] = x1, h2, route

    dest, blk = _moe_plan_call(route)
    dest_flat = dest[:, :2].reshape(2 * L)
    slots = _moe_slots(L)
    nb = slots // MOE_ROWS
    blk_e = jnp.concatenate([blk[:nb, 0], blk[0:1, 1]])
    seg = blk[blk.shape[0] - 8:blk.shape[0] - 5, :N_EXPERTS]
    plan = jnp.concatenate([seg.reshape(3 * N_EXPERTS), blk[0:1, 1]])
    xs = _moe_scatter_call(dest_flat, plan, h2)
    ys = _moe_ffn_call(blk_e, xs, w_exp_gate[0], w_exp_up[0], w_exp_down[0])
    out = _moe_combine_call(dest_flat, x1, route, norm_final.reshape(1, D_MODEL), ys)
    st["out"] = out.reshape(1, L, D_MODEL)
    return st


def kernel(x, positions, norm_mix, w_in, b_gates, dn_conv_w, dn_a_log, dn_dt_bias, dn_norm_w, idx_k_norm, w_proj_dn,
           w_proj_att, w_out, norm_ffn, w_group, b_group, w_router, b_router, w_exp_gate, w_exp_up, w_exp_down,
           norm_final):
    return _forward(x, positions, norm_mix, w_in, b_gates, dn_conv_w, dn_a_log, dn_dt_bias, dn_norm_w, idx_k_norm,
                    w_proj_dn, w_proj_att, w_out, norm_ffn, w_group, b_group, w_router, b_router, w_exp_gate,
                    w_exp_up, w_exp_down, norm_final)["out"]


def _stages(d, upto=None):
    return _forward(*[d[n] for n in ("x", "positions", "norm_mix", "w_in", "b_gates", "dn_conv_w", "dn_a_log",
                                     "dn_dt_bias", "dn_norm_w", "idx_k_norm", "w_proj_dn", "w_proj_att", "w_out",
                                     "norm_ffn", "w_group", "b_group", "w_router", "b_router", "w_exp_gate",
                                     "w_exp_up", "w_exp_down", "norm_final")])
```

```python
import functools

import jax
import jax.numpy as jnp
import numpy as np
from jax import lax
from jax.experimental import pallas as pl
from jax.experimental.pallas import tpu as pltpu

D_MODEL = 2048
DN_HEADS = 8
DN_DK = 128
DN_DV = 128
DN_CONV = 4
DN_CHUNK = 64
ATT_HEADS = 8
ATT_DIM = 128
IDX_HEADS = 8
IDX_DIM = 64
TOPK_MAX = 256
ROPE_THETA = 500000.0
ROPE_FRACTION = 4
N_GROUPS = 8
EXPERTS_PER_GROUP = 8
N_EXPERTS = N_GROUPS * EXPERTS_PER_GROUP
EXPERT_HIDDEN = 512
EPS = 1e-6

LANES = 128
MOE_ROWS = 256
NEG_BIG = -1e30
LOG2_E = 1.4426950408889634

F32 = jnp.float32
BF16 = jnp.bfloat16
I32 = jnp.int32
I16 = jnp.int16

QKV_COLS = 2 * DN_HEADS * DN_DK + DN_HEADS * DN_DV
COL_QKV = 0
COL_Z = COL_QKV + QKV_COLS
COL_ATQ = COL_Z + DN_HEADS * DN_DV
COL_ATK = COL_ATQ + ATT_HEADS * ATT_DIM
COL_ATV = COL_ATK + ATT_HEADS * ATT_DIM
COL_IXQ = COL_ATV + ATT_HEADS * ATT_DIM
COL_GATE = COL_IXQ + IDX_HEADS * IDX_DIM
MAIN_COLS = COL_GATE + 2 * D_MODEL
SM_B = 0
SM_A = 8
SM_IXW = 16
SM_IXK = 64


def _params(sem, vmem_mb=48):
    return pltpu.CompilerParams(dimension_semantics=sem, vmem_limit_bytes=vmem_mb * 1024 * 1024)


def _sigmoid(x):
    return 1.0 / (1.0 + jnp.exp(-x))


def _dot(a, b):
    return jnp.dot(a, b, preferred_element_type=F32)


def _dot_nt(a, b):
    return lax.dot_general(a, b, (((1,), (1,)), ((), ())), preferred_element_type=F32)


def _proj_kernel(x_ref, g_ref, w_ref, ws_ref, o_ref, os_ref, h_ref):
    @pl.when(pl.program_id(1) == 0)
    def _():
        x = x_ref[...]
        h = x * lax.rsqrt(jnp.mean(x * x, axis=-1, keepdims=True) + EPS) * g_ref[...]
        h_ref[...] = h.astype(BF16)
        os_ref[...] = _dot(h_ref[...], ws_ref[...])

    o_ref[...] = _dot(h_ref[...], w_ref[...]).astype(o_ref.dtype)


def _proj_call(x2, gain, w_main, w_small):
    L, D = x2.shape
    N = w_main.shape[1]
    tm = min(512, L)
    tn = N // 4
    return pl.pallas_call(
        _proj_kernel,
        grid=(L // tm, N // tn),
        in_specs=[
            pl.BlockSpec((tm, D), lambda i, j: (i, 0)),
            pl.BlockSpec((1, D), lambda i, j: (0, 0)),
            pl.BlockSpec((D, tn), lambda i, j: (0, j)),
            pl.BlockSpec((D, LANES), lambda i, j: (0, 0)),
        ],
        out_specs=[
            pl.BlockSpec((tm, tn), lambda i, j: (i, j)),
            pl.BlockSpec((tm, LANES), lambda i, j: (i, 0)),
        ],
        out_shape=[jax.ShapeDtypeStruct((L, N), BF16), jax.ShapeDtypeStruct((L, LANES), F32)],
        scratch_shapes=[pltpu.VMEM((tm, D), BF16)],
        compiler_params=_params(("parallel", "arbitrary")),
        name="proj",
    )(x2, gain, w_main, w_small)


def _gdn_prep_kernel(qkv_ref, halo_ref, sm_ref, cw_ref, alog_ref, dtb_ref,
                     q_ref, k_ref, v_ref, g_ref, b_ref, xs_ref):
    tT = qkv_ref.shape[0]
    first = pl.program_id(0) == 0
    xs_ref[8:8 + tT, :] = qkv_ref[...].astype(F32)
    halo = halo_ref[8:16, :].astype(F32)
    xs_ref[0:8, :] = jnp.where(first, 0.0, halo)
    nh = DN_HEADS * DN_DK
    for c in range(QKV_COLS // LANES):
        sl = slice(c * LANES, (c + 1) * LANES)
        y = xs_ref[8:8 + tT, sl] * cw_ref[3:4, sl]
        for j in range(DN_CONV - 1):
            y = y + xs_ref[5 + j:5 + j + tT, sl] * cw_ref[j:j + 1, sl]
        y = y * _sigmoid(y)
        if c < 2 * DN_HEADS:
            y = y * lax.rsqrt(jnp.sum(y * y, axis=-1, keepdims=True) + EPS)
            if c < DN_HEADS:
                q_ref[:, sl] = (y * (DN_DK ** -0.5)).astype(BF16)
            else:
                k_ref[:, c * LANES - nh:(c + 1) * LANES - nh] = y.astype(BF16)
        else:
            v_ref[:, c * LANES - 2 * nh:(c + 1) * LANES - 2 * nh] = y.astype(BF16)
    sm = sm_ref[...]
    b_ref[...] = _sigmoid(sm[:, SM_B:SM_B + DN_HEADS])
    a = sm[:, SM_A:SM_A + DN_HEADS] + dtb_ref[...]
    softplus = jnp.maximum(a, 0.0) + jnp.log1p(jnp.exp(-jnp.abs(a)))
    g_ref[...] = -jnp.exp(alog_ref[...]) * softplus


def _gdn_prep_call(proj, small, conv_w, a_log, dt_bias):
    L = proj.shape[0]
    tT = min(256, L)
    hb = tT // 16
    nh = DN_HEADS * DN_DK
    return pl.pallas_call(
        _gdn_prep_kernel,
        grid=(L // tT,),
        in_specs=[
            pl.BlockSpec((tT, QKV_COLS), lambda i: (i, 0)),
            pl.BlockSpec((16, QKV_COLS), lambda i: (jnp.maximum(i * hb - 1, 0), 0)),
            pl.BlockSpec((tT, LANES), lambda i: (i, 0)),
            pl.BlockSpec((DN_CONV, QKV_COLS), lambda i: (0, 0)),
            pl.BlockSpec((1, DN_HEADS), lambda i: (0, 0)),
            pl.BlockSpec((1, DN_HEADS), lambda i: (0, 0)),
        ],
        out_specs=[
            pl.BlockSpec((tT, nh), lambda i: (i, 0)),
            pl.BlockSpec((tT, nh), lambda i: (i, 0)),
            pl.BlockSpec((tT, nh), lambda i: (i, 0)),
            pl.BlockSpec((tT, DN_HEADS), lambda i: (i, 0)),
            pl.BlockSpec((tT, DN_HEADS), lambda i: (i, 0)),
        ],
        out_shape=[jax.ShapeDtypeStruct((L, nh), BF16)] * 3 + [jax.ShapeDtypeStruct((L, DN_HEADS), F32)] * 2,
        scratch_shapes=[pltpu.VMEM((tT + 8, QKV_COLS), F32)],
        compiler_params=_params(("parallel",)),
        name="gdn_prep",
    )(proj, proj, small, conv_w, a_log, dt_bias)


GDN_STEP = 4 * DN_CHUNK


def _bdot(a, b):
    return lax.dot_general(a, b, (((2,), (1,)), ((0,), (0,))), preferred_element_type=F32)


def _bdot_nt(a, b):
    return lax.dot_general(a, b, (((2,), (2,)), ((0,), (0,))), preferred_element_type=F32)


def _gdn_chunk_kernel(q_ref, k_ref, v_ref, z_ref, gc_ref, gr_ref, bc_ref, nw_ref, y_ref, s_ref):
    C, H = DN_CHUNK, DN_HEADS
    n_chunks = q_ref.shape[0] // C

    @pl.when(pl.program_id(0) == 0)
    def _():
        s_ref[...] = jnp.zeros_like(s_ref)

    row = lax.broadcasted_iota(I32, (C, C), 0)
    col = lax.broadcasted_iota(I32, (C, C), 1)
    incl = row >= col
    strict = row > col
    tri = incl.astype(F32)
    tri_t = (row <= col).astype(F32)
    eye = (row == col).astype(F32)

    def per_head(fn):
        return jnp.stack([fn(slice(c * C, (c + 1) * C), h) for c in range(n_chunks) for h in range(H)])

    head = lambda ref: per_head(lambda rs, h: ref[rs, h * DN_DK:(h + 1) * DN_DK])
    qb, kb, vb = head(q_ref), head(k_ref), head(v_ref)
    G_col = [jnp.dot(tri, gc_ref[c * C:(c + 1) * C, :], preferred_element_type=F32, precision=lax.Precision.HIGHEST)
             for c in range(n_chunks)]
    G_row = [jnp.dot(gr_ref[:, c * C:(c + 1) * C], tri_t, preferred_element_type=F32, precision=lax.Precision.HIGHEST)
             for c in range(n_chunks)]
    Gc = jnp.stack([G_col[c][:, h:h + 1] for c in range(n_chunks) for h in range(H)])
    Gr = jnp.stack([G_row[c][h:h + 1, :] for c in range(n_chunks) for h in range(H)])
    Gl = jnp.stack([G_col[c][C - 1:C, h:h + 1] for c in range(n_chunks) for h in range(H)])
    bcol = per_head(lambda rs, h: bc_ref[rs, h:h + 1])

    decay = jnp.exp(jnp.where(incl[None], Gc - Gr, -jnp.inf))
    A = jnp.where(strict[None], bcol * _bdot_nt(kb, kb) * decay, 0.0)
    M = -A
    T = eye[None] + M
    for _ in range(5):
        Mb = M.astype(BF16)
        M = _bdot(Mb, Mb)
        T = T + _bdot(T.astype(BF16), M.astype(BF16))
    Tb = T.astype(BF16)
    eg = jnp.exp(Gc)
    kf = kb.astype(F32)
    w = _bdot(Tb, (kf * (bcol * eg)).astype(BF16)).astype(BF16)
    u = _bdot(Tb, (vb.astype(F32) * bcol).astype(BF16))
    attn = (_bdot_nt(qb, kb) * decay).astype(BF16)
    q_dec = (qb.astype(F32) * eg).astype(BF16)
    k_dec = (kf * jnp.exp(Gl - Gc)).astype(BF16)
    g_last = jnp.exp(Gl)

    nw = nw_ref[...]
    S = s_ref[...]
    for c in range(n_chunks):
        rs = slice(c * C, (c + 1) * C)
        bs = slice(c * H, (c + 1) * H)
        Sb = S.astype(BF16)
        v_new = (u[bs] - _bdot(w[bs], Sb)).astype(BF16)
        o = _bdot(q_dec[bs], Sb) + _bdot(attn[bs], v_new)
        S = g_last[bs] * S + jnp.stack([
            lax.dot_general(k_dec[c * H + h], v_new[h], (((0,), (0,)), ((), ())), preferred_element_type=F32)
            for h in range(H)])
        on = o * lax.rsqrt(jnp.mean(o * o, axis=-1, keepdims=True) + EPS) * nw
        for h in range(H):
            hs = slice(h * DN_DV, (h + 1) * DN_DV)
            zz = z_ref[rs, hs].astype(F32)
            y_ref[rs, hs] = (on[h] * (zz * _sigmoid(zz))).astype(BF16)
    s_ref[...] = S


def _gdn_chunk_call(q, k, v, proj, g, g_t, beta, norm_w):
    L, nh = q.shape
    R = GDN_STEP
    zc = COL_Z // nh
    return pl.pallas_call(
        _gdn_chunk_kernel,
        grid=(L // R,),
        in_specs=[
            pl.BlockSpec((R, nh), lambda i: (i, 0)),
            pl.BlockSpec((R, nh), lambda i: (i, 0)),
            pl.BlockSpec((R, nh), lambda i: (i, 0)),
            pl.BlockSpec((R, nh), lambda i: (i, zc)),
            pl.BlockSpec((R, DN_HEADS), lambda i: (i, 0)),
            pl.BlockSpec((DN_HEADS, R), lambda i: (0, i)),
            pl.BlockSpec((R, DN_HEADS), lambda i: (i, 0)),
            pl.BlockSpec((1, DN_DV), lambda i: (0, 0)),
        ],
        out_specs=pl.BlockSpec((R, nh), lambda i: (i, 0)),
        out_shape=jax.ShapeDtypeStruct((L, nh), BF16),
        scratch_shapes=[pltpu.VMEM((DN_HEADS, DN_DK, DN_DV), F32)],
        compiler_params=_params(("arbitrary",)),
        name="gdn_chunk",
    )(q, k, v, proj, g, g_t, beta, norm_w)


def _rope_tables(pos, period, lane):
    rot = period // ROPE_FRACTION
    half = rot // 2
    lp = lane % period
    expo = -((lp % half).astype(F32) * 2.0 / rot)
    inv_freq = jnp.power(jnp.float32(ROPE_THETA), expo)
    ang = pos * inv_freq
    cos, sin = jnp.cos(ang), jnp.sin(ang)
    c = jnp.where(lp < rot, cos, 1.0)
    s_lo = jnp.where(lp < half, -sin, 0.0)
    s_hi = jnp.where((lp >= half) & (lp < rot), sin, 0.0)
    return c, s_lo, s_hi, half


def _rope(x, tab):
    c, s_lo, s_hi, half = tab
    return x * c + pltpu.roll(x, LANES - half, 1) * s_lo + pltpu.roll(x, half, 1) * s_hi


def _dsa_prep_kernel(q_ref, k_ref, v_ref, iq_ref, sm_ref, pos_ref, kn_ref,
                     qo_ref, ko_ref, vto_ref, iqo_ref, iko_ref, wto_ref, qno_ref, kno_ref):
    lane = lax.broadcasted_iota(I32, (1, LANES), 1)
    pos = pos_ref[...]
    tab_att = _rope_tables(pos, ATT_DIM, lane)
    tab_idx = _rope_tables(pos, IDX_DIM, lane)
    scale = ATT_DIM ** -0.5 * LOG2_E
    q_norms = jnp.zeros((q_ref.shape[0], LANES), F32)
    k_norms = jnp.zeros((q_ref.shape[0], LANES), F32)
    for h in range(ATT_HEADS):
        hs = slice(h * ATT_DIM, (h + 1) * ATT_DIM)
        qh = (_rope(q_ref[:, hs].astype(F32), tab_att) * scale).astype(BF16)
        kh = _rope(k_ref[:, hs].astype(F32), tab_att).astype(BF16)
        qo_ref[:, hs] = qh
        ko_ref[:, hs] = kh
        vto_ref[hs, :] = v_ref[:, hs].astype(F32).T.astype(BF16)
        norm = lambda t: jnp.sqrt(jnp.sum(t.astype(F32) ** 2, axis=-1, keepdims=True))
        q_norms = jnp.where(lane == h, norm(qh), q_norms)
        k_norms = jnp.where(lane == h, norm(kh), k_norms)
    qno_ref[...] = q_norms.T[0:ATT_HEADS, :]
    kno_ref[...] = k_norms.T[0:ATT_HEADS, :]
    low = lane < IDX_DIM
    for p in range(IDX_HEADS // 2):
        x = _rope(iq_ref[:, p * LANES:(p + 1) * LANES].astype(F32), tab_idx)
        iqo_ref[:, (2 * p) * LANES:(2 * p + 1) * LANES] = jnp.where(low, x, 0.0).astype(BF16)
        iqo_ref[:, (2 * p + 1) * LANES:(2 * p + 2) * LANES] = jnp.where(low, pltpu.roll(x, IDX_DIM, 1), 0.0).astype(BF16)
    sm = sm_ref[...]
    kx = jnp.where(low, 0.0, sm)
    kx = kx * lax.rsqrt(jnp.sum(kx * kx, axis=-1, keepdims=True) * (1.0 / IDX_DIM) + EPS) * kn_ref[...]
    kx = _rope(kx, tab_idx)
    iko_ref[...] = jnp.where(low, pltpu.roll(kx, IDX_DIM, 1), 0.0).astype(BF16)
    wto_ref[...] = (sm * (IDX_HEADS ** -0.5 * IDX_DIM ** -0.5)).T[SM_IXW:SM_IXW + IDX_HEADS, :]


def _dsa_prep_call(proj, small, pos_col, kn_lanes):
    L = proj.shape[0]
    tT = min(256, L)
    na = ATT_HEADS * ATT_DIM
    ni = IDX_HEADS * IDX_DIM
    return pl.pallas_call(
        _dsa_prep_kernel,
        grid=(L // tT,),
        in_specs=[
            pl.BlockSpec((tT, na), lambda i: (i, COL_ATQ // na)),
            pl.BlockSpec((tT, na), lambda i: (i, COL_ATK // na)),
            pl.BlockSpec((tT, na), lambda i: (i, COL_ATV // na)),
            pl.BlockSpec((tT, ni), lambda i: (i, COL_IXQ // ni)),
            pl.BlockSpec((tT, LANES), lambda i: (i, 0)),
            pl.BlockSpec((tT, 1), lambda i: (i, 0)),
            pl.BlockSpec((1, LANES), lambda i: (0, 0)),
        ],
        out_specs=[
            pl.BlockSpec((tT, na), lambda i: (i, 0)),
            pl.BlockSpec((tT, na), lambda i: (i, 0)),
            pl.BlockSpec((na, tT), lambda i: (0, i)),
            pl.BlockSpec((tT, IDX_HEADS * LANES), lambda i: (i, 0)),
            pl.BlockSpec((tT, LANES), lambda i: (i, 0)),
            pl.BlockSpec((IDX_HEADS, tT), lambda i: (0, i)),
            pl.BlockSpec((ATT_HEADS, tT), lambda i: (0, i)),
            pl.BlockSpec((ATT_HEADS, tT), lambda i: (0, i)),
        ],
        out_shape=[jax.ShapeDtypeStruct((L, na), BF16), jax.ShapeDtypeStruct((L, na), BF16),
                   jax.ShapeDtypeStruct((na, L), BF16),
                   jax.ShapeDtypeStruct((L, IDX_HEADS * LANES), BF16), jax.ShapeDtypeStruct((L, LANES), BF16),
                   jax.ShapeDtypeStruct((IDX_HEADS, L), F32), jax.ShapeDtypeStruct((ATT_HEADS, L), F32),
                   jax.ShapeDtypeStruct((ATT_HEADS, L), F32)],
        compiler_params=_params(("parallel",)),
        name="dsa_prep",
    )(proj, proj, proj, proj, small, pos_col, kn_lanes)


def _index_scores_t(ik_blk, iq_heads, w_rows):
    acc = None
    for qh, wh in zip(iq_heads, w_rows):
        term = wh * jnp.maximum(_dot_nt(ik_blk, qh), 0.0)
        acc = term if acc is None else acc + term
    return acc


def _sortable(bits):
    return jnp.where(bits < 0, bits ^ jnp.int32(0x7FFFFFFF), bits)


def _score_keys(s):
    key = _sortable(pltpu.bitcast(s, I32))
    return jnp.where(key == -1, 0, key)


DSA_TQ = 256
DSA_TK = 512
SUM_ROWS = 16
KEY_NEG_INF = int(np.int32(np.array(-np.inf, np.float32).view(np.int32)) ^ np.int32(0x7FFFFFFF))
INT_MIN = -(2 ** 31)
INT_MAX = 2 ** 31 - 1


def _dsa_kernel(q_ref, iq_ref, wt_ref, qn_ref, ik_ref, kn_ref, k_hbm, vt_hbm, o_ref,
                sc_ref, hi_ref, kbuf, vbuf, sem, tie_ref, bias_ref, *acc_refs, topk, tk):
    TQ = q_ref.shape[0]
    L = ik_ref.shape[0]
    t0 = pl.program_id(0) * TQ
    nkt = (t0 + TQ + tk - 1) // tk

    def kv_copies(kt, slot):
        off = pl.multiple_of(kt * tk, tk)
        return (pltpu.make_async_copy(k_hbm.at[pl.ds(off, tk)], kbuf.at[slot], sem.at[0, slot]),
                pltpu.make_async_copy(vt_hbm.at[:, pl.ds(off, tk)], vbuf.at[slot], sem.at[1, slot]))

    for cp in kv_copies(0, 0):
        cp.start()

    iq_heads = [iq_ref[:, h * LANES:(h + 1) * LANES] for h in range(IDX_HEADS)]
    w_rows = [wt_ref[h:h + 1, :] for h in range(IDX_HEADS)]
    qpos = t0 + lax.broadcasted_iota(I32, (tk, TQ), 1)
    krow = lax.broadcasted_iota(I32, (tk, TQ), 0)

    def fill(kt, carry, on_diagonal):
        off = pl.multiple_of(kt * tk, tk)
        s = _index_scores_t(ik_ref[pl.ds(off, tk), :], iq_heads, w_rows)
        if on_diagonal:
            s = jnp.where(krow + off <= qpos, s, -jnp.inf)
        keys = _score_keys(s)
        sc_ref[pl.ds(off, tk), :] = keys
        hi_ref[pl.ds(off, tk), :] = (keys >> 16).astype(I16)
        return carry

    n_below = (t0 + 1) // tk
    lax.fori_loop(0, n_below, functools.partial(fill, on_diagonal=False), 0)
    lax.fori_loop(n_below, nkt, functools.partial(fill, on_diagonal=True), 0)

    def count(pred):
        def body(kt, acc):
            off = pl.multiple_of(kt * tk, tk)
            m = pred(sc_ref[pl.ds(off, tk), :], krow + off).astype(I32)
            return acc + jnp.sum(m.reshape(tk // 32, 32, TQ), axis=0)
        acc = lax.fori_loop(0, nkt, body, jnp.zeros((32, TQ), I32))
        return jnp.sum(acc, axis=0, keepdims=True)

    def count_hi(cand):
        def body(kt, acc):
            off = pl.multiple_of(kt * tk, tk)
            m = (hi_ref[pl.ds(off, tk), :] >= cand).astype(I16)
            for g in range(tk // 32):
                acc = acc + m[g * 32:(g + 1) * 32]
            return acc
        acc = lax.fori_loop(0, nkt, body, jnp.zeros((32, TQ), I16))
        return jnp.sum(acc.astype(I32), axis=0, keepdims=True)

    def hi_body(i, st):
        u, cnt = st
        uc = u | (jnp.int32(2 ** 15) >> i)
        c = count_hi((uc - 2 ** 15).astype(I16))
        ok = c >= topk
        return jnp.where(ok, uc, u), jnp.where(ok, c, cnt)

    cnt_all = jnp.zeros((1, TQ), I32) + nkt * tk
    u, cnt = lax.fori_loop(0, 16, hi_body, (jnp.zeros((1, TQ), I32), cnt_all))
    base = (u - 2 ** 15) << 16

    theta_hi = (u - 2 ** 15).astype(I16)
    above_hi = jnp.where(u >= 2 ** 16 - 1, 0, count_hi((jnp.minimum(u, 2 ** 16 - 2) + 1 - 2 ** 15).astype(I16)))

    def low_halves(kt, carry):
        off = pl.multiple_of(kt * tk, tk)
        low = ((sc_ref[pl.ds(off, tk), :] & 0xFFFF) - 2 ** 15).astype(I16)
        hi_ref[pl.ds(off, tk), :] = jnp.where(hi_ref[pl.ds(off, tk), :] == theta_hi, low, jnp.int16(-2 ** 15))
        return carry

    lax.fori_loop(0, nkt, low_halves, 0)

    def lo_body(i, st):
        delta, cnt = st
        dc = delta | (jnp.int32(2 ** 15) >> i)
        c = above_hi + count_hi((dc - 2 ** 15).astype(I16))
        ok = c >= topk
        return jnp.where(ok, dc, delta), jnp.where(ok, c, cnt)

    LO_GROUP = 4

    def group_cond(st):
        g, _, cnt = st
        return (g < 16 // LO_GROUP) & (jnp.max(jnp.abs(cnt - topk)) > 0)

    def group_body(st):
        g, delta, cnt = st
        delta, cnt = lax.fori_loop(0, LO_GROUP, lambda j, s: lo_body(g * LO_GROUP + j, s), (delta, cnt))
        return g + 1, delta, cnt

    _, delta, cnt = lax.while_loop(group_cond, group_body, (jnp.int32(0), jnp.zeros((1, TQ), I32), cnt))
    theta = base + delta

    none_valid = theta <= KEY_NEG_INF
    tied = (cnt > topk) & jnp.logical_not(none_valid)
    any_tied = jnp.max(tied.astype(I32)) > 0
    tie_ref[0:1, :] = jnp.where(none_valid, 0, INT_MAX)
    tie_ref[1:2, :] = jnp.zeros((1, TQ), I32)

    @pl.when(any_tied)
    def _():
        above = count(lambda keys, _: keys > theta)
        tie_ref[0:1, :] = jnp.where(tied, topk - above, tie_ref[0:1, :])

    need = tie_ref[0:1, :]
    theta_keep = jnp.where(need > 0, theta, theta + 1)

    ones_rows = jnp.ones((SUM_ROWS, tk), BF16)
    heads = [slice(h * ATT_DIM, (h + 1) * ATT_DIM) for h in range(ATT_HEADS)]

    def sweep(step, init):
        for acc_ref in acc_refs:
            acc_ref[...] = jnp.zeros_like(acc_ref)
        tie_ref[1:2, :] = jnp.zeros((1, TQ), I32)

        def body(kt, carry):
            slot = kt % 2
            off = pl.multiple_of(kt * tk, tk)
            for cp in kv_copies(kt, slot):
                cp.wait()

            @pl.when(kt + 1 < nkt)
            def _():
                for cp in kv_copies(kt + 1, 1 - slot):
                    cp.start()

            keys = sc_ref[pl.ds(off, tk), :]

            @pl.when(jnp.logical_not(any_tied))
            def _():
                bias_ref[...] = jnp.where(keys >= theta_keep, 0.0, NEG_BIG)

            @pl.when(any_tied)
            def _():
                tie = keys == theta
                r = lax.broadcasted_iota(I32, (tk, tk), 0)
                c = lax.broadcasted_iota(I32, (tk, tk), 1)
                earlier = _dot((r > c).astype(F32).astype(BF16), jnp.where(tie, 1.0, 0.0).astype(BF16))
                rank = earlier + tie_ref[1:2, :].astype(F32)
                keep = (keys > theta) | (tie & (rank < need.astype(F32)))
                bias_ref[...] = jnp.where(keep, 0.0, NEG_BIG)
                tie_ref[1:2, :] += jnp.sum(tie.astype(I32), axis=0, keepdims=True)

            return step(slot, carry)

        return lax.fori_loop(0, nkt, body, init)

    def v_ext(slot, h):
        return jnp.concatenate([vbuf[slot, heads[h], :], ones_rows], axis=0)

    shift = qn_ref[...] * jnp.max(kn_ref[...], axis=1, keepdims=True)

    def fixed_shift_step(slot, carry):
        probs = [jnp.exp2(_dot_nt(kbuf[slot, :, hs], q_ref[:, hs]) + bias_ref[...] - shift[h:h + 1, :]).astype(BF16)
                 for h, hs in enumerate(heads)]
        for h, acc_ref in enumerate(acc_refs):
            acc_ref[...] += _dot(v_ext(slot, h), probs[h])
        return carry

    sweep(fixed_shift_step, 0)
    norm_min = functools.reduce(jnp.minimum, [acc_ref[ATT_DIM:ATT_DIM + 1, :] for acc_ref in acc_refs])

    @pl.when(jnp.min(norm_min) < 2.0 ** -80)
    def _():
        def running_max_step(slot, m_run):
            logits, tile_max = [], []
            for hs in heads:
                s = _dot_nt(kbuf[slot, :, hs], q_ref[:, hs]) + bias_ref[...]
                logits.append(s)
                tile_max.append(jnp.max(s, axis=0, keepdims=True))
            m_rows = []
            for h, acc_ref in enumerate(acc_refs):
                m_old = m_run[h:h + 1, :]
                m_new = jnp.maximum(m_old, tile_max[h])
                p = jnp.exp2(logits[h] - m_new).astype(BF16)
                acc_ref[...] = jnp.exp2(m_old - m_new) * acc_ref[...] + _dot(v_ext(slot, h), p)
                m_rows.append(m_new)
            return jnp.concatenate(m_rows, axis=0)

        for cp in kv_copies(0, 0):
            cp.start()
        sweep(running_max_step, jnp.full((ATT_HEADS, TQ), NEG_BIG, F32))

    for h, acc_ref in enumerate(acc_refs):
        out_t = acc_ref[0:ATT_DIM, :] / acc_ref[ATT_DIM:ATT_DIM + 1, :]
        o_ref[:, h * ATT_DIM:(h + 1) * ATT_DIM] = out_t.T.astype(BF16)


def _dsa_call(q, k, vt, iq, ik, wt, qn, kn, topk):
    L, na = q.shape
    TQ = min(DSA_TQ, L)
    tk = min(DSA_TK, L)
    assert topk <= tk
    return pl.pallas_call(
        functools.partial(_dsa_kernel, topk=topk, tk=tk),
        grid=(L // TQ,),
        in_specs=[
            pl.BlockSpec((TQ, na), lambda i: (i, 0)),
            pl.BlockSpec((TQ, IDX_HEADS * LANES), lambda i: (i, 0)),
            pl.BlockSpec((IDX_HEADS, TQ), lambda i: (0, i)),
            pl.BlockSpec((ATT_HEADS, TQ), lambda i: (0, i)),
            pl.BlockSpec((L, LANES), lambda i: (0, 0)),
            pl.BlockSpec((ATT_HEADS, L), lambda i: (0, 0)),
            pl.BlockSpec(memory_space=pl.ANY),
            pl.BlockSpec(memory_space=pl.ANY),
        ],
        out_specs=pl.BlockSpec((TQ, na), lambda i: (i, 0)),
        out_shape=jax.ShapeDtypeStruct((L, na), BF16),
        scratch_shapes=[
            pltpu.VMEM((L, TQ), I32),
            pltpu.VMEM((L, TQ), I16),
            pltpu.VMEM((2, tk, na), BF16),
            pltpu.VMEM((2, na, tk), BF16),
            pltpu.SemaphoreType.DMA((2, 2)),
            pltpu.VMEM((8, TQ), I32),
            pltpu.VMEM((tk, TQ), F32),
        ] + [pltpu.VMEM((ATT_DIM + SUM_ROWS, TQ), F32)] * ATT_HEADS,
        compiler_params=_params(("parallel",), vmem_mb=56),
        name="dsa",
    )(q, iq, wt, qn, ik, kn, k, vt)


def _merge_kernel(g0_ref, g1_ref, b0_ref, b1_ref, ydn_ref, yat_ref, wdn_ref, wat_ref, o_ref):
    gate0 = _sigmoid(g0_ref[...].astype(F32) + b0_ref[...])
    gate1 = _sigmoid(g1_ref[...].astype(F32) + b1_ref[...])
    merged = gate0 * _dot(ydn_ref[...], wdn_ref[...]) + gate1 * _dot(yat_ref[...], wat_ref[...])
    o_ref[...] = merged.astype(o_ref.dtype)


def _merge_call(proj, b_gates, y_dn, y_at, w_dn, w_at):
    L = proj.shape[0]
    tm = min(1024, L)
    tn = 512
    nj = D_MODEL // tn
    c0 = COL_GATE // tn
    kd = y_dn.shape[1]
    return pl.pallas_call(
        _merge_kernel,
        grid=(L // tm, nj),
        in_specs=[
            pl.BlockSpec((tm, tn), lambda i, j: (i, c0 + j)),
            pl.BlockSpec((tm, tn), lambda i, j: (i, c0 + nj + j)),
            pl.BlockSpec((1, tn), lambda i, j: (0, j)),
            pl.BlockSpec((1, tn), lambda i, j: (0, nj + j)),
            pl.BlockSpec((tm, kd), lambda i, j: (i, 0)),
            pl.BlockSpec((tm, kd), lambda i, j: (i, 0)),
            pl.BlockSpec((kd, tn), lambda i, j: (0, j)),
            pl.BlockSpec((kd, tn), lambda i, j: (0, j)),
        ],
        out_specs=pl.BlockSpec((tm, tn), lambda i, j: (i, j)),
        out_shape=jax.ShapeDtypeStruct((L, D_MODEL), BF16),
        compiler_params=_params(("parallel", "arbitrary")),
        name="merge",
    )(proj, proj, b_gates, b_gates, y_dn, y_at, w_dn, w_at)


RT_E0, RT_E1, RT_W0, RT_W1 = 0, 1, 2, 3


def _first_lane_of_max(v, lane):
    m = jnp.max(v, axis=-1, keepdims=True)
    return m, jnp.min(jnp.where(v == m, lane, LANES), axis=-1, keepdims=True)


def _outproj_kernel(x_ref, mg_ref, wo_ref, nf_ref, wr_ref, br_ref, x1_ref, h2_ref, rt_ref):
    x1 = x_ref[...] + _dot(mg_ref[...], wo_ref[...])
    x1_ref[...] = x1
    h2 = x1 * lax.rsqrt(jnp.mean(x1 * x1, axis=-1, keepdims=True) + EPS) * nf_ref[...]
    h2_ref[...] = h2
    lg = _dot(h2.astype(BF16), wr_ref[...]) + br_ref[...]
    lane = lax.broadcasted_iota(I32, (1, LANES), 1)
    ninf = -jnp.inf
    gl = jnp.where(lane < N_GROUPS, lg, ninf)
    gmax, g_sel = _first_lane_of_max(gl, lane)
    p_group = 1.0 / jnp.sum(jnp.exp(gl - gmax), axis=-1, keepdims=True)
    ex = lane - N_GROUPS
    in_group = (ex >= 0) & (ex < N_EXPERTS) & ((ex // EXPERTS_PER_GROUP) == g_sel)
    el = jnp.where(in_group, lg, ninf)
    m1, i1 = _first_lane_of_max(el, lane)
    m2, i2 = _first_lane_of_max(jnp.where(lane == i1, ninf, el), lane)
    e2 = jnp.exp(m2 - m1)
    w0 = p_group / (1.0 + e2)
    w1 = p_group * e2 / (1.0 + e2)
    rec = jnp.where(lane == RT_E0, (i1 - N_GROUPS).astype(F32), 0.0)
    rec = jnp.where(lane == RT_E1, (i2 - N_GROUPS).astype(F32), rec)
    rec = jnp.where(lane == RT_W0, w0, rec)
    rt_ref[...] = jnp.where(lane == RT_W1, w1, rec)


def _outproj_call(x2, merged, w_out, norm_ffn, w_route, b_route):
    L = x2.shape[0]
    tm = min(256, L)
    row = lambda i: (i, 0)
    fixed = lambda i: (0, 0)
    return pl.pallas_call(
        _outproj_kernel,
        grid=(L // tm,),
        in_specs=[
            pl.BlockSpec((tm, D_MODEL), row),
            pl.BlockSpec((tm, D_MODEL), row),
            pl.BlockSpec((D_MODEL, D_MODEL), fixed),
            pl.BlockSpec((1, D_MODEL), fixed),
            pl.BlockSpec((D_MODEL, LANES), fixed),
            pl.BlockSpec((1, LANES), fixed),
        ],
        out_specs=[pl.BlockSpec((tm, D_MODEL), row), pl.BlockSpec((tm, D_MODEL), row), pl.BlockSpec((tm, LANES), row)],
        out_shape=[jax.ShapeDtypeStruct((L, D_MODEL), F32), jax.ShapeDtypeStruct((L, D_MODEL), F32),
                   jax.ShapeDtypeStruct((L, LANES), F32)],
        compiler_params=_params(("parallel",)),
        name="outproj_route",
    )(x2, merged, w_out, norm_ffn, w_route, b_route)


def _moe_slots(L):
    return -(-(2 * L + N_EXPERTS * (MOE_ROWS - 1)) // MOE_ROWS) * MOE_ROWS


def _moe_plan_kernel(rt_ref, dest_ref, blk_ref, cnt_ref, carry_ref, start_ref):
    phase, i = pl.program_id(0), pl.program_id(1)
    tR = rt_ref.shape[0]
    lane = lax.broadcasted_iota(I32, (1, LANES), 1)
    lane_f = lane.astype(F32)
    rt = rt_ref[...]
    e0, e1 = rt[:, RT_E0:RT_E0 + 1], rt[:, RT_E1:RT_E1 + 1]
    hit0, hit1 = lane_f == e0, lane_f == e1
    onehot = (hit0 | hit1).astype(F32)
    colsum = jnp.sum(onehot, axis=0, keepdims=True)

    @pl.when((phase == 0) & (i == 0))
    def _():
        cnt_ref[...] = jnp.zeros_like(cnt_ref)

    @pl.when(phase == 0)
    def _():
        cnt_ref[...] += colsum

    @pl.when((phase == 1) & (i == 0))
    def _():
        carry_ref[...] = jnp.zeros_like(carry_ref)
        padded = jnp.floor((cnt_ref[...] + (MOE_ROWS - 1)) * (1.0 / MOE_ROWS)) * MOE_ROWS
        r = lax.broadcasted_iota(I32, (LANES, LANES), 0)
        c = lax.broadcasted_iota(I32, (LANES, LANES), 1)
        upper = (r <= c).astype(F32)
        end = jnp.dot(jnp.broadcast_to(padded, (8, LANES)), upper, preferred_element_type=F32,
                      precision=lax.Precision.HIGHEST)[0:1, :]
        start_ref[...] = end - padded
        n_used = end[:, N_EXPERTS - 1:N_EXPERTS] * (1.0 / MOE_ROWS)
        nb = blk_ref.shape[0] - 8
        b = lax.broadcasted_iota(I32, (nb, 1), 0).astype(F32)
        b_eff = jnp.minimum(b, n_used - 1.0)
        done = ((end <= b_eff * MOE_ROWS) & (lane < N_EXPERTS)).astype(F32)
        blk_e = jnp.minimum(jnp.sum(done, axis=-1, keepdims=True), N_EXPERTS - 1.0)
        blk_ref[0:nb, :] = jnp.where(lane == 0, blk_e, jnp.where(lane == 1, n_used, 0.0)).astype(I32)
        row = lax.broadcasted_iota(I32, (8, LANES), 0)
        seg = jnp.where(row == 0, end - padded, jnp.where(row == 1, cnt_ref[...], jnp.where(row == 2, padded, 0.0)))
        blk_ref[nb:nb + 8, :] = seg.astype(I32)

    @pl.when(phase == 1)
    def _():
        r = lax.broadcasted_iota(I32, (tR, tR), 0)
        c = lax.broadcasted_iota(I32, (tR, tR), 1)
        before = _dot((r > c).astype(BF16), onehot.astype(BF16)) + carry_ref[...]
        slot = before + start_ref[...]
        d0 = jnp.sum(jnp.where(hit0, slot, 0.0), axis=-1, keepdims=True)
        d1 = jnp.sum(jnp.where(hit1, slot, 0.0), axis=-1, keepdims=True)
        dest_ref[...] = jnp.where(lane == 0, d0, jnp.where(lane == 1, d1, 0.0)).astype(I32)
        carry_ref[...] += colsum


def _moe_plan_call(route):
    L = route.shape[0]
    tR = min(256, L)
    nb = _moe_slots(L) // MOE_ROWS
    nb_pad = -(-nb // 8) * 8
    return pl.pallas_call(
        _moe_plan_kernel,
        grid=(2, L // tR),
        in_specs=[pl.BlockSpec((tR, LANES), lambda p, i: (i, 0))],
        out_specs=[pl.BlockSpec((tR, LANES), lambda p, i: (p * i, 0)),
                   pl.BlockSpec((nb_pad + 8, LANES), lambda p, i: (0, 0))],
        out_shape=[jax.ShapeDtypeStruct((L, LANES), I32), jax.ShapeDtypeStruct((nb_pad + 8, LANES), I32)],
        scratch_shapes=[pltpu.VMEM((1, LANES), F32)] * 3,
        compiler_params=_params(("arbitrary", "arbitrary")),
        name="moe_plan",
    )(route)


PLAN_START, PLAN_COUNT, PLAN_PADDED, PLAN_USED = 0, N_EXPERTS, 2 * N_EXPERTS, 3 * N_EXPERTS
PAD_PIECES = tuple(MOE_ROWS >> s for s in range(1, MOE_ROWS.bit_length() - 3))


def _moe_scatter_kernel(dest_ref, plan_ref, h_ref, xs_out, zero_ref, sem, zero_sem, *, rows, nb):
    base = pl.program_id(0) * rows

    def row_copy(r, j):
        return pltpu.make_async_copy(h_ref.at[pl.ds(r, 1)], xs_out.at[pl.ds(dest_ref[2 * (base + r) + j], 1)], sem)

    def start(r, c):
        row_copy(r, 0).start()
        row_copy(r, 1).start()
        return c

    lax.fori_loop(0, rows, start, 0, unroll=8)

    @pl.when(pl.program_id(0) == 0)
    def _():
        zero_ref[...] = jnp.zeros_like(zero_ref)

        def for_each_pad_piece(act):
            def per_expert(e, c):
                count = plan_ref[PLAN_COUNT + e]
                pad = plan_ref[PLAN_PADDED + e] - count
                first = plan_ref[PLAN_START + e] + count
                end = first + pad

                def single_row(r, cc):
                    act(pltpu.make_async_copy(zero_ref.at[pl.ds(0, 1)], xs_out.at[pl.ds(first + r, 1)], zero_sem))
                    return cc

                lax.fori_loop(0, pad & 7, single_row, 0)
                for k in PAD_PIECES:
                    @pl.when((pad & k) != 0)
                    def _():
                        at = pl.multiple_of(end - (pad & ~(k - 1)), 8)
                        act(pltpu.make_async_copy(zero_ref.at[pl.ds(0, k)], xs_out.at[pl.ds(at, k)], zero_sem))
                return c
            lax.fori_loop(0, N_EXPERTS, per_expert, 0)

        def for_each_unused_block(act):
            def per_block(b, c):
                act(pltpu.make_async_copy(zero_ref, xs_out.at[pl.ds(b * MOE_ROWS, MOE_ROWS)], zero_sem))
                return c
            lax.fori_loop(plan_ref[PLAN_USED], nb, per_block, 0)

        for_each_pad_piece(lambda cp: cp.start())
        for_each_unused_block(lambda cp: cp.start())
        for_each_pad_piece(lambda cp: cp.wait())
        for_each_unused_block(lambda cp: cp.wait())

    for _ in range(2):
        pltpu.make_async_copy(h_ref, xs_out.at[pl.ds(0, rows)], sem).wait()


def _moe_scatter_call(dest_flat, plan, h2):
    L = h2.shape[0]
    rows = min(256, L)
    slots = _moe_slots(L)
    return pl.pallas_call(
        functools.partial(_moe_scatter_kernel, rows=rows, nb=slots // MOE_ROWS),
        grid_spec=pltpu.PrefetchScalarGridSpec(
            num_scalar_prefetch=2,
            grid=(L // rows,),
            in_specs=[pl.BlockSpec((rows, D_MODEL), lambda i, d, p: (i, 0))],
            out_specs=pl.BlockSpec(memory_space=pl.ANY),
            scratch_shapes=[pltpu.VMEM((MOE_ROWS, D_MODEL), F32), pltpu.SemaphoreType.DMA(()),
                            pltpu.SemaphoreType.DMA(())],
        ),
        out_shape=jax.ShapeDtypeStruct((slots, D_MODEL), F32),
        compiler_params=_params(("arbitrary",)),
        name="moe_scatter",
    )(dest_flat, plan, h2)


def _moe_ffn_kernel(be_ref, x_ref, wg_ref, wu_ref, wd_ref, y_ref, *, nb):
    b = pl.program_id(0)

    @pl.when(b < be_ref[nb])
    def _():
        xb = x_ref[...].astype(BF16)
        gate = _dot(xb, wg_ref[...].astype(BF16))
        up = _dot(xb, wu_ref[...].astype(BF16))
        hidden = (gate * _sigmoid(gate) * up).astype(BF16)
        y_ref[...] = _dot(hidden, wd_ref[...].astype(BF16))

    @pl.when(b >= be_ref[nb])
    def _():
        y_ref[...] = jnp.zeros_like(y_ref)


def _moe_ffn_call(blk_e, xs, w_gate, w_up, w_down):
    P = xs.shape[0]
    nb = P // MOE_ROWS
    return pl.pallas_call(
        functools.partial(_moe_ffn_kernel, nb=nb),
        grid_spec=pltpu.PrefetchScalarGridSpec(
            num_scalar_prefetch=1,
            grid=(nb,),
            in_specs=[
                pl.BlockSpec((MOE_ROWS, D_MODEL), lambda b, be: (jnp.minimum(b, be[nb] - 1), 0)),
                pl.BlockSpec((None, D_MODEL, EXPERT_HIDDEN), lambda b, be: (be[b], 0, 0)),
                pl.BlockSpec((None, D_MODEL, EXPERT_HIDDEN), lambda b, be: (be[b], 0, 0)),
                pl.BlockSpec((None, EXPERT_HIDDEN, D_MODEL), lambda b, be: (be[b], 0, 0)),
            ],
            out_specs=pl.BlockSpec((MOE_ROWS, D_MODEL), lambda b, be: (b, 0)),
        ),
        out_shape=jax.ShapeDtypeStruct((P, D_MODEL), F32),
        compiler_params=_params(("arbitrary",), vmem_mb=56),
        name="moe_ffn",
    )(blk_e, xs, w_gate, w_up, w_down)


def _moe_combine_kernel(dest_ref, x1_ref, rt_ref, nf_ref, ys_hbm, o_ref, ya_ref, yb_ref, sem, *, rows):
    step, n_steps = pl.program_id(0), pl.num_programs(0)

    def gather(s, act):
        slot = s % 2

        def body(r, c):
            for j, dst in enumerate((ya_ref, yb_ref)):
                src = ys_hbm.at[pl.ds(dest_ref[2 * (s * rows + r) + j], 1)]
                act(pltpu.make_async_copy(src, dst.at[slot, pl.ds(r, 1)], sem.at[slot]))
            return c

        lax.fori_loop(0, rows, body, 0, unroll=8)

    @pl.when(step == 0)
    def _():
        gather(step, lambda cp: cp.start())

    @pl.when(step + 1 < n_steps)
    def _():
        gather(step + 1, lambda cp: cp.start())

    slot = step % 2
    for dst in (ya_ref, yb_ref):
        pltpu.make_async_copy(ys_hbm.at[pl.ds(0, rows)], dst.at[slot], sem.at[slot]).wait()
    rt = rt_ref[...]
    x = x1_ref[...] + rt[:, RT_W0:RT_W0 + 1] * ya_ref[slot] + rt[:, RT_W1:RT_W1 + 1] * yb_ref[slot]
    o_ref[...] = x * lax.rsqrt(jnp.mean(x * x, axis=-1, keepdims=True) + EPS) * nf_ref[...]


def _moe_combine_call(dest_flat, x1, route, norm_final, ys):
    L = x1.shape[0]
    rows = min(256, L)
    return pl.pallas_call(
        functools.partial(_moe_combine_kernel, rows=rows),
        grid_spec=pltpu.PrefetchScalarGridSpec(
            num_scalar_prefetch=1,
            grid=(L // rows,),
            in_specs=[
                pl.BlockSpec((rows, D_MODEL), lambda i, d: (i, 0)),
                pl.BlockSpec((rows, LANES), lambda i, d: (i, 0)),
                pl.BlockSpec((1, D_MODEL), lambda i, d: (0, 0)),
                pl.BlockSpec(memory_space=pl.ANY),
            ],
            out_specs=pl.BlockSpec((rows, D_MODEL), lambda i, d: (i, 0)),
            scratch_shapes=[pltpu.VMEM((2, rows, D_MODEL), F32), pltpu.VMEM((2, rows, D_MODEL), F32),
                            pltpu.SemaphoreType.DMA((2,))],
        ),
        out_shape=jax.ShapeDtypeStruct((L, D_MODEL), F32),
        compiler_params=_params(("arbitrary",)),
        name="moe_combine",
    )(dest_flat, x1, route, norm_final, ys)


def _pack_w_in(w):
    s = np.cumsum((0, QKV_COLS, DN_HEADS * DN_DV, DN_HEADS, DN_HEADS, ATT_HEADS * ATT_DIM, ATT_HEADS * ATT_DIM,
                   ATT_HEADS * ATT_DIM, IDX_HEADS * IDX_DIM, IDX_DIM, IDX_HEADS, 2 * D_MODEL))
    seg = lambda n: w[:, int(s[n]):int(s[n + 1])]
    w_main = jnp.concatenate([seg(0), seg(1), seg(4), seg(5), seg(6), seg(7), seg(10)], axis=1).astype(BF16)
    pad = jnp.zeros((w.shape[0], SM_IXK - SM_IXW - IDX_HEADS), w.dtype)
    w_small = jnp.concatenate([seg(2), seg(3), seg(9), pad, seg(8)], axis=1).astype(BF16)
    return w_main, w_small


def _forward(x, positions, norm_mix, w_in, b_gates, dn_conv_w, dn_a_log, dn_dt_bias, dn_norm_w, idx_k_norm,
             w_proj_dn, w_proj_att, w_out, norm_ffn, w_group, b_group, w_router, b_router, w_exp_gate, w_exp_up,
             w_exp_down, norm_final):
    st = {}
    L = x.shape[1]
    x2 = x.reshape(L, D_MODEL)
    w_main, w_small = _pack_w_in(w_in[0])
    proj, small = _proj_call(x2, norm_mix[0].reshape(1, D_MODEL), w_main, w_small)
    st["proj"], st["small"] = proj, small
    q, k, v, g, beta = _gdn_prep_call(proj, small, dn_conv_w[0], dn_a_log[0].reshape(1, DN_HEADS),
                                      dn_dt_bias[0].reshape(1, DN_HEADS))
    st["y_dn"] = _gdn_chunk_call(q, k, v, proj, g, g.T, beta, dn_norm_w[0].reshape(1, DN_DV))

    pos_col = positions.reshape(L, 1).astype(F32)
    kn_lanes = jnp.concatenate([jnp.zeros((SM_IXK,), F32), idx_k_norm[0].astype(F32)]).reshape(1, LANES)
    aq, ak, avt, iq, ik, wt, qn, kn = _dsa_prep_call(proj, small, pos_col, kn_lanes)
    st["y_at"] = _dsa_call(aq, ak, avt, iq, ik, wt, qn, kn, min(TOPK_MAX, L // 4))

    merged = _merge_call(proj, b_gates[0].reshape(1, 2 * D_MODEL), st["y_dn"], st["y_at"],
                         w_proj_dn[0].astype(BF16), w_proj_att[0].astype(BF16))
    st["merged"] = merged
    n_route = N_GROUPS + N_EXPERTS
    w_route = jnp.concatenate([w_group[0], w_router[0], jnp.zeros((D_MODEL, LANES - n_route), F32)], axis=1).astype(BF16)
    b_route = jnp.concatenate([b_group[0], b_router[0], jnp.zeros((LANES - n_route,), F32)]).reshape(1, LANES)
    x1, h2, route = _outproj_call(x2, merged, w_out[0].astype(BF16), norm_ffn[0].reshape(1, D_MODEL), w_route, b_route)
    st["x1"], st["h2"], st["route"] = x1, h2, route

    dest, blk = _moe_plan_call(route)
    dest_flat = dest[:, :2].reshape(2 * L)
    slots = _moe_slots(L)
    nb = slots // MOE_ROWS
    blk_e = jnp.concatenate([blk[:nb, 0], blk[0:1, 1]])
    seg = blk[blk.shape[0] - 8:blk.shape[0] - 5, :N_EXPERTS]
    plan = jnp.concatenate([seg.reshape(3 * N_EXPERTS), blk[0:1, 1]])
    xs = _moe_scatter_call(dest_flat, plan, h2)
    ys = _moe_ffn_call(blk_e, xs, w_exp_gate[0], w_exp_up[0], w_exp_down[0])
    out = _moe_combine_call(dest_flat, x1, route, norm_final.reshape(1, D_MODEL), ys)
    st["out"] = out.reshape(1, L, D_MODEL)
    return st


def kernel(x, positions, norm_mix, w_in, b_gates, dn_conv_w, dn_a_log, dn_dt_bias, dn_norm_w, idx_k_norm, w_proj_dn,
           w_proj_att, w_out, norm_ffn, w_group, b_group, w_router, b_router, w_exp_gate, w_exp_up, w_exp_down,
           norm_final):
    return _forward(x, positions, norm_mix, w_in, b_gates, dn_conv_w, dn_a_log, dn_dt_bias, dn_norm_w, idx_k_norm,
                    w_proj_dn, w_proj_att, w_out, norm_ffn, w_group, b_group, w_router, b_router, w_exp_gate,
                    w_exp_up, w_exp_down, norm_final)["out"]


def _stages(d, upto=None):
    return _forward(*[d[n] for n in ("x", "positions", "norm_mix", "w_in", "b_gates", "dn_conv_w", "dn_a_log",
                                     "dn_dt_bias", "dn_norm_w", "idx_k_norm", "w_proj_dn", "w_proj_att", "w_out",
                                     "norm_ffn", "w_group", "b_group", "w_router", "b_router", "w_exp_gate",
                                     "w_exp_up", "w_exp_down", "norm_final")])
```
